```python
import jax
import jax.numpy as jnp
from jax import lax
import numpy as np

D_MODEL = 4096
BATCH = 4
SEQ = 2048
DEPTH = 4

CTX_LEN = 256
GRID_W = 64
HEAD_DIM = 128
GLA_HEADS = 4
GLA_DK = 128
GLA_DV = 256
GLA_GATE_RANK = 16
GLA_GATE_NORM = 16.0
GLA_CHUNK = 64
SWA_HEADS = 8
SWA_KV_HEADS = 2
SWA_WINDOW = 128
SWA_BLOCK = 128
NA_HEADS = 8
NA_KH_MAX = 8
NA_KW = 16
N_BRANCH = 3
BRANCH_W = 1024
D_FF = 4096
MACARON_W = 0.5
ADA_RANK = 256
N_MOD = 9
ROPE_BASE = 10000.0
EPS = 1e-6
NEG_INF = -1e30
F32 = jnp.float32

SPLITS = (
    GLA_HEADS * GLA_DK, GLA_HEADS * GLA_DK, GLA_HEADS * GLA_DV, GLA_HEADS * GLA_DV, 2 * GLA_GATE_RANK,
    SWA_HEADS * HEAD_DIM, SWA_KV_HEADS * HEAD_DIM, SWA_KV_HEADS * HEAD_DIM,
    NA_HEADS * HEAD_DIM, NA_HEADS * HEAD_DIM, NA_HEADS * HEAD_DIM,
    N_BRANCH * D_MODEL,
)
IN_COLS = sum(SPLITS)
SPLIT_IDX = tuple(sum(SPLITS[:i + 1]) for i in range(len(SPLITS) - 1))

kernel_name = 'hybrid_gla_swa_natten_macaron_dit'


def rms_norm(x, gain):
    xf = x.astype(F32)
    y = xf * lax.rsqrt(jnp.mean(xf * xf, axis=-1, keepdims=True) + EPS)
    return (y * gain.astype(F32)).astype(x.dtype)


def modulate(h, gain, shift, scale):
    return rms_norm(h, gain) * (1.0 + scale) + shift


def swiglu(x, w_in, w_out):
    a, b = jnp.split(x @ w_in, 2, axis=-1)
    return (jax.nn.silu(a) * b) @ w_out


def adaln(v, down, up, bias):
    m = (jax.nn.silu(v) @ down) @ up + bias
    return m.reshape(m.shape[0], N_MOD, -1)


def half_ffn(h, mod, k, gain, w_in, w_out):
    shift, scale, gate = mod[:, 3 * k, None], mod[:, 3 * k + 1, None], mod[:, 3 * k + 2, None]
    return h + MACARON_W * gate * swiglu(modulate(h, gain, shift, scale), w_in, w_out)


def split_heads(a, n):
    return a.reshape(a.shape[0], a.shape[1], n, -1)


def axial_rope(x):
    t, dh = x.shape[1], x.shape[-1]
    half, quarter = dh // 2, dh // 4
    pos = jnp.arange(t)
    rows = (pos // GRID_W).astype(F32)
    cols = (pos % GRID_W).astype(F32)
    inv = ROPE_BASE ** (-jnp.arange(quarter, dtype=F32) / quarter)

    def rot(xa, p):
        ang = p[:, None] * inv[None, :]
        cos, sin = jnp.cos(ang)[None, :, None, :], jnp.sin(ang)[None, :, None, :]
        x1, x2 = xa[..., :quarter], xa[..., quarter:]
        return jnp.concatenate([x1 * cos - x2 * sin, x1 * sin + x2 * cos], axis=-1)

    xf = x.astype(F32)
    return jnp.concatenate([rot(xf[..., :half], rows), rot(xf[..., half:], cols)], axis=-1).astype(x.dtype)


def softmax_sink(s, sink, n_kv, n_g):
    sink_b = sink.astype(F32).reshape((1, n_kv, n_g) + (1,) * (s.ndim - 3))
    s = jnp.concatenate([s, jnp.broadcast_to(sink_b, s.shape[:-1] + (1,))], axis=-1)
    return jax.nn.softmax(s, axis=-1)[..., :-1]


def dense_attn(q, k, v, sink):
    b, l, hq, dh = q.shape
    hkv = k.shape[2]
    g = hq // hkv
    s = jnp.einsum('blkgd,bckd->bkglc', q.reshape(b, l, hkv, g, dh), k).astype(F32) * dh ** -0.5
    p = jax.nn.softmax(s, axis=-1) if sink is None else softmax_sink(s, sink, hkv, g)
    o = jnp.einsum('bkglc,bckd->blkgd', p.astype(v.dtype), v)
    return o.reshape(b, l, hq * dh)


def gla_chunk(q, k, v, g, s0):
    b_, h_, t, _ = k.shape
    nc = t // GLA_CHUNK
    blk = lambda a: a.reshape(b_, h_, nc, GLA_CHUNK, a.shape[-1])
    k, v, g = blk(k), blk(v), blk(g)
    cum = jnp.cumsum(g, axis=3)
    cum_end = cum[:, :, :, -1:, :]
    kv_blk = jnp.einsum('bhncd,bhncv->bhndv', k * jnp.exp(cum_end - cum), v)
    decay = jnp.exp(cum_end[:, :, :, 0, :])

    def step(s, inp):
        dec, kv = inp
        return dec[..., None] * s + kv, s

    s_fin, s_start = lax.scan(step, s0, (jnp.moveaxis(decay, 2, 0), jnp.moveaxis(kv_blk, 2, 0)))
    if q is None:
        return None, s_fin
    s_start = jnp.moveaxis(s_start, 0, 2)
    qe = blk(q) * jnp.exp(cum)
    a = jnp.einsum('bhncd,bhnsd->bhncs', qe, k * jnp.exp(-cum))
    a = jnp.where(jnp.tril(jnp.ones((GLA_CHUNK, GLA_CHUNK), dtype=bool)), a, 0.0)
    o = jnp.einsum('bhncs,bhnsv->bhncv', a, v) + jnp.einsum('bhncd,bhndv->bhncv', qe, s_start)
    return o.reshape(b_, h_, t, v.shape[-1]), s_fin


def gla_kvg(p, gate_up, gate_bias):
    k, v, gd = p[1], p[2], p[4]
    b, t, _ = k.shape
    to_bhtd = lambda a: split_heads(a, GLA_HEADS).astype(F32).transpose(0, 2, 1, 3)
    z = jnp.einsum('btdr,drk->dbtk', gd.reshape(b, t, 2, GLA_GATE_RANK).astype(F32), gate_up.astype(F32))
    z = z + gate_bias.astype(F32)[:, None, None, :]
    lg = jax.nn.log_sigmoid(z) / GLA_GATE_NORM
    lg = lg.reshape(2, b, t, GLA_HEADS, GLA_DK).transpose(0, 1, 3, 2, 4)
    return to_bhtd(k), to_bhtd(v), lg


def gla_q(p):
    return split_heads(p[0], GLA_HEADS).astype(F32).transpose(0, 2, 1, 3) * GLA_DK ** -0.5


def gla_out(o, og, gain):
    b, h, t, dv = o.shape
    y = rms_norm(o.transpose(0, 2, 1, 3), gain) * jax.nn.silu(split_heads(og, GLA_HEADS).astype(F32))
    return y.reshape(b, t, h * dv).astype(og.dtype)


def window_attn(q, k, v, kc, vc, sink):
    b, n, _, dh = q.shape
    nb = n // SWA_BLOCK
    g = SWA_HEADS // SWA_KV_HEADS
    qb = q.reshape(b, nb, SWA_BLOCK, SWA_KV_HEADS, g, dh)

    def band(a):
        a = jnp.pad(a, ((0, 0), (SWA_BLOCK, SWA_BLOCK), (0, 0), (0, 0)))
        a = a.reshape(b, nb + 2, SWA_BLOCK, SWA_KV_HEADS, dh)
        return jnp.concatenate([a[:, :-2], a[:, 1:-1], a[:, 2:]], axis=2)

    kw, vw = band(k), band(v)
    scale = dh ** -0.5
    s_loc = jnp.einsum('bnqkgd,bnskd->bkgnqs', qb, kw).astype(F32) * scale
    s_ctx = jnp.einsum('bnqkgd,bckd->bkgnqc', qb, kc).astype(F32) * scale
    start = jnp.arange(nb)[:, None] * SWA_BLOCK
    qpos = start + jnp.arange(SWA_BLOCK)[None, :]
    kpos = start - SWA_BLOCK + jnp.arange(3 * SWA_BLOCK)[None, :]
    kp = kpos[:, None, :]
    valid = (jnp.abs(kp - qpos[:, :, None]) <= SWA_WINDOW) & (kp >= 0) & (kp < n)
    s_loc = jnp.where(valid, s_loc, NEG_INF)
    p = softmax_sink(jnp.concatenate([s_loc, s_ctx], axis=-1), sink, SWA_KV_HEADS, g).astype(v.dtype)
    n_loc = 3 * SWA_BLOCK
    o = jnp.einsum('bkgnqs,bnskd->bnqkgd', p[..., :n_loc], vw)
    o = o + jnp.einsum('bkgnqc,bckd->bnqkgd', p[..., n_loc:], vc)
    return o.reshape(b, n, SWA_HEADS * dh)


def neighbourhood_attn(q, k, v, kc, vc, rpb):
    b, n, h, dh = q.shape
    rows = n // GRID_W
    kh = min(NA_KH_MAX, rows)
    ncb = GRID_W // NA_KW
    kbw = 2 * NA_KW
    r = jnp.arange(rows)
    row_idx = jnp.clip(r - kh // 2, 0, rows - kh)[:, None] + jnp.arange(kh)[None, :]
    col_start = jnp.clip(jnp.arange(GRID_W) - NA_KW // 2, 0, GRID_W - NA_KW)
    blk_start = jnp.clip(jnp.arange(ncb) * NA_KW - NA_KW // 2, 0, GRID_W - kbw)
    col_idx = blk_start[:, None] + jnp.arange(kbw)[None, :]
    qcol = jnp.arange(ncb)[:, None] * NA_KW + jnp.arange(NA_KW)[None, :]

    def gather(a):
        a = a.reshape(b, rows, GRID_W, h, dh)[:, row_idx[:, None, :, None], col_idx[None, :, None, :]]
        return a.reshape(b, rows, ncb, kh * kbw, h, dh)

    kg, vg = gather(k), gather(v)
    qg = q.reshape(b, rows, ncb, NA_KW, h, dh)
    scale = dh ** -0.5
    s_loc = jnp.einsum('brjqhd,brjshd->bhrjqs', qg, kg).astype(F32) * scale
    s_ctx = jnp.einsum('brjqhd,bchd->bhrjqc', qg, kc).astype(F32) * scale
    cs = col_start[qcol][:, :, None]
    kcol = col_idx[:, None, :]
    valid_c = (kcol >= cs) & (kcol < cs + NA_KW)
    valid = jnp.broadcast_to(valid_c[:, :, None, :], (ncb, NA_KW, kh, kbw)).reshape(ncb, NA_KW, kh * kbw)
    dr = row_idx - r[:, None] + NA_KH_MAX - 1
    dc = jnp.clip(kcol - qcol[:, :, None] + NA_KW - 1, 0, 2 * NA_KW - 2)
    bias = rpb.astype(F32)[:, dr[:, None, None, :, None], dc[None, :, :, None, :]]
    bias = bias.reshape(h, rows, ncb, NA_KW, kh * kbw)
    s_loc = jnp.where(valid, s_loc + bias[None], NEG_INF)
    p = jax.nn.softmax(jnp.concatenate([s_loc, s_ctx], axis=-1), axis=-1).astype(v.dtype)
    n_loc = kh * kbw
    o = jnp.einsum('bhrjqs,brjshd->brjqhd', p[..., :n_loc], vg)
    o = o + jnp.einsum('bhrjqc,bchd->brjqhd', p[..., n_loc:], vc)
    return o.reshape(b, n, h * dh)


def merge(branches, gate_logits, w_branch, w_out):
    gates = jnp.split(jax.nn.sigmoid(gate_logits.astype(F32)), N_BRANCH, axis=-1)
    y = sum(gt * (o @ w_branch[i]) for i, (gt, o) in enumerate(zip(gates, branches)))
    return y.astype(gate_logits.dtype) @ w_out


def token_mixer(xl, xc, w_in, gla_gate_up, gla_gate_bias, gla_norm, swa_q_norm, swa_k_norm, swa_sink,
                na_q_norm, na_k_norm, na_rpb, w_branch, w_out, with_ctx_out):
    n_parts = len(SPLITS) if with_ctx_out else len(SPLITS) - 1
    ctx_cols = IN_COLS if with_ctx_out else SPLIT_IDX[-1]
    pl = jnp.split(xl @ w_in, SPLIT_IDX, axis=-1)
    pc = jnp.split(xc @ w_in[:, :ctx_cols], SPLIT_IDX[:n_parts - 1], axis=-1)
    flip = lambda a: jnp.flip(a, axis=2)

    lk, lv, lg = gla_kvg(pl, gla_gate_up, gla_gate_bias)
    ck, cv, cg = gla_kvg(pc, gla_gate_up, gla_gate_bias)
    lq = gla_q(pl)
    cq = gla_q(pc) if with_ctx_out else None
    s0 = jnp.zeros(ck.shape[:2] + (GLA_DK, GLA_DV), F32)
    oc_f, sc_f = gla_chunk(cq, ck, cv, cg[0], s0)
    oc_b, sc_b = gla_chunk(None if cq is None else flip(cq), flip(ck), flip(cv), flip(cg[1]), s0)
    ol_f, _ = gla_chunk(lq, lk, lv, lg[0], sc_f)
    ol_b, _ = gla_chunk(flip(lq), flip(lk), flip(lv), flip(lg[1]), sc_b)
    a_lat = gla_out(ol_f + flip(ol_b), pl[3], gla_norm)

    sq = axial_rope(rms_norm(split_heads(pl[5], SWA_HEADS), swa_q_norm))
    sk = axial_rope(rms_norm(split_heads(pl[6], SWA_KV_HEADS), swa_k_norm))
    sv = split_heads(pl[7], SWA_KV_HEADS)
    skc = rms_norm(split_heads(pc[6], SWA_KV_HEADS), swa_k_norm)
    svc = split_heads(pc[7], SWA_KV_HEADS)
    b_lat = window_attn(sq, sk, sv, skc, svc, swa_sink)

    nq = rms_norm(split_heads(pl[8], NA_HEADS), na_q_norm)
    nk = rms_norm(split_heads(pl[9], NA_HEADS), na_k_norm)
    nv = split_heads(pl[10], NA_HEADS)
    nkc = rms_norm(split_heads(pc[9], NA_HEADS), na_k_norm)
    nvc = split_heads(pc[10], NA_HEADS)
    c_lat = neighbourhood_attn(nq, nk, nv, nkc, nvc, na_rpb)

    y_lat = merge((a_lat, b_lat, c_lat), pl[11], w_branch, w_out)
    if not with_ctx_out:
        return y_lat, None
    a_ctx = gla_out(oc_f + flip(oc_b), pc[3], gla_norm)
    b_ctx = dense_attn(rms_norm(split_heads(pc[5], SWA_HEADS), swa_q_norm), skc, svc, swa_sink)
    c_ctx_out = dense_attn(rms_norm(split_heads(pc[8], NA_HEADS), na_q_norm), nkc, nvc, None)
    y_ctx = merge((a_ctx, b_ctx, c_ctx_out), pc[11], w_branch, w_out)
    return y_lat, y_ctx


def setup_inputs(seed: int = 0) -> dict:
    key = jax.random.key(seed)
    ks = jax.random.split(key, 24)
    nrm = lambda k, shape, s: jax.random.normal(k, shape, F32) * s
    return {
        'x': nrm(ks[0], (BATCH, SEQ, D_MODEL), 1.0),
        'c': nrm(ks[1], (BATCH, D_MODEL), 1.0),
        'ctx': nrm(ks[2], (BATCH, CTX_LEN, D_MODEL), 1.0),
        'c_ctx': nrm(ks[3], (D_MODEL,), 1.0),
        'ada_down': nrm(ks[4], (DEPTH, D_MODEL, ADA_RANK), D_MODEL ** -0.5),
        'ada_up': nrm(ks[5], (DEPTH, ADA_RANK, N_MOD * D_MODEL), 0.5 * ADA_RANK ** -0.5),
        'ada_bias': nrm(ks[6], (DEPTH, N_MOD * D_MODEL), 0.02),
        'norm_gain': 1.0 + nrm(ks[7], (DEPTH, 3, D_MODEL), 0.02),
        'ffn_w_in': nrm(ks[8], (DEPTH, 2, D_MODEL, 2 * D_FF), D_MODEL ** -0.5),
        'ffn_w_out': nrm(ks[9], (DEPTH, 2, D_FF, D_MODEL), D_FF ** -0.5),
        'w_in': nrm(ks[10], (DEPTH, D_MODEL, IN_COLS), D_MODEL ** -0.5),
        'gla_gate_up': nrm(ks[11], (DEPTH, 2, GLA_GATE_RANK, GLA_HEADS * GLA_DK), GLA_GATE_RANK ** -0.5),
        'gla_gate_bias': nrm(ks[12], (DEPTH, 2, GLA_HEADS * GLA_DK), 0.1),
        'gla_norm': 1.0 + nrm(ks[13], (DEPTH, GLA_DV), 0.02),
        'swa_q_norm': 1.0 + nrm(ks[14], (DEPTH, HEAD_DIM), 0.02),
        'swa_k_norm': 1.0 + nrm(ks[15], (DEPTH, HEAD_DIM), 0.02),
        'swa_sink': nrm(ks[16], (DEPTH, SWA_HEADS), 0.5),
        'na_q_norm': 1.0 + nrm(ks[17], (DEPTH, HEAD_DIM), 0.02),
        'na_k_norm': 1.0 + nrm(ks[18], (DEPTH, HEAD_DIM), 0.02),
        'na_rpb': nrm(ks[19], (DEPTH, NA_HEADS, 2 * NA_KH_MAX - 1, 2 * NA_KW - 1), 0.1),
        'w_branch': nrm(ks[20], (DEPTH, N_BRANCH, BRANCH_W, D_MODEL), BRANCH_W ** -0.5),
        'w_out': nrm(ks[21], (DEPTH, D_MODEL, D_MODEL), D_MODEL ** -0.5),
    }


def reference(x, c, ctx, c_ctx, ada_down, ada_up, ada_bias, norm_gain, ffn_w_in, ffn_w_out, w_in,
              gla_gate_up, gla_gate_bias, gla_norm, swa_q_norm, swa_k_norm, swa_sink, na_q_norm,
              na_k_norm, na_rpb, w_branch, w_out):
    h, hc = x, ctx
    for l in range(DEPTH):
        last = l == DEPTH - 1
        mod_l = adaln(c, ada_down[l], ada_up[l], ada_bias[l])
        mod_c = adaln(c_ctx[None], ada_down[l], ada_up[l], ada_bias[l])
        h = half_ffn(h, mod_l, 0, norm_gain[l, 0], ffn_w_in[l, 0], ffn_w_out[l, 0])
        hc = half_ffn(hc, mod_c, 0, norm_gain[l, 0], ffn_w_in[l, 0], ffn_w_out[l, 0])
        xl = modulate(h, norm_gain[l, 1], mod_l[:, 3, None], mod_l[:, 4, None])
        xc = modulate(hc, norm_gain[l, 1], mod_c[:, 3, None], mod_c[:, 4, None])
        y_lat, y_ctx = token_mixer(xl, xc, w_in[l], gla_gate_up[l], gla_gate_bias[l], gla_norm[l],
                                   swa_q_norm[l], swa_k_norm[l], swa_sink[l], na_q_norm[l], na_k_norm[l],
                                   na_rpb[l], w_branch[l], w_out[l], not last)
        h = h + mod_l[:, 5, None] * y_lat
        h = half_ffn(h, mod_l, 2, norm_gain[l, 2], ffn_w_in[l, 1], ffn_w_out[l, 1])
        if not last:
            hc = hc + mod_c[:, 5, None] * y_ctx
            hc = half_ffn(hc, mod_c, 2, norm_gain[l, 2], ffn_w_in[l, 1], ffn_w_out[l, 1])
    return h
```

```python
import functools

import jax
import jax.numpy as jnp
import numpy as np
from jax import lax
from jax.experimental import pallas as pl
from jax.experimental.pallas import tpu as pltpu

F32 = jnp.float32
BF16 = jnp.bfloat16
HIGHEST = lax.Precision.HIGHEST

D_MODEL = 4096
SEQ = 2048
CTX_LEN = 256
GRID_W = 64
HEAD_DIM = 128
GLA_HEADS = 4
GLA_DK = 128
GLA_DV = 256
GLA_GATE_RANK = 16
GLA_GATE_NORM = 16.0
GLA_CHUNK = 64
SWA_HEADS = 8
SWA_KV_HEADS = 2
SWA_GROUP = SWA_HEADS // SWA_KV_HEADS
SWA_WINDOW = 128
NA_HEADS = 8
NA_KH = 8
NA_KW = 16
N_BRANCH = 3
BRANCH_W = 1024
D_FF = 4096
MACARON_W = 0.5
N_MOD = 9
ROPE_BASE = 10000.0
EPS = 1e-6
NEG_INF = -1e30
MOD_ROWS = 8

C_GQ, C_GK, C_GV, C_GO = 0, 512, 1024, 2048
C_SQ, C_NQ, C_NK, C_NV = 3072, 4096, 5120, 6144
C_SK, C_SV, C_GD = 7168, 7424, 7680
C_GATE = 8192
P_COLS = C_GATE + N_BRANCH * D_MODEL

V7X_VMEM_BYTES = 64 * 1024 * 1024
V7X_VMEM_RESERVE = 6 * 1024 * 1024
QBLK = 128
NA_SLAB_ROWS = 10


def _vmem_limit(pipelined_bytes, scratch_bytes=0, temp_bytes=0):
    need = 2 * pipelined_bytes + scratch_bytes + temp_bytes + (4 << 20)
    return int(min(max(need, 16 << 20), V7X_VMEM_BYTES - V7X_VMEM_RESERVE))


def _nbytes(shape, dtype):
    return int(np.prod(shape)) * jnp.dtype(dtype).itemsize


def _adaln_kernel(v_ref, down_ref, up_ref, bias_ref, o_ref):
    v = v_ref[...]
    t = jnp.dot(jax.nn.silu(v), down_ref[...], preferred_element_type=F32, precision=HIGHEST)
    o_ref[...] = jnp.dot(t, up_ref[...], preferred_element_type=F32, precision=HIGHEST) + bias_ref[...]


def _adaln(v8, ada_down, ada_up, ada_bias):
    depth, d, rank = ada_down.shape
    blocks = (_nbytes((MOD_ROWS, d), F32) + _nbytes((d, rank), F32) + _nbytes((rank, d), F32)
              + _nbytes((1, d), F32) + _nbytes((MOD_ROWS, d), F32))
    return pl.pallas_call(
        _adaln_kernel,
        grid=(depth, N_MOD),
        in_specs=[
            pl.BlockSpec((MOD_ROWS, d), lambda l, j: (0, 0)),
            pl.BlockSpec((None, d, rank), lambda l, j: (l, 0, 0)),
            pl.BlockSpec((None, rank, d), lambda l, j: (l, 0, j)),
            pl.BlockSpec((None, 1, d), lambda l, j: (l, 0, j)),
        ],
        out_specs=pl.BlockSpec((None, None, MOD_ROWS, d), lambda l, j: (l, j, 0, 0)),
        out_shape=jax.ShapeDtypeStruct((depth, N_MOD, MOD_ROWS, d), F32),
        compiler_params=pltpu.CompilerParams(
            dimension_semantics=("arbitrary", "arbitrary"), vmem_limit_bytes=_vmem_limit(blocks)),
        name="adaln",
    )(v8, ada_down, ada_up, ada_bias.reshape(depth, 1, -1))


def _mod_row_index(i, bm, lat_rows, seq):
    del lat_rows
    return jnp.minimum((i * bm) // seq, MOD_ROWS - 1)


NORM_ROWS = 32


def _norm_prologue(h_ref, g_ref, sh_ref, sc_ref, xn_ref, bm):
    gain = g_ref[...]
    scale1 = 1.0 + sc_ref[...]
    shift = sh_ref[...]

    def body(c, carry):
        r0 = pl.multiple_of(c * NORM_ROWS, NORM_ROWS)
        x = h_ref[pl.ds(r0, NORM_ROWS), :]
        y = x * lax.rsqrt(jnp.mean(x * x, axis=-1, keepdims=True) + EPS)
        xn_ref[pl.ds(r0, NORM_ROWS), :] = ((y * gain) * scale1 + shift).astype(BF16)
        return carry

    lax.fori_loop(0, bm // NORM_ROWS, body, 0)


def _norm_mm_kernel(h_ref, g_ref, sh_ref, sc_ref, w_ref, o_ref, xn_ref, *, bm):
    @pl.when(pl.program_id(1) == 0)
    def _():
        _norm_prologue(h_ref, g_ref, sh_ref, sc_ref, xn_ref, bm)

    o_ref[...] = jnp.dot(xn_ref[...], w_ref[...], preferred_element_type=F32).astype(o_ref.dtype)


def _norm_swiglu_kernel(h_ref, g_ref, sh_ref, sc_ref, wa_ref, wb_ref, o_ref, xn_ref, *, bm):
    @pl.when(pl.program_id(1) == 0)
    def _():
        _norm_prologue(h_ref, g_ref, sh_ref, sc_ref, xn_ref, bm)

    xn = xn_ref[...]
    a = jnp.dot(xn, wa_ref[...], preferred_element_type=F32)
    b = jnp.dot(xn, wb_ref[...], preferred_element_type=F32)
    o_ref[...] = (jax.nn.silu(a) * b).astype(o_ref.dtype)


def _mod_spec(bm, seq, d):
    return pl.BlockSpec((None, 1, d), lambda i, j: (_mod_row_index(i, bm, None, seq), 0, 0))


def _norm_matmul(h, rows, gain, shift, scale, w, *, seq, bm, bn, out_dtype):
    d = h.shape[1]
    n = w.shape[1]
    blocks = (_nbytes((bm, d), F32) + _nbytes((d, bn), BF16) + _nbytes((bm, bn), out_dtype)
              + 3 * _nbytes((1, d), F32))
    return pl.pallas_call(
        functools.partial(_norm_mm_kernel, bm=bm),
        grid=(rows // bm, n // bn),
        in_specs=[
            pl.BlockSpec((bm, d), lambda i, j: (i, 0)),
            pl.BlockSpec((1, d), lambda i, j: (0, 0)),
            _mod_spec(bm, seq, d),
            _mod_spec(bm, seq, d),
            pl.BlockSpec((d, bn), lambda i, j: (0, j)),
        ],
        out_specs=pl.BlockSpec((bm, bn), lambda i, j: (i, j)),
        out_shape=jax.ShapeDtypeStruct((rows, n), out_dtype),
        scratch_shapes=[pltpu.VMEM((bm, d), BF16)],
        compiler_params=pltpu.CompilerParams(
            dimension_semantics=("arbitrary", "arbitrary"),
            vmem_limit_bytes=_vmem_limit(blocks, _nbytes((bm, d), BF16), _nbytes((bm, bn), F32))),
        name="norm_matmul",
    )(h, gain, shift, scale, w)


def _norm_swiglu(h, rows, gain, shift, scale, w_in, *, seq, bm, bn):
    d = h.shape[1]
    f = w_in.shape[1] // 2
    nb = f // bn
    blocks = (_nbytes((bm, d), F32) + 2 * _nbytes((d, bn), BF16) + _nbytes((bm, bn), BF16)
              + 3 * _nbytes((1, d), F32))
    return pl.pallas_call(
        functools.partial(_norm_swiglu_kernel, bm=bm),
        grid=(rows // bm, nb),
        in_specs=[
            pl.BlockSpec((bm, d), lambda i, j: (i, 0)),
            pl.BlockSpec((1, d), lambda i, j: (0, 0)),
            _mod_spec(bm, seq, d),
            _mod_spec(bm, seq, d),
            pl.BlockSpec((d, bn), lambda i, j: (0, j)),
            pl.BlockSpec((d, bn), lambda i, j: (0, j + nb)),
        ],
        out_specs=pl.BlockSpec((bm, bn), lambda i, j: (i, j)),
        out_shape=jax.ShapeDtypeStruct((rows, f), BF16),
        scratch_shapes=[pltpu.VMEM((bm, d), BF16)],
        compiler_params=pltpu.CompilerParams(
            dimension_semantics=("arbitrary", "arbitrary"),
            vmem_limit_bytes=_vmem_limit(blocks, _nbytes((bm, d), BF16), 3 * _nbytes((bm, bn), F32))),
        name="norm_swiglu",
    )(h, gain, shift, scale, w_in, w_in)


def _resid_mm_kernel(x_ref, w_ref, h_ref, gate_ref, o_ref, *, weight):
    y = jnp.dot(x_ref[...], w_ref[...], preferred_element_type=F32)
    if weight != 1.0:
        o_ref[...] = h_ref[...] + weight * gate_ref[...] * y
    else:
        o_ref[...] = h_ref[...] + gate_ref[...] * y


def _resid_matmul(x, w, h, rows, gate, *, weight, seq, bm, bn):
    k = x.shape[1]
    n = w.shape[1]
    blocks = (_nbytes((bm, k), BF16) + _nbytes((k, bn), BF16) + 2 * _nbytes((bm, bn), F32)
              + _nbytes((1, bn), F32))
    return pl.pallas_call(
        functools.partial(_resid_mm_kernel, weight=weight),
        grid=(rows // bm, n // bn),
        in_specs=[
            pl.BlockSpec((bm, k), lambda i, j: (i, 0)),
            pl.BlockSpec((k, bn), lambda i, j: (0, j)),
            pl.BlockSpec((bm, bn), lambda i, j: (i, j)),
            pl.BlockSpec((None, 1, bn), lambda i, j: (_mod_row_index(i, bm, None, seq), 0, j)),
        ],
        out_specs=pl.BlockSpec((bm, bn), lambda i, j: (i, j)),
        out_shape=jax.ShapeDtypeStruct((rows, n), F32),
        compiler_params=pltpu.CompilerParams(
            dimension_semantics=("arbitrary", "arbitrary"),
            vmem_limit_bytes=_vmem_limit(blocks, 0, _nbytes((bm, bn), F32))),
        name="resid_matmul",
    )(x, w, h, gate)


def _merge_kernel(oa_ref, ob_ref, oc_ref, w_ref, ga_ref, gb_ref, gc_ref, y_ref):
    y = jax.nn.sigmoid(ga_ref[...]) * jnp.dot(oa_ref[...], w_ref[0], preferred_element_type=F32)
    y = y + jax.nn.sigmoid(gb_ref[...]) * jnp.dot(ob_ref[...], w_ref[1], preferred_element_type=F32)
    y = y + jax.nn.sigmoid(gc_ref[...]) * jnp.dot(oc_ref[...], w_ref[2], preferred_element_type=F32)
    y_ref[...] = y.astype(y_ref.dtype)


def _merge(oa, ob, oc, w_branch, p, rows, *, bm, bn):
    kb = oa.shape[1]
    d = w_branch.shape[2]
    gate_blk = C_GATE // bn
    per_branch = d // bn
    blocks = (3 * _nbytes((bm, kb), BF16) + _nbytes((N_BRANCH, kb, bn), BF16) + 3 * _nbytes((bm, bn), F32)
              + _nbytes((bm, bn), BF16))
    o_spec = pl.BlockSpec((bm, kb), lambda i, j: (i, 0))

    def gate_spec(br):
        return pl.BlockSpec((bm, bn), lambda i, j: (i, gate_blk + br * per_branch + j))

    return pl.pallas_call(
        _merge_kernel,
        grid=(rows // bm, d // bn),
        in_specs=[o_spec, o_spec, o_spec,
                  pl.BlockSpec((N_BRANCH, kb, bn), lambda i, j: (0, 0, j)),
                  gate_spec(0), gate_spec(1), gate_spec(2)],
        out_specs=pl.BlockSpec((bm, bn), lambda i, j: (i, j)),
        out_shape=jax.ShapeDtypeStruct((rows, d), BF16),
        compiler_params=pltpu.CompilerParams(
            dimension_semantics=("arbitrary", "arbitrary"),
            vmem_limit_bytes=_vmem_limit(blocks, 0, 2 * _nbytes((bm, bn), F32))),
        name="merge",
    )(oa, ob, oc, w_branch, p, p, p)


_NT = (((1,), (1,)), ((), ()))
_TN = (((0,), (0,)), ((), ()))


def _gla_kernel(ql_ref, kl_ref, vl_ref, ogl_ref, gdl_ref, qc_ref, kc_ref, vc_ref, ogc_ref, gdc_ref,
                up_ref, gb_ref, gn_ref, al_ref, ac_ref, lg_ref, of_ref, ob_ref, st_ref, *, t_lat, t_ctx):
    c = GLA_CHUNK
    row = lax.broadcasted_iota(jnp.int32, (c, c), 0)
    col = lax.broadcasted_iota(jnp.int32, (c, c), 1)
    keep = (col <= row, col >= row)
    ones = (keep[0].astype(F32), keep[1].astype(F32))
    q_scale = GLA_DK ** -0.5
    gain = gn_ref[...]

    st_ref[...] = jnp.zeros_like(st_ref)

    def segment(q_ref, k_ref, v_ref, og_ref, gd_ref, out_ref, t):
        nc = t // c
        gate_rows = min(t, 256)

        def gate_body(r, carry):
            r0 = pl.multiple_of(r * gate_rows, gate_rows)
            gd = gd_ref[pl.ds(r0, gate_rows), :]
            for d in range(2):
                z = jnp.dot(gd, up_ref[d], preferred_element_type=F32) + gb_ref[d]
                lg_ref[d, pl.ds(r0, gate_rows), :] = jax.nn.log_sigmoid(z) * (1.0 / GLA_GATE_NORM)
            return carry

        lax.fori_loop(0, t // gate_rows, gate_body, 0)

        def chunk(r0, d):
            q = q_ref[pl.ds(r0, c), :] * q_scale
            k = k_ref[pl.ds(r0, c), :]
            v = v_ref[pl.ds(r0, c), :].astype(BF16)
            lg = lg_ref[d, pl.ds(r0, c), :]
            cum = jnp.dot(ones[d], lg, preferred_element_type=F32, precision=HIGHEST)
            cum_end = cum[c - 1:c, :] if d == 0 else cum[0:1, :]
            qe = (q * jnp.exp(cum)).astype(BF16)
            kinv = (k * jnp.exp(-cum)).astype(BF16)
            kdec = (k * jnp.exp(cum_end - cum)).astype(BF16)
            a = lax.dot_general(qe, kinv, _NT, preferred_element_type=F32)
            a = jnp.where(keep[d], a, 0.0).astype(BF16)
            st = st_ref[d]
            o = jnp.dot(a, v, preferred_element_type=F32)
            o = o + lax.dot_general(qe, st.astype(BF16), _NT, preferred_element_type=F32)
            st_ref[d] = st * jnp.exp(cum_end) + lax.dot_general(v, kdec, _TN, preferred_element_type=F32)
            return o

        def chunk_body(n, carry):
            rf = pl.multiple_of(n * c, c)
            rb = pl.multiple_of((nc - 1 - n) * c, c)
            of_ref[pl.ds(rf, c), :] = chunk(rf, 0)
            ob_ref[pl.ds(rb, c), :] = chunk(rb, 1)
            return carry

        lax.fori_loop(0, nc, chunk_body, 0)

        def out_body(n, carry):
            r0 = pl.multiple_of(n * c, c)
            o = of_ref[pl.ds(r0, c), :] + ob_ref[pl.ds(r0, c), :]
            y = o * lax.rsqrt(jnp.mean(o * o, axis=-1, keepdims=True) + EPS)
            y = (y * gain) * jax.nn.silu(og_ref[pl.ds(r0, c), :])
            out_ref[pl.ds(r0, c), :] = y.astype(out_ref.dtype)
            return carry

        lax.fori_loop(0, nc, out_body, 0)

    segment(qc_ref, kc_ref, vc_ref, ogc_ref, gdc_ref, ac_ref, t_ctx)
    segment(ql_ref, kl_ref, vl_ref, ogl_ref, gdl_ref, al_ref, t_lat)


def _gla(p, gate_up_pad, gate_bias, gla_norm, *, batch):
    t_lat, t_ctx = SEQ, CTX_LEN
    ctx0 = batch * t_lat // t_ctx
    dk, dv = GLA_DK, GLA_DV

    def lat(width, col0):
        return pl.BlockSpec((t_lat, width), lambda b, h: (b, col0 // width + h))

    def ctx(width, col0):
        return pl.BlockSpec((t_ctx, width), lambda b, h: (ctx0 + b, col0 // width + h))

    lat_gd = pl.BlockSpec((t_lat, 128), lambda b, h: (b, C_GD // 128))
    ctx_gd = pl.BlockSpec((t_ctx, 128), lambda b, h: (ctx0 + b, C_GD // 128))
    blocks = ((t_lat + t_ctx) * (3 * dk + 2 * dv) * 4 + _nbytes((2, 128, dk), F32)
              + (t_lat + t_ctx) * dv * 2)
    scratch = _nbytes((2, t_lat, dk), F32) + 2 * _nbytes((t_lat, dv), F32) + _nbytes((2, dv, dk), F32)
    return pl.pallas_call(
        functools.partial(_gla_kernel, t_lat=t_lat, t_ctx=t_ctx),
        grid=(batch, GLA_HEADS),
        in_specs=[lat(dk, C_GQ), lat(dk, C_GK), lat(dv, C_GV), lat(dv, C_GO), lat_gd,
                  ctx(dk, C_GQ), ctx(dk, C_GK), ctx(dv, C_GV), ctx(dv, C_GO), ctx_gd,
                  pl.BlockSpec((2, 128, dk), lambda b, h: (0, 0, h)),
                  pl.BlockSpec((2, 1, dk), lambda b, h: (0, 0, h)),
                  pl.BlockSpec((1, dv), lambda b, h: (0, 0))],
        out_specs=[pl.BlockSpec((t_lat, dv), lambda b, h: (b, h)),
                   pl.BlockSpec((t_ctx, dv), lambda b, h: (b, h))],
        out_shape=[jax.ShapeDtypeStruct((batch * t_lat, GLA_HEADS * dv), BF16),
                   jax.ShapeDtypeStruct((batch * t_ctx, GLA_HEADS * dv), BF16)],
        scratch_shapes=[pltpu.VMEM((2, t_lat, dk), F32), pltpu.VMEM((t_lat, dv), F32),
                        pltpu.VMEM((t_lat, dv), F32), pltpu.VMEM((2, dv, dk), F32)],
        compiler_params=pltpu.CompilerParams(
            dimension_semantics=("arbitrary", "arbitrary"),
            vmem_limit_bytes=_vmem_limit(blocks, scratch, 4 << 20)),
        name="gla",
    )(p, p, p, p, p, p, p, p, p, p, gate_up_pad, gate_bias, gla_norm)


def _rms(x, gain):
    return (x * lax.rsqrt(jnp.mean(x * x, axis=-1, keepdims=True) + EPS)) * gain


def _rope(x, cos, sin_signed):
    lane = lax.broadcasted_iota(jnp.int32, x.shape, 1)
    partner = jnp.where((lane % 64) < 32, pltpu.roll(x, 96, 1), pltpu.roll(x, 32, 1))
    return x * cos + partner * sin_signed


def _q_row_block(b, qb, *, batch, n_lat, n_ctx):
    return jnp.where(qb < n_lat, b * n_lat + qb, batch * n_lat + b * n_ctx + (qb - n_lat))


KV_PREP_ROWS = 256


def _swa_kernel(sink_ref, q_ref, kl_ref, vl_ref, kc_ref, vc_ref, cq_ref, sq_ref, ck_ref, sk_ref,
                qg_ref, kg_ref, o_ref, kn_ref, vb_ref, kcn_ref, vcb_ref, *, t_lat, n_lat):
    kv = pl.program_id(1)
    qb = pl.program_id(2)
    g = SWA_GROUP
    dh = HEAD_DIM
    scale = dh ** -0.5
    k_gain = kg_ref[...]

    @pl.when(qb == 0)
    def _prep():
        def body(r, carry):
            r0 = pl.multiple_of(r * KV_PREP_ROWS, KV_PREP_ROWS)
            rows = pl.ds(r0, KV_PREP_ROWS)
            kn_ref[rows, :] = _rope(_rms(kl_ref[rows, :], k_gain), ck_ref[rows, :], sk_ref[rows, :]).astype(BF16)
            vb_ref[rows, :] = vl_ref[rows, :].astype(BF16)
            return carry

        lax.fori_loop(0, t_lat // KV_PREP_ROWS, body, 0)
        kcn_ref[...] = _rms(kc_ref[...], k_gain).astype(BF16)
        vcb_ref[...] = vc_ref[...].astype(BF16)

    q = q_ref[...]
    cos, sin = cq_ref[...], sq_ref[...]
    q_gain = qg_ref[...]
    heads = [_rope(_rms(q[:, i * dh:(i + 1) * dh], q_gain), cos, sin).astype(BF16) for i in range(g)]
    qs = jnp.concatenate(heads, axis=0)
    head_of_row = lax.broadcasted_iota(jnp.int32, (g * QBLK, 1), 0) // QBLK
    sink = jnp.full((g * QBLK, 1), sink_ref[kv * g], F32)
    for i in range(1, g):
        sink = jnp.where(head_of_row == i, sink_ref[kv * g + i], sink)

    s_ctx = lax.dot_general(qs, kcn_ref[...], _NT, preferred_element_type=F32) * scale
    m_ctx = jnp.maximum(jnp.max(s_ctx, axis=-1, keepdims=True), sink)

    def finish(m, p_loc_sum, o_loc):
        p_ctx = jnp.exp(s_ctx - m)
        den = p_loc_sum + jnp.sum(p_ctx, axis=-1, keepdims=True) + jnp.exp(sink - m)
        o = (o_loc + jnp.dot(p_ctx.astype(BF16), vcb_ref[...], preferred_element_type=F32)) / den
        for i in range(g):
            o_ref[:, i * dh:(i + 1) * dh] = o[i * QBLK:(i + 1) * QBLK, :].astype(o_ref.dtype)

    @pl.when(qb < n_lat)
    def _latent():
        span = 3 * QBLK
        start = pl.multiple_of(jnp.clip((qb - 1) * QBLK, 0, t_lat - span), QBLK)
        s_loc = lax.dot_general(qs, kn_ref[pl.ds(start, span), :], _NT, preferred_element_type=F32) * scale
        qpos = qb * QBLK + lax.broadcasted_iota(jnp.int32, (g * QBLK, span), 0) % QBLK
        kpos = start + lax.broadcasted_iota(jnp.int32, (g * QBLK, span), 1)
        s_loc = jnp.where(jnp.abs(kpos - qpos) <= SWA_WINDOW, s_loc, NEG_INF)
        m = jnp.maximum(m_ctx, jnp.max(s_loc, axis=-1, keepdims=True))
        p_loc = jnp.exp(s_loc - m)
        o_loc = jnp.dot(p_loc.astype(BF16), vb_ref[pl.ds(start, span), :], preferred_element_type=F32)
        finish(m, jnp.sum(p_loc, axis=-1, keepdims=True), o_loc)

    @pl.when(qb >= n_lat)
    def _context():
        finish(m_ctx, 0.0, 0.0)


def _swa(p, sink, cos_tab, sin_tab, q_gain, k_gain, *, batch, with_ctx_out):
    t_lat, t_ctx, dh, g = SEQ, CTX_LEN, HEAD_DIM, SWA_GROUP
    n_lat, n_ctx = t_lat // QBLK, t_ctx // QBLK
    n_q = n_lat + (n_ctx if with_ctx_out else 0)
    rows_out = batch * (t_lat + (t_ctx if with_ctx_out else 0))
    ctx0 = batch * t_lat // t_ctx
    qmap = functools.partial(_q_row_block, batch=batch, n_lat=n_lat, n_ctx=n_ctx)
    blocks = (_nbytes((QBLK, g * dh), F32) + 2 * _nbytes((t_lat + t_ctx, dh), F32) + 2 * _nbytes((QBLK, dh), F32)
              + 2 * _nbytes((t_lat, dh), F32) + _nbytes((QBLK, g * dh), BF16))
    scratch = 2 * _nbytes((t_lat + t_ctx, dh), BF16)
    return pl.pallas_call(
        functools.partial(_swa_kernel, t_lat=t_lat, n_lat=n_lat),
        grid=(batch, SWA_KV_HEADS, n_q),
        in_specs=[
            pl.BlockSpec(memory_space=pltpu.SMEM),
            pl.BlockSpec((QBLK, g * dh), lambda b, kv, qb: (qmap(b, qb), C_SQ // (g * dh) + kv)),
            pl.BlockSpec((t_lat, dh), lambda b, kv, qb: (b, C_SK // dh + kv)),
            pl.BlockSpec((t_lat, dh), lambda b, kv, qb: (b, C_SV // dh + kv)),
            pl.BlockSpec((t_ctx, dh), lambda b, kv, qb: (ctx0 + b, C_SK // dh + kv)),
            pl.BlockSpec((t_ctx, dh), lambda b, kv, qb: (ctx0 + b, C_SV // dh + kv)),
            pl.BlockSpec((QBLK, dh), lambda b, kv, qb: (qb, 0)),
            pl.BlockSpec((QBLK, dh), lambda b, kv, qb: (qb, 0)),
            pl.BlockSpec((t_lat, dh), lambda b, kv, qb: (0, 0)),
            pl.BlockSpec((t_lat, dh), lambda b, kv, qb: (0, 0)),
            pl.BlockSpec((1, dh), lambda b, kv, qb: (0, 0)),
            pl.BlockSpec((1, dh), lambda b, kv, qb: (0, 0)),
        ],
        out_specs=pl.BlockSpec((QBLK, g * dh), lambda b, kv, qb: (qmap(b, qb), kv)),
        out_shape=jax.ShapeDtypeStruct((rows_out, SWA_HEADS * dh), BF16),
        scratch_shapes=[pltpu.VMEM((t_lat, dh), BF16), pltpu.VMEM((t_lat, dh), BF16),
                        pltpu.VMEM((t_ctx, dh), BF16), pltpu.VMEM((t_ctx, dh), BF16)],
        compiler_params=pltpu.CompilerParams(
            dimension_semantics=("arbitrary", "arbitrary", "arbitrary"),
            vmem_limit_bytes=_vmem_limit(blocks, scratch, 8 << 20)),
        name="swa",
    )(sink, p, p, p, p, p, cos_tab, sin_tab, cos_tab, sin_tab, q_gain, k_gain)


def _na_slab_start(qb):
    r = qb * (QBLK // GRID_W)
    rows = SEQ // GRID_W
    return jnp.minimum(jnp.clip(r - NA_KH // 2, 0, rows - NA_KH), rows - NA_SLAB_ROWS)


def _na_bias_variant(qb, n_lat):
    return jnp.where(qb < 2, qb, jnp.where(qb < n_lat - 2, 2, jnp.minimum(qb, n_lat - 1) - (n_lat - 5)))


def _na_kernel(q_ref, kl_ref, vl_ref, kc_ref, vc_ref, bias_ref, qg_ref, kg_ref, o_ref,
               kn_ref, vb_ref, kcn_ref, vcb_ref, *, t_lat, n_lat):
    qb = pl.program_id(2)
    scale = HEAD_DIM ** -0.5
    k_gain = kg_ref[...]

    @pl.when(qb == 0)
    def _prep():
        def body(r, carry):
            r0 = pl.multiple_of(r * KV_PREP_ROWS, KV_PREP_ROWS)
            rows = pl.ds(r0, KV_PREP_ROWS)
            kn_ref[rows, :] = _rms(kl_ref[rows, :], k_gain).astype(BF16)
            vb_ref[rows, :] = vl_ref[rows, :].astype(BF16)
            return carry

        lax.fori_loop(0, t_lat // KV_PREP_ROWS, body, 0)
        kcn_ref[...] = _rms(kc_ref[...], k_gain).astype(BF16)
        vcb_ref[...] = vc_ref[...].astype(BF16)

    qn = _rms(q_ref[...], qg_ref[...]).astype(BF16)
    s_ctx = lax.dot_general(qn, kcn_ref[...], _NT, preferred_element_type=F32) * scale
    m_ctx = jnp.max(s_ctx, axis=-1, keepdims=True)

    def finish(m, p_loc_sum, o_loc):
        p_ctx = jnp.exp(s_ctx - m)
        den = p_loc_sum + jnp.sum(p_ctx, axis=-1, keepdims=True)
        o = (o_loc + jnp.dot(p_ctx.astype(BF16), vcb_ref[...], preferred_element_type=F32)) / den
        o_ref[...] = o.astype(o_ref.dtype)

    @pl.when(qb < n_lat)
    def _latent():
        span = NA_SLAB_ROWS * GRID_W
        start = pl.multiple_of(_na_slab_start(qb) * GRID_W, GRID_W)
        s_loc = lax.dot_general(qn, kn_ref[pl.ds(start, span), :], _NT, preferred_element_type=F32) * scale
        s_loc = s_loc + bias_ref[...]
        m = jnp.maximum(m_ctx, jnp.max(s_loc, axis=-1, keepdims=True))
        p_loc = jnp.exp(s_loc - m)
        o_loc = jnp.dot(p_loc.astype(BF16), vb_ref[pl.ds(start, span), :], preferred_element_type=F32)
        finish(m, jnp.sum(p_loc, axis=-1, keepdims=True), o_loc)

    @pl.when(qb >= n_lat)
    def _context():
        finish(m_ctx, 0.0, 0.0)


def _na(p, bias_tab, q_gain, k_gain, *, batch, with_ctx_out):
    t_lat, t_ctx, dh = SEQ, CTX_LEN, HEAD_DIM
    n_lat, n_ctx = t_lat // QBLK, t_ctx // QBLK
    n_q = n_lat + (n_ctx if with_ctx_out else 0)
    rows_out = batch * (t_lat + (t_ctx if with_ctx_out else 0))
    ctx0 = batch * t_lat // t_ctx
    span = NA_SLAB_ROWS * GRID_W
    qmap = functools.partial(_q_row_block, batch=batch, n_lat=n_lat, n_ctx=n_ctx)
    blocks = (_nbytes((QBLK, dh), F32) + 2 * _nbytes((t_lat + t_ctx, dh), F32) + _nbytes((QBLK, span), F32)
              + _nbytes((QBLK, dh), BF16))
    scratch = 2 * _nbytes((t_lat + t_ctx, dh), BF16)
    return pl.pallas_call(
        functools.partial(_na_kernel, t_lat=t_lat, n_lat=n_lat),
        grid=(batch, NA_HEADS, n_q),
        in_specs=[
            pl.BlockSpec((QBLK, dh), lambda b, h, qb: (qmap(b, qb), C_NQ // dh + h)),
            pl.BlockSpec((t_lat, dh), lambda b, h, qb: (b, C_NK // dh + h)),
            pl.BlockSpec((t_lat, dh), lambda b, h, qb: (b, C_NV // dh + h)),
            pl.BlockSpec((t_ctx, dh), lambda b, h, qb: (ctx0 + b, C_NK // dh + h)),
            pl.BlockSpec((t_ctx, dh), lambda b, h, qb: (ctx0 + b, C_NV // dh + h)),
            pl.BlockSpec((None, None, QBLK, span), lambda b, h, qb: (h, _na_bias_variant(qb, n_lat), 0, 0)),
            pl.BlockSpec((1, dh), lambda b, h, qb: (0, 0)),
            pl.BlockSpec((1, dh), lambda b, h, qb: (0, 0)),
        ],
        out_specs=pl.BlockSpec((QBLK, dh), lambda b, h, qb: (qmap(b, qb), h)),
        out_shape=jax.ShapeDtypeStruct((rows_out, NA_HEADS * dh), BF16),
        scratch_shapes=[pltpu.VMEM((t_lat, dh), BF16), pltpu.VMEM((t_lat, dh), BF16),
                        pltpu.VMEM((t_ctx, dh), BF16), pltpu.VMEM((t_ctx, dh), BF16)],
        compiler_params=pltpu.CompilerParams(
            dimension_semantics=("arbitrary", "arbitrary", "arbitrary"),
            vmem_limit_bytes=_vmem_limit(blocks, scratch, 8 << 20)),
        name="natten",
    )(p, p, p, p, p, bias_tab, q_gain, k_gain)


def _rope_tables():
    quarter = HEAD_DIM // 4
    pos = jnp.arange(SEQ)
    rows = (pos // GRID_W).astype(F32)
    cols = (pos % GRID_W).astype(F32)
    inv = ROPE_BASE ** (-jnp.arange(quarter, dtype=F32) / quarter)
    ang_r = rows[:, None] * inv[None, :]
    ang_c = cols[:, None] * inv[None, :]
    cos = jnp.concatenate([jnp.cos(ang_r), jnp.cos(ang_r), jnp.cos(ang_c), jnp.cos(ang_c)], axis=-1)
    sin = jnp.concatenate([-jnp.sin(ang_r), jnp.sin(ang_r), -jnp.sin(ang_c), jnp.sin(ang_c)], axis=-1)
    cos = jnp.concatenate([cos, jnp.ones((CTX_LEN, HEAD_DIM), F32)], axis=0)
    sin = jnp.concatenate([sin, jnp.zeros((CTX_LEN, HEAD_DIM), F32)], axis=0)
    return cos, sin


def _na_bias_table(rpb):
    rows = SEQ // GRID_W
    n_lat = SEQ // QBLK
    rq_per = QBLK // GRID_W
    reps = np.array([0, 1, 2, n_lat - 2, n_lat - 1])
    r = reps * rq_per
    start = np.minimum(np.clip(r - NA_KH // 2, 0, rows - NA_KH), rows - NA_SLAB_ROWS)
    rq = r[:, None] + np.arange(rq_per)[None, :]
    kr = start[:, None] + np.arange(NA_SLAB_ROWS)[None, :]
    r0 = np.clip(rq - NA_KH // 2, 0, rows - NA_KH)
    valid_r = (kr[:, None, :] >= r0[:, :, None]) & (kr[:, None, :] < r0[:, :, None] + NA_KH)
    dr = np.clip(kr[:, None, :] - rq[:, :, None] + NA_KH - 1, 0, 2 * NA_KH - 2)
    qc = np.arange(GRID_W)
    kc = np.arange(GRID_W)
    cs = np.clip(qc - NA_KW // 2, 0, GRID_W - NA_KW)
    valid_c = (kc[None, :] >= cs[:, None]) & (kc[None, :] < cs[:, None] + NA_KW)
    dc = np.clip(kc[None, :] - qc[:, None] + NA_KW - 1, 0, 2 * NA_KW - 2)
    bias = rpb.astype(F32)[:, dr[:, :, None, :, None], dc[None, None, :, None, :]]
    valid = valid_r[:, :, None, :, None] & valid_c[None, None, :, None, :]
    bias = jnp.where(valid[None], bias, NEG_INF)
    return bias.reshape(rpb.shape[0], len(reps), QBLK, NA_SLAB_ROWS * GRID_W)


def _relayout_w_in(w):
    d = w.shape[0]
    o = np.cumsum([0, 512, 512, 1024, 1024, 32, 1024, 256, 256, 1024, 1024, 1024, 12288])
    gq_go, gd = w[:, o[0]:o[4]], w[:, o[4]:o[5]]
    sq, sk, sv = w[:, o[5]:o[6]], w[:, o[6]:o[7]], w[:, o[7]:o[8]]
    nq, nk, nv = w[:, o[8]:o[9]], w[:, o[9]:o[10]], w[:, o[10]:o[11]]
    gates = w[:, o[11]:o[12]]
    pad = jnp.zeros((d, C_GATE - C_GD - 2 * GLA_GATE_RANK), w.dtype)
    return jnp.concatenate([gq_go, sq, nq, nk, nv, sk, sv, gd, pad, gates], axis=1).astype(BF16)


def _pad_gate_up(gate_up):
    r = gate_up.shape[1]
    out = jnp.zeros((2, 128, gate_up.shape[2]), F32)
    out = out.at[0, 0:r].set(gate_up[0])
    out = out.at[1, r:2 * r].set(gate_up[1])
    return out


BM = 512
BN = 1024
BN_FF = 512
BN_MERGE = 512


def kernel(x, c, ctx, c_ctx, ada_down, ada_up, ada_bias, norm_gain, ffn_w_in, ffn_w_out, w_in,
           gla_gate_up, gla_gate_bias, gla_norm, swa_q_norm, swa_k_norm, swa_sink, na_q_norm,
           na_k_norm, na_rpb, w_branch, w_out):
    batch, seq, d = x.shape
    depth = ada_down.shape[0]
    assert (seq, d, ctx.shape[1]) == (SEQ, D_MODEL, CTX_LEN) and batch + 1 <= MOD_ROWS
    lat_rows = batch * seq
    all_rows = lat_rows + batch * ctx.shape[1]

    v8 = jnp.concatenate([c, c_ctx[None], jnp.zeros((MOD_ROWS - batch - 1, d), F32)], axis=0)
    mods = _adaln(v8, ada_down, ada_up, ada_bias)
    mods = mods.reshape(depth, N_MOD, MOD_ROWS, 1, d)
    cos_tab, sin_tab = _rope_tables()

    h = jnp.concatenate([x.reshape(lat_rows, d), ctx.reshape(-1, d)], axis=0)
    mm = dict(seq=seq, bm=BM)

    for l in range(depth):
        last = l == depth - 1
        rows_out = lat_rows if last else all_rows
        gain = norm_gain[l].reshape(3, 1, d)
        m = mods[l]

        g1 = _norm_swiglu(h, all_rows, gain[0], m[0], m[1], ffn_w_in[l, 0].astype(BF16), bn=BN_FF, **mm)
        h = _resid_matmul(g1, ffn_w_out[l, 0].astype(BF16), h, all_rows, m[2], weight=MACARON_W, bn=BN, **mm)

        p = _norm_matmul(h, all_rows, gain[1], m[3], m[4], _relayout_w_in(w_in[l]), bn=BN, out_dtype=F32, **mm)
        a_lat, a_ctx = _gla(p, _pad_gate_up(gla_gate_up[l]), gla_gate_bias[l].reshape(2, 1, -1),
                            gla_norm[l].reshape(1, -1), batch=batch)
        o_a = a_lat if last else jnp.concatenate([a_lat, a_ctx], axis=0)
        o_b = _swa(p, swa_sink[l], cos_tab, sin_tab, swa_q_norm[l].reshape(1, -1), swa_k_norm[l].reshape(1, -1),
                   batch=batch, with_ctx_out=not last)
        o_c = _na(p, _na_bias_table(na_rpb[l]), na_q_norm[l].reshape(1, -1), na_k_norm[l].reshape(1, -1),
                  batch=batch, with_ctx_out=not last)
        y = _merge(o_a, o_b, o_c, w_branch[l].astype(BF16), p, rows_out, bm=BM, bn=BN_MERGE)
        h = _resid_matmul(y, w_out[l].astype(BF16), h, rows_out, m[5], weight=1.0, bn=BN, **mm)

        g2 = _norm_swiglu(h, rows_out, gain[2], m[6], m[7], ffn_w_in[l, 1].astype(BF16), bn=BN_FF, **mm)
        h = _resid_matmul(g2, ffn_w_out[l, 1].astype(BF16), h, rows_out, m[8], weight=MACARON_W, bn=BN, **mm)

    return h.reshape(batch, seq, d)
```

```python
import functools

import jax
import jax.numpy as jnp
import numpy as np
from jax import lax
from jax.experimental import pallas as pl
from jax.experimental.pallas import tpu as pltpu

F32 = jnp.float32
BF16 = jnp.bfloat16
HIGHEST = lax.Precision.HIGHEST

D_MODEL = 4096
SEQ = 2048
CTX_LEN = 256
GRID_W = 64
HEAD_DIM = 128
GLA_HEADS = 4
GLA_DK = 128
GLA_DV = 256
GLA_GATE_RANK = 16
GLA_GATE_NORM = 16.0
GLA_CHUNK = 64
SWA_HEADS = 8
SWA_KV_HEADS = 2
SWA_GROUP = SWA_HEADS // SWA_KV_HEADS
SWA_WINDOW = 128
NA_HEADS = 8
NA_KH = 8
NA_KW = 16
N_BRANCH = 3
BRANCH_W = 1024
D_FF = 4096
MACARON_W = 0.5
N_MOD = 9
ROPE_BASE = 10000.0
EPS = 1e-6
NEG_INF = -1e30
MOD_ROWS = 8

C_GQ, C_GK, C_GV, C_GO = 0, 512, 1024, 2048
C_SQ, C_NQ, C_NK, C_NV = 3072, 4096, 5120, 6144
C_SK, C_SV, C_GD = 7168, 7424, 7680
C_GATE = 8192
P_COLS = C_GATE + N_BRANCH * D_MODEL

V7X_VMEM_BYTES = 64 * 1024 * 1024
V7X_VMEM_RESERVE = 6 * 1024 * 1024
QBLK = 128
NA_SLAB_ROWS = 10


def _vmem_limit(pipelined_bytes, scratch_bytes=0, temp_bytes=0):
    need = 2 * pipelined_bytes + scratch_bytes + temp_bytes + (4 << 20)
    return int(min(max(need, 16 << 20), V7X_VMEM_BYTES - V7X_VMEM_RESERVE))


def _nbytes(shape, dtype):
    return int(np.prod(shape)) * jnp.dtype(dtype).itemsize


def _adaln_kernel(v_ref, down_ref, up_ref, bias_ref, o_ref):
    v = v_ref[...]
    t = jnp.dot(jax.nn.silu(v), down_ref[...], preferred_element_type=F32, precision=HIGHEST)
    o_ref[...] = jnp.dot(t, up_ref[...], preferred_element_type=F32, precision=HIGHEST) + bias_ref[...]


def _adaln(v8, ada_down, ada_up, ada_bias):
    depth, d, rank = ada_down.shape
    blocks = (_nbytes((MOD_ROWS, d), F32) + _nbytes((d, rank), F32) + _nbytes((rank, d), F32)
              + _nbytes((1, d), F32) + _nbytes((MOD_ROWS, d), F32))
    return pl.pallas_call(
        _adaln_kernel,
        grid=(depth, N_MOD),
        in_specs=[
            pl.BlockSpec((MOD_ROWS, d), lambda l, j: (0, 0)),
            pl.BlockSpec((None, d, rank), lambda l, j: (l, 0, 0)),
            pl.BlockSpec((None, rank, d), lambda l, j: (l, 0, j)),
            pl.BlockSpec((None, 1, d), lambda l, j: (l, 0, j)),
        ],
        out_specs=pl.BlockSpec((None, None, MOD_ROWS, d), lambda l, j: (l, j, 0, 0)),
        out_shape=jax.ShapeDtypeStruct((depth, N_MOD, MOD_ROWS, d), F32),
        compiler_params=pltpu.CompilerParams(
            dimension_semantics=("arbitrary", "arbitrary"), vmem_limit_bytes=_vmem_limit(blocks)),
        name="adaln",
    )(v8, ada_down, ada_up, ada_bias.reshape(depth, 1, -1))


def _mod_row_index(i, bm, lat_rows, seq):
    del lat_rows
    return jnp.minimum((i * bm) // seq, MOD_ROWS - 1)


NORM_ROWS = 32


def _norm_prologue(h_ref, g_ref, sh_ref, sc_ref, xn_ref, bm):
    gain = g_ref[...]
    scale1 = 1.0 + sc_ref[...]
    shift = sh_ref[...]

    def body(c, carry):
        r0 = pl.multiple_of(c * NORM_ROWS, NORM_ROWS)
        x = h_ref[pl.ds(r0, NORM_ROWS), :]
        y = x * lax.rsqrt(jnp.mean(x * x, axis=-1, keepdims=True) + EPS)
        xn_ref[pl.ds(r0, NORM_ROWS), :] = ((y * gain) * scale1 + shift).astype(BF16)
        return carry

    lax.fori_loop(0, bm // NORM_ROWS, body, 0)


def _norm_mm_kernel(h_ref, g_ref, sh_ref, sc_ref, w_ref, o_ref, xn_ref, *, bm):
    @pl.when(pl.program_id(1) == 0)
    def _():
        _norm_prologue(h_ref, g_ref, sh_ref, sc_ref, xn_ref, bm)

    o_ref[...] = jnp.dot(xn_ref[...], w_ref[...], preferred_element_type=F32).astype(o_ref.dtype)


def _norm_swiglu_kernel(h_ref, g_ref, sh_ref, sc_ref, wa_ref, wb_ref, o_ref, xn_ref, *, bm):
    @pl.when(pl.program_id(1) == 0)
    def _():
        _norm_prologue(h_ref, g_ref, sh_ref, sc_ref, xn_ref, bm)

    xn = xn_ref[...]
    a = jnp.dot(xn, wa_ref[...], preferred_element_type=F32)
    b = jnp.dot(xn, wb_ref[...], preferred_element_type=F32)
    o_ref[...] = (jax.nn.silu(a) * b).astype(o_ref.dtype)


def _mod_spec(bm, seq, d):
    return pl.BlockSpec((None, 1, d), lambda i, j: (_mod_row_index(i, bm, None, seq), 0, 0))


def _norm_matmul(h, rows, gain, shift, scale, w, *, seq, bm, bn, out_dtype):
    d = h.shape[1]
    n = w.shape[1]
    blocks = (_nbytes((bm, d), F32) + _nbytes((d, bn), BF16) + _nbytes((bm, bn), out_dtype)
              + 3 * _nbytes((1, d), F32))
    return pl.pallas_call(
        functools.partial(_norm_mm_kernel, bm=bm),
        grid=(rows // bm, n // bn),
        in_specs=[
            pl.BlockSpec((bm, d), lambda i, j: (i, 0)),
            pl.BlockSpec((1, d), lambda i, j: (0, 0)),
            _mod_spec(bm, seq, d),
            _mod_spec(bm, seq, d),
            pl.BlockSpec((d, bn), lambda i, j: (0, j)),
        ],
        out_specs=pl.BlockSpec((bm, bn), lambda i, j: (i, j)),
        out_shape=jax.ShapeDtypeStruct((rows, n), out_dtype),
        scratch_shapes=[pltpu.VMEM((bm, d), BF16)],
        compiler_params=pltpu.CompilerParams(
            dimension_semantics=("arbitrary", "arbitrary"),
            vmem_limit_bytes=_vmem_limit(blocks, _nbytes((bm, d), BF16), _nbytes((bm, bn), F32))),
        name="norm_matmul",
    )(h, gain, shift, scale, w)


def _norm_swiglu(h, rows, gain, shift, scale, w_in, *, seq, bm, bn):
    d = h.shape[1]
    f = w_in.shape[1] // 2
    nb = f // bn
    blocks = (_nbytes((bm, d), F32) + 2 * _nbytes((d, bn), BF16) + _nbytes((bm, bn), BF16)
              + 3 * _nbytes((1, d), F32))
    return pl.pallas_call(
        functools.partial(_norm_swiglu_kernel, bm=bm),
        grid=(rows // bm, nb),
        in_specs=[
            pl.BlockSpec((bm, d), lambda i, j: (i, 0)),
            pl.BlockSpec((1, d), lambda i, j: (0, 0)),
            _mod_spec(bm, seq, d),
            _mod_spec(bm, seq, d),
            pl.BlockSpec((d, bn), lambda i, j: (0, j)),
            pl.BlockSpec((d, bn), lambda i, j: (0, j + nb)),
        ],
        out_specs=pl.BlockSpec((bm, bn), lambda i, j: (i, j)),
        out_shape=jax.ShapeDtypeStruct((rows, f), BF16),
        scratch_shapes=[pltpu.VMEM((bm, d), BF16)],
        compiler_params=pltpu.CompilerParams(
            dimension_semantics=("arbitrary", "arbitrary"),
            vmem_limit_bytes=_vmem_limit(blocks, _nbytes((bm, d), BF16), 3 * _nbytes((bm, bn), F32))),
        name="norm_swiglu",
    )(h, gain, shift, scale, w_in, w_in)


def _resid_mm_kernel(x_ref, w_ref, h_ref, gate_ref, o_ref, *, weight):
    y = jnp.dot(x_ref[...], w_ref[...], preferred_element_type=F32)
    if weight != 1.0:
        o_ref[...] = h_ref[...] + weight * gate_ref[...] * y
    else:
        o_ref[...] = h_ref[...] + gate_ref[...] * y


def _resid_matmul(x, w, h, rows, gate, *, weight, seq, bm, bn):
    k = x.shape[1]
    n = w.shape[1]
    blocks = (_nbytes((bm, k), BF16) + _nbytes((k, bn), BF16) + 2 * _nbytes((bm, bn), F32)
              + _nbytes((1, bn), F32))
    return pl.pallas_call(
        functools.partial(_resid_mm_kernel, weight=weight),
        grid=(rows // bm, n // bn),
        in_specs=[
            pl.BlockSpec((bm, k), lambda i, j: (i, 0)),
            pl.BlockSpec((k, bn), lambda i, j: (0, j)),
            pl.BlockSpec((bm, bn), lambda i, j: (i, j)),
            pl.BlockSpec((None, 1, bn), lambda i, j: (_mod_row_index(i, bm, None, seq), 0, j)),
        ],
        out_specs=pl.BlockSpec((bm, bn), lambda i, j: (i, j)),
        out_shape=jax.ShapeDtypeStruct((rows, n), F32),
        compiler_params=pltpu.CompilerParams(
            dimension_semantics=("arbitrary", "arbitrary"),
            vmem_limit_bytes=_vmem_limit(blocks, 0, _nbytes((bm, bn), F32))),
        name="resid_matmul",
    )(x, w, h, gate)


def _merge_kernel(oa_ref, ob_ref, oc_ref, w_ref, ga_ref, gb_ref, gc_ref, y_ref):
    y = jax.nn.sigmoid(ga_ref[...]) * jnp.dot(oa_ref[...], w_ref[0], preferred_element_type=F32)
    y = y + jax.nn.sigmoid(gb_ref[...]) * jnp.dot(ob_ref[...], w_ref[1], preferred_element_type=F32)
    y = y + jax.nn.sigmoid(gc_ref[...]) * jnp.dot(oc_ref[...], w_ref[2], preferred_element_type=F32)
    y_ref[...] = y.astype(y_ref.dtype)


def _merge(oa, ob, oc, w_branch, p, rows, *, bm, bn):
    kb = oa.shape[1]
    d = w_branch.shape[2]
    gate_blk = C_GATE // bn
    per_branch = d // bn
    blocks = (3 * _nbytes((bm, kb), BF16) + _nbytes((N_BRANCH, kb, bn), BF16) + 3 * _nbytes((bm, bn), F32)
              + _nbytes((bm, bn), BF16))
    o_spec = pl.BlockSpec((bm, kb), lambda i, j: (i, 0))

    def gate_spec(br):
        return pl.BlockSpec((bm, bn), lambda i, j: (i, gate_blk + br * per_branch + j))

    return pl.pallas_call(
        _merge_kernel,
        grid=(rows // bm, d // bn),
        in_specs=[o_spec, o_spec, o_spec,
                  pl.BlockSpec((N_BRANCH, kb, bn), lambda i, j: (0, 0, j)),
                  gate_spec(0), gate_spec(1), gate_spec(2)],
        out_specs=pl.BlockSpec((bm, bn), lambda i, j: (i, j)),
        out_shape=jax.ShapeDtypeStruct((rows, d), BF16),
        compiler_params=pltpu.CompilerParams(
            dimension_semantics=("arbitrary", "arbitrary"),
            vmem_limit_bytes=_vmem_limit(blocks, 0, 2 * _nbytes((bm, bn), F32))),
        name="merge",
    )(oa, ob, oc, w_branch, p, p, p)


GLA_UNROLL = 4
_NT = (((1,), (1,)), ((), ()))
_TN = (((0,), (0,)), ((), ()))


def _gla_kernel(ql_ref, kl_ref, vl_ref, ogl_ref, gdl_ref, qc_ref, kc_ref, vc_ref, ogc_ref, gdc_ref,
                up_ref, gb_ref, gn_ref, al_ref, ac_ref, lg_ref, of_ref, ob_ref, st_ref, *, t_lat, t_ctx):
    c = GLA_CHUNK
    row = lax.broadcasted_iota(jnp.int32, (c, c), 0)
    col = lax.broadcasted_iota(jnp.int32, (c, c), 1)
    keep = (col <= row, col >= row)
    ones = (keep[0].astype(F32), keep[1].astype(F32))
    q_scale = GLA_DK ** -0.5
    gain = gn_ref[...]

    st_ref[...] = jnp.zeros_like(st_ref)

    def segment(q_ref, k_ref, v_ref, og_ref, gd_ref, out_ref, t):
        nc = t // c
        gate_rows = min(t, 256)

        def gate_body(r, carry):
            r0 = pl.multiple_of(r * gate_rows, gate_rows)
            gd = gd_ref[pl.ds(r0, gate_rows), :]
            for d in range(2):
                z = jnp.dot(gd, up_ref[d], preferred_element_type=F32) + gb_ref[d]
                lg_ref[d, pl.ds(r0, gate_rows), :] = jax.nn.log_sigmoid(z) * (1.0 / GLA_GATE_NORM)
            return carry

        lax.fori_loop(0, t // gate_rows, gate_body, 0)

        def chunk(r0, d):
            q = q_ref[pl.ds(r0, c), :] * q_scale
            k = k_ref[pl.ds(r0, c), :]
            v = v_ref[pl.ds(r0, c), :].astype(BF16)
            lg = lg_ref[d, pl.ds(r0, c), :]
            cum = jnp.dot(ones[d], lg, preferred_element_type=F32, precision=HIGHEST)
            cum_end = cum[c - 1:c, :] if d == 0 else cum[0:1, :]
            qe = (q * jnp.exp(cum)).astype(BF16)
            kinv = (k * jnp.exp(-cum)).astype(BF16)
            kdec = (k * jnp.exp(cum_end - cum)).astype(BF16)
            a = lax.dot_general(qe, kinv, _NT, preferred_element_type=F32)
            a = jnp.where(keep[d], a, 0.0).astype(BF16)
            st = st_ref[d]
            o = jnp.dot(a, v, preferred_element_type=F32)
            o = o + lax.dot_general(qe, st.astype(BF16), _NT, preferred_element_type=F32)
            st_ref[d] = st * jnp.exp(cum_end) + lax.dot_general(v, kdec, _TN, preferred_element_type=F32)
            return o

        def chunk_body(n, carry):
            rf = pl.multiple_of(n * c, c)
            rb = pl.multiple_of((nc - 1 - n) * c, c)
            of_ref[pl.ds(rf, c), :] = chunk(rf, 0)
            ob_ref[pl.ds(rb, c), :] = chunk(rb, 1)
            return carry

        lax.fori_loop(0, nc, chunk_body, 0, unroll=GLA_UNROLL)

        def out_body(n, carry):
            r0 = pl.multiple_of(n * c, c)
            o = of_ref[pl.ds(r0, c), :] + ob_ref[pl.ds(r0, c), :]
            y = o * lax.rsqrt(jnp.mean(o * o, axis=-1, keepdims=True) + EPS)
            y = (y * gain) * jax.nn.silu(og_ref[pl.ds(r0, c), :])
            out_ref[pl.ds(r0, c), :] = y.astype(out_ref.dtype)
            return carry

        lax.fori_loop(0, nc, out_body, 0, unroll=GLA_UNROLL)

    segment(qc_ref, kc_ref, vc_ref, ogc_ref, gdc_ref, ac_ref, t_ctx)
    segment(ql_ref, kl_ref, vl_ref, ogl_ref, gdl_ref, al_ref, t_lat)


def _gla(p, gate_up_pad, gate_bias, gla_norm, *, batch):
    t_lat, t_ctx = SEQ, CTX_LEN
    ctx0 = batch * t_lat // t_ctx
    dk, dv = GLA_DK, GLA_DV

    def lat(width, col0):
        return pl.BlockSpec((t_lat, width), lambda b, h: (b, col0 // width + h))

    def ctx(width, col0):
        return pl.BlockSpec((t_ctx, width), lambda b, h: (ctx0 + b, col0 // width + h))

    lat_gd = pl.BlockSpec((t_lat, 128), lambda b, h: (b, C_GD // 128))
    ctx_gd = pl.BlockSpec((t_ctx, 128), lambda b, h: (ctx0 + b, C_GD // 128))
    blocks = ((t_lat + t_ctx) * (3 * dk + 2 * dv) * 4 + _nbytes((2, 128, dk), F32)
              + (t_lat + t_ctx) * dv * 2)
    scratch = _nbytes((2, t_lat, dk), F32) + 2 * _nbytes((t_lat, dv), F32) + _nbytes((2, dv, dk), F32)
    return pl.pallas_call(
        functools.partial(_gla_kernel, t_lat=t_lat, t_ctx=t_ctx),
        grid=(batch, GLA_HEADS),
        in_specs=[lat(dk, C_GQ), lat(dk, C_GK), lat(dv, C_GV), lat(dv, C_GO), lat_gd,
                  ctx(dk, C_GQ), ctx(dk, C_GK), ctx(dv, C_GV), ctx(dv, C_GO), ctx_gd,
                  pl.BlockSpec((2, 128, dk), lambda b, h: (0, 0, h)),
                  pl.BlockSpec((2, 1, dk), lambda b, h: (0, 0, h)),
                  pl.BlockSpec((1, dv), lambda b, h: (0, 0))],
        out_specs=[pl.BlockSpec((t_lat, dv), lambda b, h: (b, h)),
                   pl.BlockSpec((t_ctx, dv), lambda b, h: (b, h))],
        out_shape=[jax.ShapeDtypeStruct((batch * t_lat, GLA_HEADS * dv), BF16),
                   jax.ShapeDtypeStruct((batch * t_ctx, GLA_HEADS * dv), BF16)],
        scratch_shapes=[pltpu.VMEM((2, t_lat, dk), F32), pltpu.VMEM((t_lat, dv), F32),
                        pltpu.VMEM((t_lat, dv), F32), pltpu.VMEM((2, dv, dk), F32)],
        compiler_params=pltpu.CompilerParams(
            dimension_semantics=("arbitrary", "arbitrary"),
            vmem_limit_bytes=_vmem_limit(blocks, scratch, 4 << 20)),
        name="gla",
    )(p, p, p, p, p, p, p, p, p, p, gate_up_pad, gate_bias, gla_norm)


def _rms(x, gain):
    return (x * lax.rsqrt(jnp.mean(x * x, axis=-1, keepdims=True) + EPS)) * gain


def _rope(x, cos, sin_signed):
    lane = lax.broadcasted_iota(jnp.int32, x.shape, 1)
    partner = jnp.where((lane % 64) < 32, pltpu.roll(x, 96, 1), pltpu.roll(x, 32, 1))
    return x * cos + partner * sin_signed


def _q_row_block(b, qb, *, batch, n_lat, n_ctx):
    return jnp.where(qb < n_lat, b * n_lat + qb, batch * n_lat + b * n_ctx + (qb - n_lat))


KV_PREP_ROWS = 256


def _swa_kernel(sink_ref, q_ref, kl_ref, vl_ref, kc_ref, vc_ref, cq_ref, sq_ref, ck_ref, sk_ref,
                qg_ref, kg_ref, o_ref, kn_ref, vb_ref, kcn_ref, vcb_ref, *, t_lat, n_lat):
    qb = pl.program_id(1)
    g = SWA_GROUP
    dh = HEAD_DIM
    scale = dh ** -0.5

    @pl.when(qb == 0)
    def _prep():
        k_gain = kg_ref[...]

        def body(r, carry):
            r0 = pl.multiple_of(r * KV_PREP_ROWS, KV_PREP_ROWS)
            rows = pl.ds(r0, KV_PREP_ROWS)
            cos, sin = ck_ref[rows, :], sk_ref[rows, :]
            for kv in range(SWA_KV_HEADS):
                cols = slice(kv * dh, (kv + 1) * dh)
                kn_ref[rows, cols] = _rope(_rms(kl_ref[rows, cols], k_gain), cos, sin).astype(BF16)
            vb_ref[rows, :] = vl_ref[rows, :].astype(BF16)
            return carry

        lax.fori_loop(0, t_lat // KV_PREP_ROWS, body, 0)
        for kv in range(SWA_KV_HEADS):
            cols = slice(kv * dh, (kv + 1) * dh)
            kcn_ref[:, cols] = _rms(kc_ref[:, cols], k_gain).astype(BF16)
        vcb_ref[...] = vc_ref[...].astype(BF16)

    def attend(latent):
        cos, sin = cq_ref[...], sq_ref[...]
        q_gain = qg_ref[...]
        head_of_row = lax.broadcasted_iota(jnp.int32, (g * QBLK, 1), 0) // QBLK
        for kv in range(SWA_KV_HEADS):
            cols = slice(kv * dh, (kv + 1) * dh)
            heads = [_rope(_rms(q_ref[:, (kv * g + i) * dh:(kv * g + i + 1) * dh], q_gain), cos, sin).astype(BF16)
                     for i in range(g)]
            qs = jnp.concatenate(heads, axis=0)
            sink = jnp.full((g * QBLK, 1), sink_ref[kv * g], F32)
            for i in range(1, g):
                sink = jnp.where(head_of_row == i, sink_ref[kv * g + i], sink)
            s_ctx = lax.dot_general(qs, kcn_ref[:, cols], _NT, preferred_element_type=F32) * scale
            m = jnp.maximum(jnp.max(s_ctx, axis=-1, keepdims=True), sink)
            if latent:
                span = 3 * QBLK
                start = pl.multiple_of(jnp.clip((qb - 1) * QBLK, 0, t_lat - span), QBLK)
                s_loc = lax.dot_general(qs, kn_ref[pl.ds(start, span), cols], _NT,
                                        preferred_element_type=F32) * scale
                qpos = qb * QBLK + lax.broadcasted_iota(jnp.int32, (g * QBLK, span), 0) % QBLK
                kpos = start + lax.broadcasted_iota(jnp.int32, (g * QBLK, span), 1)
                s_loc = jnp.where(jnp.abs(kpos - qpos) <= SWA_WINDOW, s_loc, NEG_INF)
                m = jnp.maximum(m, jnp.max(s_loc, axis=-1, keepdims=True))
                p_loc = jnp.exp(s_loc - m)
                den = jnp.sum(p_loc, axis=-1, keepdims=True)
                o = jnp.dot(p_loc.astype(BF16), vb_ref[pl.ds(start, span), cols], preferred_element_type=F32)
            p_ctx = jnp.exp(s_ctx - m)
            o_ctx = jnp.dot(p_ctx.astype(BF16), vcb_ref[:, cols], preferred_element_type=F32)
            den_ctx = jnp.sum(p_ctx, axis=-1, keepdims=True) + jnp.exp(sink - m)
            o, den = (o + o_ctx, den + den_ctx) if latent else (o_ctx, den_ctx)
            o = o / den
            for i in range(g):
                o_ref[:, (kv * g + i) * dh:(kv * g + i + 1) * dh] = o[i * QBLK:(i + 1) * QBLK, :].astype(o_ref.dtype)

    pl.when(qb < n_lat)(functools.partial(attend, True))
    pl.when(qb >= n_lat)(functools.partial(attend, False))


def _swa(p, sink, cos_tab, sin_tab, q_gain, k_gain, *, batch, with_ctx_out):
    t_lat, t_ctx, dh = SEQ, CTX_LEN, HEAD_DIM
    qw, kw = SWA_HEADS * dh, SWA_KV_HEADS * dh
    n_lat, n_ctx = t_lat // QBLK, t_ctx // QBLK
    n_q = n_lat + (n_ctx if with_ctx_out else 0)
    rows_out = batch * (t_lat + (t_ctx if with_ctx_out else 0))
    ctx0 = batch * t_lat // t_ctx
    qmap = functools.partial(_q_row_block, batch=batch, n_lat=n_lat, n_ctx=n_ctx)
    blocks = (_nbytes((QBLK, qw), F32) + 2 * _nbytes((t_lat + t_ctx, kw), F32) + 2 * _nbytes((QBLK, dh), F32)
              + 2 * _nbytes((t_lat, dh), F32) + _nbytes((QBLK, qw), BF16))
    scratch = 2 * _nbytes((t_lat + t_ctx, kw), BF16)
    return pl.pallas_call(
        functools.partial(_swa_kernel, t_lat=t_lat, n_lat=n_lat),
        grid=(batch, n_q),
        in_specs=[
            pl.BlockSpec(memory_space=pltpu.SMEM),
            pl.BlockSpec((QBLK, qw), lambda b, qb: (qmap(b, qb), C_SQ // qw)),
            pl.BlockSpec((t_lat, kw), lambda b, qb: (b, C_SK // kw)),
            pl.BlockSpec((t_lat, kw), lambda b, qb: (b, C_SV // kw)),
            pl.BlockSpec((t_ctx, kw), lambda b, qb: (ctx0 + b, C_SK // kw)),
            pl.BlockSpec((t_ctx, kw), lambda b, qb: (ctx0 + b, C_SV // kw)),
            pl.BlockSpec((QBLK, dh), lambda b, qb: (qb, 0)),
            pl.BlockSpec((QBLK, dh), lambda b, qb: (qb, 0)),
            pl.BlockSpec((t_lat, dh), lambda b, qb: (0, 0)),
            pl.BlockSpec((t_lat, dh), lambda b, qb: (0, 0)),
            pl.BlockSpec((1, dh), lambda b, qb: (0, 0)),
            pl.BlockSpec((1, dh), lambda b, qb: (0, 0)),
        ],
        out_specs=pl.BlockSpec((QBLK, qw), lambda b, qb: (qmap(b, qb), 0)),
        out_shape=jax.ShapeDtypeStruct((rows_out, qw), BF16),
        scratch_shapes=[pltpu.VMEM((t_lat, kw), BF16), pltpu.VMEM((t_lat, kw), BF16),
                        pltpu.VMEM((t_ctx, kw), BF16), pltpu.VMEM((t_ctx, kw), BF16)],
        compiler_params=pltpu.CompilerParams(
            dimension_semantics=("arbitrary", "arbitrary"),
            vmem_limit_bytes=_vmem_limit(blocks, scratch, 12 << 20)),
        name="swa",
    )(sink, p, p, p, p, p, cos_tab, sin_tab, cos_tab, sin_tab, q_gain, k_gain)


NA_HEADS_PER_STEP = 4


def _na_slab_start(qb):
    r = qb * (QBLK // GRID_W)
    rows = SEQ // GRID_W
    return jnp.minimum(jnp.clip(r - NA_KH // 2, 0, rows - NA_KH), rows - NA_SLAB_ROWS)


def _na_bias_variant(qb, n_lat):
    return jnp.where(qb < 2, qb, jnp.where(qb < n_lat - 2, 2, jnp.minimum(qb, n_lat - 1) - (n_lat - 5)))


def _na_kernel(q_ref, kl_ref, vl_ref, kc_ref, vc_ref, bias_ref, qg_ref, kg_ref, o_ref,
               kn_ref, vb_ref, kcn_ref, vcb_ref, *, t_lat, n_lat):
    qb = pl.program_id(2)
    dh = HEAD_DIM
    scale = dh ** -0.5

    @pl.when(qb == 0)
    def _prep():
        k_gain = kg_ref[...]

        def body(r, carry):
            r0 = pl.multiple_of(r * KV_PREP_ROWS, KV_PREP_ROWS)
            rows = pl.ds(r0, KV_PREP_ROWS)
            for h in range(NA_HEADS_PER_STEP):
                cols = slice(h * dh, (h + 1) * dh)
                kn_ref[rows, cols] = _rms(kl_ref[rows, cols], k_gain).astype(BF16)
            vb_ref[rows, :] = vl_ref[rows, :].astype(BF16)
            return carry

        lax.fori_loop(0, t_lat // KV_PREP_ROWS, body, 0)
        for h in range(NA_HEADS_PER_STEP):
            cols = slice(h * dh, (h + 1) * dh)
            kcn_ref[:, cols] = _rms(kc_ref[:, cols], k_gain).astype(BF16)
        vcb_ref[...] = vc_ref[...].astype(BF16)

    def attend(latent):
        q_gain = qg_ref[...]
        for h in range(NA_HEADS_PER_STEP):
            cols = slice(h * dh, (h + 1) * dh)
            qn = _rms(q_ref[:, cols], q_gain).astype(BF16)
            s_ctx = lax.dot_general(qn, kcn_ref[:, cols], _NT, preferred_element_type=F32) * scale
            m = jnp.max(s_ctx, axis=-1, keepdims=True)
            if latent:
                span = NA_SLAB_ROWS * GRID_W
                start = pl.multiple_of(_na_slab_start(qb) * GRID_W, GRID_W)
                s_loc = lax.dot_general(qn, kn_ref[pl.ds(start, span), cols], _NT,
                                        preferred_element_type=F32) * scale
                s_loc = s_loc + bias_ref[h]
                m = jnp.maximum(m, jnp.max(s_loc, axis=-1, keepdims=True))
                p_loc = jnp.exp(s_loc - m)
                den = jnp.sum(p_loc, axis=-1, keepdims=True)
                o = jnp.dot(p_loc.astype(BF16), vb_ref[pl.ds(start, span), cols], preferred_element_type=F32)
            p_ctx = jnp.exp(s_ctx - m)
            o_ctx = jnp.dot(p_ctx.astype(BF16), vcb_ref[:, cols], preferred_element_type=F32)
            den_ctx = jnp.sum(p_ctx, axis=-1, keepdims=True)
            o, den = (o + o_ctx, den + den_ctx) if latent else (o_ctx, den_ctx)
            o_ref[:, cols] = (o / den).astype(o_ref.dtype)

    pl.when(qb < n_lat)(functools.partial(attend, True))
    pl.when(qb >= n_lat)(functools.partial(attend, False))


def _na(p, bias_tab, q_gain, k_gain, *, batch, with_ctx_out):
    t_lat, t_ctx, dh = SEQ, CTX_LEN, HEAD_DIM
    hb = NA_HEADS_PER_STEP
    w = hb * dh
    n_lat, n_ctx = t_lat // QBLK, t_ctx // QBLK
    n_q = n_lat + (n_ctx if with_ctx_out else 0)
    rows_out = batch * (t_lat + (t_ctx if with_ctx_out else 0))
    ctx0 = batch * t_lat // t_ctx
    span = NA_SLAB_ROWS * GRID_W
    qmap = functools.partial(_q_row_block, batch=batch, n_lat=n_lat, n_ctx=n_ctx)
    blocks = (_nbytes((QBLK, w), F32) + 2 * _nbytes((t_lat + t_ctx, w), F32) + _nbytes((hb, QBLK, span), F32)
              + _nbytes((QBLK, w), BF16))
    scratch = 2 * _nbytes((t_lat + t_ctx, w), BF16)
    return pl.pallas_call(
        functools.partial(_na_kernel, t_lat=t_lat, n_lat=n_lat),
        grid=(batch, NA_HEADS // hb, n_q),
        in_specs=[
            pl.BlockSpec((QBLK, w), lambda b, hg, qb: (qmap(b, qb), C_NQ // w + hg)),
            pl.BlockSpec((t_lat, w), lambda b, hg, qb: (b, C_NK // w + hg)),
            pl.BlockSpec((t_lat, w), lambda b, hg, qb: (b, C_NV // w + hg)),
            pl.BlockSpec((t_ctx, w), lambda b, hg, qb: (ctx0 + b, C_NK // w + hg)),
            pl.BlockSpec((t_ctx, w), lambda b, hg, qb: (ctx0 + b, C_NV // w + hg)),
            pl.BlockSpec((hb, None, QBLK, span), lambda b, hg, qb: (hg, _na_bias_variant(qb, n_lat), 0, 0)),
            pl.BlockSpec((1, dh), lambda b, hg, qb: (0, 0)),
            pl.BlockSpec((1, dh), lambda b, hg, qb: (0, 0)),
        ],
        out_specs=pl.BlockSpec((QBLK, w), lambda b, hg, qb: (qmap(b, qb), hg)),
        out_shape=jax.ShapeDtypeStruct((rows_out, NA_HEADS * dh), BF16),
        scratch_shapes=[pltpu.VMEM((t_lat, w), BF16), pltpu.VMEM((t_lat, w), BF16),
                        pltpu.VMEM((t_ctx, w), BF16), pltpu.VMEM((t_ctx, w), BF16)],
        compiler_params=pltpu.CompilerParams(
            dimension_semantics=("arbitrary", "arbitrary", "arbitrary"),
            vmem_limit_bytes=_vmem_limit(blocks, scratch, 12 << 20)),
        name="natten",
    )(p, p, p, p, p, bias_tab, q_gain, k_gain)


def _rope_tables():
    quarter = HEAD_DIM // 4
    pos = jnp.arange(SEQ)
    rows = (pos // GRID_W).astype(F32)
    cols = (pos % GRID_W).astype(F32)
    inv = ROPE_BASE ** (-jnp.arange(quarter, dtype=F32) / quarter)
    ang_r = rows[:, None] * inv[None, :]
    ang_c = cols[:, None] * inv[None, :]
    cos = jnp.concatenate([jnp.cos(ang_r), jnp.cos(ang_r), jnp.cos(ang_c), jnp.cos(ang_c)], axis=-1)
    sin = jnp.concatenate([-jnp.sin(ang_r), jnp.sin(ang_r), -jnp.sin(ang_c), jnp.sin(ang_c)], axis=-1)
    cos = jnp.concatenate([cos, jnp.ones((CTX_LEN, HEAD_DIM), F32)], axis=0)
    sin = jnp.concatenate([sin, jnp.zeros((CTX_LEN, HEAD_DIM), F32)], axis=0)
    return cos, sin


def _na_bias_tables(rpb):
    n_layers, n_heads = rpb.shape[:2]
    rows = SEQ // GRID_W
    n_lat = SEQ // QBLK
    rq_per = QBLK // GRID_W
    reps = np.array([0, 1, 2, n_lat - 2, n_lat - 1])
    r = reps * rq_per
    start = np.minimum(np.clip(r - NA_KH // 2, 0, rows - NA_KH), rows - NA_SLAB_ROWS)
    rq = r[:, None] + np.arange(rq_per)[None, :]
    kr = start[:, None] + np.arange(NA_SLAB_ROWS)[None, :]
    r0 = np.clip(rq - NA_KH // 2, 0, rows - NA_KH)
    valid_r = (kr[:, None, :] >= r0[:, :, None]) & (kr[:, None, :] < r0[:, :, None] + NA_KH)
    dr = np.clip(kr[:, None, :] - rq[:, :, None] + NA_KH - 1, 0, 2 * NA_KH - 2)
    qc = np.arange(GRID_W)
    kc = np.arange(GRID_W)
    cs = np.clip(qc - NA_KW // 2, 0, GRID_W - NA_KW)
    valid_c = (kc[None, :] >= cs[:, None]) & (kc[None, :] < cs[:, None] + NA_KW)
    dc = np.clip(kc[None, :] - qc[:, None] + NA_KW - 1, 0, 2 * NA_KW - 2)
    pick_r = np.eye(2 * NA_KH - 1, dtype=np.float32)[dr.reshape(-1)]
    pick_c = np.eye(2 * NA_KW - 1, dtype=np.float32)[dc.reshape(-1)]
    t = jnp.einsum("nr,lhrc->lhnc", pick_r, rpb.astype(F32), precision=HIGHEST)
    t = jnp.einsum("lhnc,xc->lhnx", t, pick_c, precision=HIGHEST)
    t = t.reshape(n_layers, n_heads, len(reps), rq_per, NA_SLAB_ROWS, GRID_W, GRID_W)
    t = t.transpose(0, 1, 2, 3, 5, 4, 6)
    valid = valid_r[:, :, None, :, None] & valid_c[None, None, :, None, :]
    t = jnp.where(valid[None, None], t, NEG_INF)
    return t.reshape(n_layers, n_heads, len(reps), QBLK, NA_SLAB_ROWS * GRID_W)


def _relayout_w_in(w):
    d = w.shape[0]
    o = np.cumsum([0, 512, 512, 1024, 1024, 32, 1024, 256, 256, 1024, 1024, 1024, 12288])
    gq_go, gd = w[:, o[0]:o[4]], w[:, o[4]:o[5]]
    sq, sk, sv = w[:, o[5]:o[6]], w[:, o[6]:o[7]], w[:, o[7]:o[8]]
    nq, nk, nv = w[:, o[8]:o[9]], w[:, o[9]:o[10]], w[:, o[10]:o[11]]
    gates = w[:, o[11]:o[12]]
    pad = jnp.zeros((d, C_GATE - C_GD - 2 * GLA_GATE_RANK), w.dtype)
    return jnp.concatenate([gq_go, sq, nq, nk, nv, sk, sv, gd, pad, gates], axis=1).astype(BF16)


def _pad_gate_up(gate_up):
    r = gate_up.shape[1]
    out = jnp.zeros((2, 128, gate_up.shape[2]), F32)
    out = out.at[0, 0:r].set(gate_up[0])
    out = out.at[1, r:2 * r].set(gate_up[1])
    return out


BM = 512
BN = 1024
BN_FF = 512
BN_MERGE = 512


def kernel(x, c, ctx, c_ctx, ada_down, ada_up, ada_bias, norm_gain, ffn_w_in, ffn_w_out, w_in,
           gla_gate_up, gla_gate_bias, gla_norm, swa_q_norm, swa_k_norm, swa_sink, na_q_norm,
           na_k_norm, na_rpb, w_branch, w_out):
    batch, seq, d = x.shape
    depth = ada_down.shape[0]
    assert (seq, d, ctx.shape[1]) == (SEQ, D_MODEL, CTX_LEN) and batch + 1 <= MOD_ROWS
    lat_rows = batch * seq
    all_rows = lat_rows + batch * ctx.shape[1]

    v8 = jnp.concatenate([c, c_ctx[None], jnp.zeros((MOD_ROWS - batch - 1, d), F32)], axis=0)
    mods = _adaln(v8, ada_down, ada_up, ada_bias)
    mods = mods.reshape(depth, N_MOD, MOD_ROWS, 1, d)
    cos_tab, sin_tab = _rope_tables()
    na_bias = _na_bias_tables(na_rpb)

    h = jnp.concatenate([x.reshape(lat_rows, d), ctx.reshape(-1, d)], axis=0)
    mm = dict(seq=seq, bm=BM)

    for l in range(depth):
        last = l == depth - 1
        rows_out = lat_rows if last else all_rows
        gain = norm_gain[l].reshape(3, 1, d)
        m = mods[l]

        g1 = _norm_swiglu(h, all_rows, gain[0], m[0], m[1], ffn_w_in[l, 0].astype(BF16), bn=BN_FF, **mm)
        h = _resid_matmul(g1, ffn_w_out[l, 0].astype(BF16), h, all_rows, m[2], weight=MACARON_W, bn=BN, **mm)

        p = _norm_matmul(h, all_rows, gain[1], m[3], m[4], _relayout_w_in(w_in[l]), bn=BN, out_dtype=F32, **mm)
        a_lat, a_ctx = _gla(p, _pad_gate_up(gla_gate_up[l]), gla_gate_bias[l].reshape(2, 1, -1),
                            gla_norm[l].reshape(1, -1), batch=batch)
        o_a = a_lat if last else jnp.concatenate([a_lat, a_ctx], axis=0)
        o_b = _swa(p, swa_sink[l], cos_tab, sin_tab, swa_q_norm[l].reshape(1, -1), swa_k_norm[l].reshape(1, -1),
                   batch=batch, with_ctx_out=not last)
        o_c = _na(p, na_bias[l], na_q_norm[l].reshape(1, -1), na_k_norm[l].reshape(1, -1),
                  batch=batch, with_ctx_out=not last)
        y = _merge(o_a, o_b, o_c, w_branch[l].astype(BF16), p, rows_out, bm=BM, bn=BN_MERGE)
        h = _resid_matmul(y, w_out[l].astype(BF16), h, rows_out, m[5], weight=1.0, bn=BN, **mm)

        g2 = _norm_swiglu(h, rows_out, gain[2], m[6], m[7], ffn_w_in[l, 1].astype(BF16), bn=BN_FF, **mm)
        h = _resid_matmul(g2, ffn_w_out[l, 1].astype(BF16), h, rows_out, m[8], weight=MACARON_W, bn=BN, **mm)

    return h.reshape(batch, seq, d)
```

```python
import functools

import jax
import jax.numpy as jnp
import numpy as np
from jax import lax
from jax.experimental import pallas as pl
from jax.experimental.pallas import tpu as pltpu

F32 = jnp.float32
BF16 = jnp.bfloat16
HIGHEST = lax.Precision.HIGHEST

D_MODEL = 4096
SEQ = 2048
CTX_LEN = 256
GRID_W = 64
HEAD_DIM = 128
GLA_HEADS = 4
GLA_DK = 128
GLA_DV = 256
GLA_GATE_RANK = 16
GLA_GATE_NORM = 16.0
GLA_CHUNK = 64
SWA_HEADS = 8
SWA_KV_HEADS = 2
SWA_GROUP = SWA_HEADS // SWA_KV_HEADS
SWA_WINDOW = 128
NA_HEADS = 8
NA_KH = 8
NA_KW = 16
N_BRANCH = 3
BRANCH_W = 1024
D_FF = 4096
MACARON_W = 0.5
N_MOD = 9
ROPE_BASE = 10000.0
EPS = 1e-6
NEG_INF = -1e30
MOD_ROWS = 8

A_GQ, A_GK, A_GV, A_GO, A_GD = 0, 512, 1024, 2048, 3072
A_COLS = 3200
C_START = 3104
C_SQ, C_SK, C_SV = 0, 1024, 1280
C_NQ, C_NK, C_NV = 1536, 2560, 3584
C_QKV_COLS = 4608
GATE_COLS = N_BRANCH * D_MODEL

V7X_VMEM_BYTES = 64 * 1024 * 1024
V7X_VMEM_RESERVE = 6 * 1024 * 1024
QBLK = 128
NA_SLAB_ROWS = 10


def _vmem_limit(pipelined_bytes, scratch_bytes=0, temp_bytes=0):
    need = 2 * pipelined_bytes + scratch_bytes + temp_bytes + (4 << 20)
    return int(min(max(need, 16 << 20), V7X_VMEM_BYTES - V7X_VMEM_RESERVE))


def _nbytes(shape, dtype):
    return int(np.prod(shape)) * jnp.dtype(dtype).itemsize


def _adaln_kernel(v_ref, down_ref, up_ref, bias_ref, o_ref):
    v = v_ref[...]
    t = jnp.dot(jax.nn.silu(v), down_ref[...], preferred_element_type=F32, precision=HIGHEST)
    o_ref[...] = jnp.dot(t, up_ref[...], preferred_element_type=F32, precision=HIGHEST) + bias_ref[...]


def _adaln(v8, ada_down, ada_up, ada_bias):
    depth, d, rank = ada_down.shape
    blocks = (_nbytes((MOD_ROWS, d), F32) + _nbytes((d, rank), F32) + _nbytes((rank, d), F32)
              + _nbytes((1, d), F32) + _nbytes((MOD_ROWS, d), F32))
    return pl.pallas_call(
        _adaln_kernel,
        grid=(depth, N_MOD),
        in_specs=[
            pl.BlockSpec((MOD_ROWS, d), lambda l, j: (0, 0)),
            pl.BlockSpec((None, d, rank), lambda l, j: (l, 0, 0)),
            pl.BlockSpec((None, rank, d), lambda l, j: (l, 0, j)),
            pl.BlockSpec((None, 1, d), lambda l, j: (l, 0, j)),
        ],
        out_specs=pl.BlockSpec((None, None, MOD_ROWS, d), lambda l, j: (l, j, 0, 0)),
        out_shape=jax.ShapeDtypeStruct((depth, N_MOD, MOD_ROWS, d), F32),
        compiler_params=pltpu.CompilerParams(
            dimension_semantics=("arbitrary", "arbitrary"), vmem_limit_bytes=_vmem_limit(blocks)),
        name="adaln",
    )(v8, ada_down, ada_up, ada_bias.reshape(depth, 1, -1))


def _mod_row_index(i, bm, seq):
    return jnp.minimum((i * bm) // seq, MOD_ROWS - 1)


def _mod_spec(bm, seq, d):
    return pl.BlockSpec((None, 1, d), lambda i, j: (_mod_row_index(i, bm, seq), 0, 0))


NORM_ROWS = 32


def _norm_prologue(h_ref, g_ref, sh_ref, sc_ref, xn_ref, bm):
    gain = g_ref[...]
    scale1 = 1.0 + sc_ref[...]
    shift = sh_ref[...]

    def body(c, carry):
        r0 = pl.multiple_of(c * NORM_ROWS, NORM_ROWS)
        x = h_ref[pl.ds(r0, NORM_ROWS), :]
        y = x * lax.rsqrt(jnp.mean(x * x, axis=-1, keepdims=True) + EPS)
        xn_ref[pl.ds(r0, NORM_ROWS), :] = ((y * gain) * scale1 + shift).astype(BF16)
        return carry

    lax.fori_loop(0, bm // NORM_ROWS, body, 0)


def _norm_mm_kernel(h_ref, g_ref, sh_ref, sc_ref, w_ref, o_ref, xn_ref, *, bm):
    @pl.when(pl.program_id(1) == 0)
    def _():
        _norm_prologue(h_ref, g_ref, sh_ref, sc_ref, xn_ref, bm)

    o_ref[...] = jnp.dot(xn_ref[...], w_ref[...], preferred_element_type=F32).astype(o_ref.dtype)


def _norm_swiglu_kernel(h_ref, g_ref, sh_ref, sc_ref, wa_ref, wb_ref, o_ref, xn_ref, *, bm):
    @pl.when(pl.program_id(1) == 0)
    def _():
        _norm_prologue(h_ref, g_ref, sh_ref, sc_ref, xn_ref, bm)

    xn = xn_ref[...]
    a = jnp.dot(xn, wa_ref[...], preferred_element_type=F32)
    b = jnp.dot(xn, wb_ref[...], preferred_element_type=F32)
    o_ref[...] = (jax.nn.silu(a) * b).astype(o_ref.dtype)


def _norm_matmul(h, rows, gain, shift, scale, w, widx, *, seq, bm, bn):
    d = h.shape[1]
    n = w.shape[2]
    blocks = (_nbytes((bm, d), F32) + _nbytes((d, bn), BF16) + _nbytes((bm, bn), F32)
              + _nbytes((bm, d), BF16) + 3 * _nbytes((1, d), F32))
    return pl.pallas_call(
        functools.partial(_norm_mm_kernel, bm=bm),
        grid=(rows // bm, n // bn),
        in_specs=[
            pl.BlockSpec((bm, d), lambda i, j: (i, 0)),
            pl.BlockSpec((1, d), lambda i, j: (0, 0)),
            _mod_spec(bm, seq, d),
            _mod_spec(bm, seq, d),
            pl.BlockSpec((None, d, bn), lambda i, j: (widx, 0, j)),
        ],
        out_specs=[pl.BlockSpec((bm, bn), lambda i, j: (i, j)),
                   pl.BlockSpec((bm, d), lambda i, j: (i, 0))],
        out_shape=[jax.ShapeDtypeStruct((rows, n), F32), jax.ShapeDtypeStruct((rows, d), BF16)],
        compiler_params=pltpu.CompilerParams(
            dimension_semantics=("arbitrary", "arbitrary"),
            vmem_limit_bytes=_vmem_limit(blocks, 0, _nbytes((bm, bn), F32))),
        name="norm_matmul",
    )(h, gain, shift, scale, w)


def _norm_swiglu(h, rows, gain, shift, scale, w_in, widx, *, seq, bm, bn):
    d = h.shape[1]
    f = w_in.shape[2] // 2
    nb = f // bn
    blocks = (_nbytes((bm, d), F32) + 2 * _nbytes((d, bn), BF16) + _nbytes((bm, bn), BF16)
              + 3 * _nbytes((1, d), F32))
    return pl.pallas_call(
        functools.partial(_norm_swiglu_kernel, bm=bm),
        grid=(rows // bm, nb),
        in_specs=[
            pl.BlockSpec((bm, d), lambda i, j: (i, 0)),
            pl.BlockSpec((1, d), lambda i, j: (0, 0)),
            _mod_spec(bm, seq, d),
            _mod_spec(bm, seq, d),
            pl.BlockSpec((None, d, bn), lambda i, j: (widx, 0, j)),
            pl.BlockSpec((None, d, bn), lambda i, j: (widx, 0, j + nb)),
        ],
        out_specs=pl.BlockSpec((bm, bn), lambda i, j: (i, j)),
        out_shape=jax.ShapeDtypeStruct((rows, f), BF16),
        scratch_shapes=[pltpu.VMEM((bm, d), BF16)],
        compiler_params=pltpu.CompilerParams(
            dimension_semantics=("arbitrary", "arbitrary"),
            vmem_limit_bytes=_vmem_limit(blocks, _nbytes((bm, d), BF16), 3 * _nbytes((bm, bn), F32))),
        name="norm_swiglu",
    )(h, gain, shift, scale, w_in, w_in)


def _mm_kernel(x_ref, w_ref, o_ref, *, sigmoid):
    y = jnp.dot(x_ref[...], w_ref[...], preferred_element_type=F32)
    if sigmoid:
        y = jax.nn.sigmoid(y)
    o_ref[...] = y.astype(o_ref.dtype)


def _matmul(x, w, widx, rows, col0, n, *, bm, bn, out_dtype, sigmoid=False):
    k = x.shape[1]
    jb0 = col0 // bn
    blocks = _nbytes((bm, k), BF16) + _nbytes((k, bn), BF16) + _nbytes((bm, bn), out_dtype)
    return pl.pallas_call(
        functools.partial(_mm_kernel, sigmoid=sigmoid),
        grid=(rows // bm, n // bn),
        in_specs=[
            pl.BlockSpec((bm, k), lambda i, j: (i, 0)),
            pl.BlockSpec((None, k, bn), lambda i, j: (widx, 0, jb0 + j)),
        ],
        out_specs=pl.BlockSpec((bm, bn), lambda i, j: (i, j)),
        out_shape=jax.ShapeDtypeStruct((rows, n), out_dtype),
        compiler_params=pltpu.CompilerParams(
            dimension_semantics=("arbitrary", "arbitrary"),
            vmem_limit_bytes=_vmem_limit(blocks, 0, 2 * _nbytes((bm, bn), F32))),
        name="matmul_sigmoid" if sigmoid else "matmul",
    )(x, w)


def _resid_mm_kernel(x_ref, w_ref, h_ref, gate_ref, o_ref, *, weight):
    y = jnp.dot(x_ref[...], w_ref[...], preferred_element_type=F32)
    if weight != 1.0:
        o_ref[...] = h_ref[...] + weight * gate_ref[...] * y
    else:
        o_ref[...] = h_ref[...] + gate_ref[...] * y


def _resid_matmul(x, w, widx, h, rows, gate, *, weight, seq, bm, bn):
    k = x.shape[1]
    n = w.shape[2]
    blocks = (_nbytes((bm, k), BF16) + _nbytes((k, bn), BF16) + 2 * _nbytes((bm, bn), F32)
              + _nbytes((1, bn), F32))
    return pl.pallas_call(
        functools.partial(_resid_mm_kernel, weight=weight),
        grid=(rows // bm, n // bn),
        in_specs=[
            pl.BlockSpec((bm, k), lambda i, j: (i, 0)),
            pl.BlockSpec((None, k, bn), lambda i, j: (widx, 0, j)),
            pl.BlockSpec((bm, bn), lambda i, j: (i, j)),
            pl.BlockSpec((None, 1, bn), lambda i, j: (_mod_row_index(i, bm, seq), 0, j)),
        ],
        out_specs=pl.BlockSpec((bm, bn), lambda i, j: (i, j)),
        out_shape=jax.ShapeDtypeStruct((rows, n), F32),
        compiler_params=pltpu.CompilerParams(
            dimension_semantics=("arbitrary", "arbitrary"),
            vmem_limit_bytes=_vmem_limit(blocks, 0, _nbytes((bm, bn), F32))),
        name="resid_matmul",
    )(x, w, h, gate)


def _merge_kernel(oa_ref, ob_ref, oc_ref, w_ref, ga_ref, gb_ref, gc_ref, y_ref):
    y = ga_ref[...].astype(F32) * jnp.dot(oa_ref[...], w_ref[0], preferred_element_type=F32)
    y = y + gb_ref[...].astype(F32) * jnp.dot(ob_ref[...], w_ref[1], preferred_element_type=F32)
    y = y + gc_ref[...].astype(F32) * jnp.dot(oc_ref[...], w_ref[2], preferred_element_type=F32)
    y_ref[...] = y.astype(y_ref.dtype)


def _merge(oa, ob, oc, w_branch, widx, gates, rows, *, bm, bn):
    kb = oa.shape[1]
    d = w_branch.shape[3]
    per_branch = d // bn
    blocks = (3 * _nbytes((bm, kb), BF16) + _nbytes((N_BRANCH, kb, bn), BF16) + 4 * _nbytes((bm, bn), BF16))
    o_spec = pl.BlockSpec((bm, kb), lambda i, j: (i, 0))

    def gate_spec(br):
        return pl.BlockSpec((bm, bn), lambda i, j: (i, br * per_branch + j))

    return pl.pallas_call(
        _merge_kernel,
        grid=(rows // bm, d // bn),
        in_specs=[o_spec, o_spec, o_spec,
                  pl.BlockSpec((None, N_BRANCH, kb, bn), lambda i, j: (widx, 0, 0, j)),
                  gate_spec(0), gate_spec(1), gate_spec(2)],
        out_specs=pl.BlockSpec((bm, bn), lambda i, j: (i, j)),
        out_shape=jax.ShapeDtypeStruct((rows, d), BF16),
        compiler_params=pltpu.CompilerParams(
            dimension_semantics=("arbitrary", "arbitrary"),
            vmem_limit_bytes=_vmem_limit(blocks, 0, 4 * _nbytes((bm, bn), F32))),
        name="merge",
    )(oa, ob, oc, w_branch, gates, gates, gates)


GLA_BLOCK = 256
GLA_UNROLL = 4
_NT = (((1,), (1,)), ((), ()))
_TN = (((0,), (0,)), ((), ()))


def _split3(x):
    hi = x.astype(BF16)
    r1 = x - hi.astype(F32)
    mid = r1.astype(BF16)
    lo = (r1 - mid.astype(F32)).astype(BF16)
    return hi, mid, lo


def _sum_dot(op, parts):
    acc = jnp.dot(op, parts[0], preferred_element_type=F32)
    for p in parts[1:]:
        acc = acc + jnp.dot(op, p, preferred_element_type=F32)
    return acc


def _gla_kernel(ql_ref, kl_ref, vl_ref, ogl_ref, gdl_ref, qc_ref, kc_ref, vc_ref, ogc_ref, gdc_ref,
                up_ref, gb_ref, gn_ref, al_ref, ac_ref,
                qe_ref, edec_ref, of_ref, ob_ref, kv_ref, st_ref, *, t_lat, t_ctx):
    c = GLA_CHUNK
    br = GLA_BLOCK
    cpb = br // c
    row = lax.broadcasted_iota(jnp.int32, (br, br), 0)
    col = lax.broadcasted_iota(jnp.int32, (br, br), 1)
    same = (row // c) == (col // c)
    keep = (same & (col <= row), same & (col >= row))
    tri = (keep[0].astype(BF16), keep[1].astype(BF16))
    blk = same.astype(BF16)
    q_scale = GLA_DK ** -0.5
    gain = gn_ref[...]
    o_refs = (of_ref, ob_ref)

    st_ref[...] = jnp.zeros_like(st_ref)

    def segment(q_ref, k_ref, v_ref, og_ref, gd_ref, out_ref, t):
        nc = t // c

        def block_body(r, carry):
            rows = pl.ds(pl.multiple_of(r * br, br), br)
            gd = gd_ref[rows, :]
            q = q_ref[rows, :] * q_scale
            k = k_ref[rows, :]
            v = v_ref[rows, :].astype(BF16)
            for d in range(2):
                z = jnp.dot(gd, up_ref[d], preferred_element_type=F32) + gb_ref[d]
                parts = _split3(jax.nn.log_sigmoid(z) * (1.0 / GLA_GATE_NORM))
                cum = _sum_dot(tri[d], parts)
                cum_end = _sum_dot(blk, parts)
                qe = (q * jnp.exp(cum)).astype(BF16)
                kinv = (k * jnp.exp(-cum)).astype(BF16)
                kdec = (k * jnp.exp(cum_end - cum)).astype(BF16)
                qe_ref[d, rows, :] = qe
                edec_ref[d, rows, :] = jnp.exp(cum_end)
                a = lax.dot_general(qe, kinv, _NT, preferred_element_type=F32)
                a = jnp.where(keep[d], a, 0.0).astype(BF16)
                o_refs[d][rows, :] = jnp.dot(a, v, preferred_element_type=F32)
                for j in range(cpb):
                    sl = slice(j * c, (j + 1) * c)
                    kv_ref[d, r * cpb + j] = lax.dot_general(v[sl], kdec[sl], _TN, preferred_element_type=F32)
            return carry

        lax.fori_loop(0, t // br, block_body, 0)

        def scan_body(n, carry):
            for d in range(2):
                m = n if d == 0 else nc - 1 - n
                rows = pl.ds(pl.multiple_of(m * c, c), c)
                st = st_ref[d]
                o_refs[d][rows, :] += lax.dot_general(qe_ref[d, rows, :], st.astype(BF16), _NT,
                                                      preferred_element_type=F32)
                st_ref[d] = st * edec_ref[d, pl.ds(pl.multiple_of(m * c, c), 1), :] + kv_ref[d, m]
            return carry

        lax.fori_loop(0, nc, scan_body, 0, unroll=GLA_UNROLL)

        def out_body(n, carry):
            rows = pl.ds(pl.multiple_of(n * c, c), c)
            o = of_ref[rows, :] + ob_ref[rows, :]
            y = o * lax.rsqrt(jnp.mean(o * o, axis=-1, keepdims=True) + EPS)
            y = (y * gain) * jax.nn.silu(og_ref[rows, :])
            out_ref[rows, :] = y.astype(out_ref.dtype)
            return carry

        lax.fori_loop(0, nc, out_body, 0, unroll=GLA_UNROLL)

    segment(qc_ref, kc_ref, vc_ref, ogc_ref, gdc_ref, ac_ref, t_ctx)
    segment(ql_ref, kl_ref, vl_ref, ogl_ref, gdl_ref, al_ref, t_lat)


def _gla(pa, gate_up_pad, gate_bias, gla_norm, *, batch):
    t_lat, t_ctx = SEQ, CTX_LEN
    ctx0 = batch * t_lat // t_ctx
    dk, dv = GLA_DK, GLA_DV
    nc = t_lat // GLA_CHUNK

    def lat(width, col0):
        return pl.BlockSpec((t_lat, width), lambda b, h: (b, col0 // width + h))

    def ctx(width, col0):
        return pl.BlockSpec((t_ctx, width), lambda b, h: (ctx0 + b, col0 // width + h))

    lat_gd = pl.BlockSpec((t_lat, 128), lambda b, h: (b, A_GD // 128))
    ctx_gd = pl.BlockSpec((t_ctx, 128), lambda b, h: (ctx0 + b, A_GD // 128))
    blocks = ((t_lat + t_ctx) * (3 * dk + 2 * dv) * 4 + _nbytes((2, 128, dk), F32)
              + (t_lat + t_ctx) * dv * 2)
    scratch_shapes = [pltpu.VMEM((2, t_lat, dk), BF16), pltpu.VMEM((2, t_lat, dk), F32),
                      pltpu.VMEM((t_lat, dv), F32), pltpu.VMEM((t_lat, dv), F32),
                      pltpu.VMEM((2, nc, dv, dk), F32), pltpu.VMEM((2, dv, dk), F32)]
    scratch = (_nbytes((2, t_lat, dk), BF16) + _nbytes((2, t_lat, dk), F32) + 2 * _nbytes((t_lat, dv), F32)
               + _nbytes((2, nc + 1, dv, dk), F32))
    return pl.pallas_call(
        functools.partial(_gla_kernel, t_lat=t_lat, t_ctx=t_ctx),
        grid=(batch, GLA_HEADS),
        in_specs=[lat(dk, A_GQ), lat(dk, A_GK), lat(dv, A_GV), lat(dv, A_GO), lat_gd,
                  ctx(dk, A_GQ), ctx(dk, A_GK), ctx(dv, A_GV), ctx(dv, A_GO), ctx_gd,
                  pl.BlockSpec((2, 128, dk), lambda b, h: (0, 0, h)),
                  pl.BlockSpec((2, 1, dk), lambda b, h: (0, 0, h)),
                  pl.BlockSpec((1, dv), lambda b, h: (0, 0))],
        out_specs=[pl.BlockSpec((t_lat, dv), lambda b, h: (b, h)),
                   pl.BlockSpec((t_ctx, dv), lambda b, h: (b, h))],
        out_shape=[jax.ShapeDtypeStruct((batch * t_lat, GLA_HEADS * dv), BF16),
                   jax.ShapeDtypeStruct((batch * t_ctx, GLA_HEADS * dv), BF16)],
        scratch_shapes=scratch_shapes,
        compiler_params=pltpu.CompilerParams(
            dimension_semantics=("arbitrary", "arbitrary"),
            vmem_limit_bytes=_vmem_limit(blocks, scratch, 6 << 20)),
        name="gla",
    )(pa, pa, pa, pa, pa, pa, pa, pa, pa, pa, gate_up_pad, gate_bias, gla_norm)


def _rms(x, gain):
    return (x * lax.rsqrt(jnp.mean(x * x, axis=-1, keepdims=True) + EPS)) * gain


def _rope(x, cos, sin_signed):
    lane = lax.broadcasted_iota(jnp.int32, x.shape, 1)
    partner = jnp.where((lane % 64) < 32, pltpu.roll(x, 96, 1), pltpu.roll(x, 32, 1))
    return x * cos + partner * sin_signed


def _q_row_block(b, qb, *, batch, n_lat, n_ctx):
    return jnp.where(qb < n_lat, b * n_lat + qb, batch * n_lat + b * n_ctx + (qb - n_lat))


KV_PREP_ROWS = 256


def _swa_kernel(sink_ref, q_ref, kl_ref, vl_ref, kc_ref, vc_ref, cq_ref, sq_ref, ck_ref, sk_ref,
                qg_ref, kg_ref, o_ref, kn_ref, vb_ref, kcn_ref, vcb_ref, *, t_lat, n_lat):
    qb = pl.program_id(1)
    g = SWA_GROUP
    dh = HEAD_DIM
    scale = dh ** -0.5

    @pl.when(qb == 0)
    def _prep():
        k_gain = kg_ref[...]

        def body(r, carry):
            r0 = pl.multiple_of(r * KV_PREP_ROWS, KV_PREP_ROWS)
            rows = pl.ds(r0, KV_PREP_ROWS)
            cos, sin = ck_ref[rows, :], sk_ref[rows, :]
            for kv in range(SWA_KV_HEADS):
                cols = slice(kv * dh, (kv + 1) * dh)
                kn_ref[rows, cols] = _rope(_rms(kl_ref[rows, cols], k_gain), cos, sin).astype(BF16)
            vb_ref[rows, :] = vl_ref[rows, :].astype(BF16)
            return carry

        lax.fori_loop(0, t_lat // KV_PREP_ROWS, body, 0)
        for kv in range(SWA_KV_HEADS):
            cols = slice(kv * dh, (kv + 1) * dh)
            kcn_ref[:, cols] = _rms(kc_ref[:, cols], k_gain).astype(BF16)
        vcb_ref[...] = vc_ref[...].astype(BF16)

    def attend(latent):
        cos, sin = cq_ref[...], sq_ref[...]
        q_gain = qg_ref[...]
        head_of_row = lax.broadcasted_iota(jnp.int32, (g * QBLK, 1), 0) // QBLK
        for kv in range(SWA_KV_HEADS):
            cols = slice(kv * dh, (kv + 1) * dh)
            heads = [_rope(_rms(q_ref[:, (kv * g + i) * dh:(kv * g + i + 1) * dh], q_gain), cos, sin).astype(BF16)
                     for i in range(g)]
            qs = jnp.concatenate(heads, axis=0)
            sink = jnp.full((g * QBLK, 1), sink_ref[kv * g], F32)
            for i in range(1, g):
                sink = jnp.where(head_of_row == i, sink_ref[kv * g + i], sink)
            s_ctx = lax.dot_general(qs, kcn_ref[:, cols], _NT, preferred_element_type=F32) * scale
            m = jnp.maximum(jnp.max(s_ctx, axis=-1, keepdims=True), sink)
            if latent:
                span = 3 * QBLK
                start = pl.multiple_of(jnp.clip((qb - 1) * QBLK, 0, t_lat - span), QBLK)
                s_loc = lax.dot_general(qs, kn_ref[pl.ds(start, span), cols], _NT,
                                        preferred_element_type=F32) * scale
                qpos = qb * QBLK + lax.broadcasted_iota(jnp.int32, (g * QBLK, span), 0) % QBLK
                kpos = start + lax.broadcasted_iota(jnp.int32, (g * QBLK, span), 1)
                s_loc = jnp.where(jnp.abs(kpos - qpos) <= SWA_WINDOW, s_loc, NEG_INF)
                m = jnp.maximum(m, jnp.max(s_loc, axis=-1, keepdims=True))
                p_loc = jnp.exp(s_loc - m)
                den = jnp.sum(p_loc, axis=-1, keepdims=True)
                o = jnp.dot(p_loc.astype(BF16), vb_ref[pl.ds(start, span), cols], preferred_element_type=F32)
            p_ctx = jnp.exp(s_ctx - m)
            o_ctx = jnp.dot(p_ctx.astype(BF16), vcb_ref[:, cols], preferred_element_type=F32)
            den_ctx = jnp.sum(p_ctx, axis=-1, keepdims=True) + jnp.exp(sink - m)
            o, den = (o + o_ctx, den + den_ctx) if latent else (o_ctx, den_ctx)
            o = o / den
            for i in range(g):
                o_ref[:, (kv * g + i) * dh:(kv * g + i + 1) * dh] = o[i * QBLK:(i + 1) * QBLK, :].astype(o_ref.dtype)

    pl.when(qb < n_lat)(functools.partial(attend, True))
    pl.when(qb >= n_lat)(functools.partial(attend, False))


def _swa(pc, sink, cos_tab, sin_tab, q_gain, k_gain, *, batch, with_ctx_out):
    t_lat, t_ctx, dh = SEQ, CTX_LEN, HEAD_DIM
    qw, kw = SWA_HEADS * dh, SWA_KV_HEADS * dh
    n_lat, n_ctx = t_lat // QBLK, t_ctx // QBLK
    n_q = n_lat + (n_ctx if with_ctx_out else 0)
    rows_out = batch * (t_lat + (t_ctx if with_ctx_out else 0))
    ctx0 = batch * t_lat // t_ctx
    qmap = functools.partial(_q_row_block, batch=batch, n_lat=n_lat, n_ctx=n_ctx)
    blocks = (_nbytes((QBLK, qw), F32) + 2 * _nbytes((t_lat + t_ctx, kw), F32) + 2 * _nbytes((QBLK, dh), F32)
              + 2 * _nbytes((t_lat, dh), F32) + _nbytes((QBLK, qw), BF16))
    scratch = 2 * _nbytes((t_lat + t_ctx, kw), BF16)
    return pl.pallas_call(
        functools.partial(_swa_kernel, t_lat=t_lat, n_lat=n_lat),
        grid=(batch, n_q),
        in_specs=[
            pl.BlockSpec(memory_space=pltpu.SMEM),
            pl.BlockSpec((QBLK, qw), lambda b, qb: (qmap(b, qb), C_SQ // qw)),
            pl.BlockSpec((t_lat, kw), lambda b, qb: (b, C_SK // kw)),
            pl.BlockSpec((t_lat, kw), lambda b, qb: (b, C_SV // kw)),
            pl.BlockSpec((t_ctx, kw), lambda b, qb: (ctx0 + b, C_SK // kw)),
            pl.BlockSpec((t_ctx, kw), lambda b, qb: (ctx0 + b, C_SV // kw)),
            pl.BlockSpec((QBLK, dh), lambda b, qb: (qb, 0)),
            pl.BlockSpec((QBLK, dh), lambda b, qb: (qb, 0)),
            pl.BlockSpec((t_lat, dh), lambda b, qb: (0, 0)),
            pl.BlockSpec((t_lat, dh), lambda b, qb: (0, 0)),
            pl.BlockSpec((1, dh), lambda b, qb: (0, 0)),
            pl.BlockSpec((1, dh), lambda b, qb: (0, 0)),
        ],
        out_specs=pl.BlockSpec((QBLK, qw), lambda b, qb: (qmap(b, qb), 0)),
        out_shape=jax.ShapeDtypeStruct((rows_out, qw), BF16),
        scratch_shapes=[pltpu.VMEM((t_lat, kw), BF16), pltpu.VMEM((t_lat, kw), BF16),
                        pltpu.VMEM((t_ctx, kw), BF16), pltpu.VMEM((t_ctx, kw), BF16)],
        compiler_params=pltpu.CompilerParams(
            dimension_semantics=("arbitrary", "arbitrary"),
            vmem_limit_bytes=_vmem_limit(blocks, scratch, 12 << 20)),
        name="swa",
    )(sink, pc, pc, pc, pc, pc, cos_tab, sin_tab, cos_tab, sin_tab, q_gain, k_gain)


NA_HEADS_PER_STEP = 4


def _na_slab_start(qb):
    r = qb * (QBLK // GRID_W)
    rows = SEQ // GRID_W
    return jnp.minimum(jnp.clip(r - NA_KH // 2, 0, rows - NA_KH), rows - NA_SLAB_ROWS)


def _na_bias_variant(qb, n_lat):
    return jnp.where(qb < 2, qb, jnp.where(qb < n_lat - 2, 2, jnp.minimum(qb, n_lat - 1) - (n_lat - 5)))


def _na_kernel(q_ref, kl_ref, vl_ref, kc_ref, vc_ref, bias_ref, qg_ref, kg_ref, o_ref,
               kn_ref, vb_ref, kcn_ref, vcb_ref, *, t_lat, n_lat):
    qb = pl.program_id(2)
    dh = HEAD_DIM
    scale = dh ** -0.5

    @pl.when(qb == 0)
    def _prep():
        k_gain = kg_ref[...]

        def body(r, carry):
            r0 = pl.multiple_of(r * KV_PREP_ROWS, KV_PREP_ROWS)
            rows = pl.ds(r0, KV_PREP_ROWS)
            for h in range(NA_HEADS_PER_STEP):
                cols = slice(h * dh, (h + 1) * dh)
                kn_ref[rows, cols] = _rms(kl_ref[rows, cols], k_gain).astype(BF16)
            vb_ref[rows, :] = vl_ref[rows, :].astype(BF16)
            return carry

        lax.fori_loop(0, t_lat // KV_PREP_ROWS, body, 0)
        for h in range(NA_HEADS_PER_STEP):
            cols = slice(h * dh, (h + 1) * dh)
            kcn_ref[:, cols] = _rms(kc_ref[:, cols], k_gain).astype(BF16)
        vcb_ref[...] = vc_ref[...].astype(BF16)

    def attend(latent):
        q_gain = qg_ref[...]
        for h in range(NA_HEADS_PER_STEP):
            cols = slice(h * dh, (h + 1) * dh)
            qn = _rms(q_ref[:, cols], q_gain).astype(BF16)
            s_ctx = lax.dot_general(qn, kcn_ref[:, cols], _NT, preferred_element_type=F32) * scale
            m = jnp.max(s_ctx, axis=-1, keepdims=True)
            if latent:
                span = NA_SLAB_ROWS * GRID_W
                start = pl.multiple_of(_na_slab_start(qb) * GRID_W, GRID_W)
                s_loc = lax.dot_general(qn, kn_ref[pl.ds(start, span), cols], _NT,
                                        preferred_element_type=F32) * scale
                s_loc = s_loc + bias_ref[h]
                m = jnp.maximum(m, jnp.max(s_loc, axis=-1, keepdims=True))
                p_loc = jnp.exp(s_loc - m)
                den = jnp.sum(p_loc, axis=-1, keepdims=True)
                o = jnp.dot(p_loc.astype(BF16), vb_ref[pl.ds(start, span), cols], preferred_element_type=F32)
            p_ctx = jnp.exp(s_ctx - m)
            o_ctx = jnp.dot(p_ctx.astype(BF16), vcb_ref[:, cols], preferred_element_type=F32)
            den_ctx = jnp.sum(p_ctx, axis=-1, keepdims=True)
            o, den = (o + o_ctx, den + den_ctx) if latent else (o_ctx, den_ctx)
            o_ref[:, cols] = (o / den).astype(o_ref.dtype)

    pl.when(qb < n_lat)(functools.partial(attend, True))
    pl.when(qb >= n_lat)(functools.partial(attend, False))


def _na(pc, bias_tab, q_gain, k_gain, *, batch, with_ctx_out):
    t_lat, t_ctx, dh = SEQ, CTX_LEN, HEAD_DIM
    hb = NA_HEADS_PER_STEP
    w = hb * dh
    n_lat, n_ctx = t_lat // QBLK, t_ctx // QBLK
    n_q = n_lat + (n_ctx if with_ctx_out else 0)
    rows_out = batch * (t_lat + (t_ctx if with_ctx_out else 0))
    ctx0 = batch * t_lat // t_ctx
    span = NA_SLAB_ROWS * GRID_W
    qmap = functools.partial(_q_row_block, batch=batch, n_lat=n_lat, n_ctx=n_ctx)
    blocks = (_nbytes((QBLK, w), F32) + 2 * _nbytes((t_lat + t_ctx, w), F32) + _nbytes((hb, QBLK, span), F32)
              + _nbytes((QBLK, w), BF16))
    scratch = 2 * _nbytes((t_lat + t_ctx, w), BF16)
    return pl.pallas_call(
        functools.partial(_na_kernel, t_lat=t_lat, n_lat=n_lat),
        grid=(batch, NA_HEADS // hb, n_q),
        in_specs=[
            pl.BlockSpec((QBLK, w), lambda b, hg, qb: (qmap(b, qb), C_NQ // w + hg)),
            pl.BlockSpec((t_lat, w), lambda b, hg, qb: (b, C_NK // w + hg)),
            pl.BlockSpec((t_lat, w), lambda b, hg, qb: (b, C_NV // w + hg)),
            pl.BlockSpec((t_ctx, w), lambda b, hg, qb: (ctx0 + b, C_NK // w + hg)),
            pl.BlockSpec((t_ctx, w), lambda b, hg, qb: (ctx0 + b, C_NV // w + hg)),
            pl.BlockSpec((hb, None, QBLK, span), lambda b, hg, qb: (hg, _na_bias_variant(qb, n_lat), 0, 0)),
            pl.BlockSpec((1, dh), lambda b, hg, qb: (0, 0)),
            pl.BlockSpec((1, dh), lambda b, hg, qb: (0, 0)),
        ],
        out_specs=pl.BlockSpec((QBLK, w), lambda b, hg, qb: (qmap(b, qb), hg)),
        out_shape=jax.ShapeDtypeStruct((rows_out, NA_HEADS * dh), BF16),
        scratch_shapes=[pltpu.VMEM((t_lat, w), BF16), pltpu.VMEM((t_lat, w), BF16),
                        pltpu.VMEM((t_ctx, w), BF16), pltpu.VMEM((t_ctx, w), BF16)],
        compiler_params=pltpu.CompilerParams(
            dimension_semantics=("arbitrary", "arbitrary", "arbitrary"),
            vmem_limit_bytes=_vmem_limit(blocks, scratch, 12 << 20)),
        name="natten",
    )(pc, pc, pc, pc, pc, bias_tab, q_gain, k_gain)


def _rope_tables():
    quarter = HEAD_DIM // 4
    pos = jnp.arange(SEQ)
    rows = (pos // GRID_W).astype(F32)
    cols = (pos % GRID_W).astype(F32)
    inv = ROPE_BASE ** (-jnp.arange(quarter, dtype=F32) / quarter)
    ang_r = rows[:, None] * inv[None, :]
    ang_c = cols[:, None] * inv[None, :]
    cos = jnp.concatenate([jnp.cos(ang_r), jnp.cos(ang_r), jnp.cos(ang_c), jnp.cos(ang_c)], axis=-1)
    sin = jnp.concatenate([-jnp.sin(ang_r), jnp.sin(ang_r), -jnp.sin(ang_c), jnp.sin(ang_c)], axis=-1)
    cos = jnp.concatenate([cos, jnp.ones((CTX_LEN, HEAD_DIM), F32)], axis=0)
    sin = jnp.concatenate([sin, jnp.zeros((CTX_LEN, HEAD_DIM), F32)], axis=0)
    return cos, sin


def _na_bias_tables(rpb):
    n_layers, n_heads = rpb.shape[:2]
    rows = SEQ // GRID_W
    n_lat = SEQ // QBLK
    rq_per = QBLK // GRID_W
    reps = np.array([0, 1, 2, n_lat - 2, n_lat - 1])
    r = reps * rq_per
    start = np.minimum(np.clip(r - NA_KH // 2, 0, rows - NA_KH), rows - NA_SLAB_ROWS)
    rq = r[:, None] + np.arange(rq_per)[None, :]
    kr = start[:, None] + np.arange(NA_SLAB_ROWS)[None, :]
    r0 = np.clip(rq - NA_KH // 2, 0, rows - NA_KH)
    valid_r = (kr[:, None, :] >= r0[:, :, None]) & (kr[:, None, :] < r0[:, :, None] + NA_KH)
    dr = np.clip(kr[:, None, :] - rq[:, :, None] + NA_KH - 1, 0, 2 * NA_KH - 2)
    qc = np.arange(GRID_W)
    kc = np.arange(GRID_W)
    cs = np.clip(qc - NA_KW // 2, 0, GRID_W - NA_KW)
    valid_c = (kc[None, :] >= cs[:, None]) & (kc[None, :] < cs[:, None] + NA_KW)
    dc = np.clip(kc[None, :] - qc[:, None] + NA_KW - 1, 0, 2 * NA_KW - 2)
    pick_r = np.eye(2 * NA_KH - 1, dtype=np.float32)[dr.reshape(-1)]
    pick_c = np.eye(2 * NA_KW - 1, dtype=np.float32)[dc.reshape(-1)]
    t = jnp.einsum("nr,lhrc->lhnc", pick_r, rpb.astype(F32), precision=HIGHEST)
    t = jnp.einsum("lhnc,xc->lhnx", t, pick_c, precision=HIGHEST)
    t = t.reshape(n_layers, n_heads, len(reps), rq_per, NA_SLAB_ROWS, GRID_W, GRID_W)
    t = t.transpose(0, 1, 2, 3, 5, 4, 6)
    valid = valid_r[:, :, None, :, None] & valid_c[None, None, :, None, :]
    t = jnp.where(valid[None, None], t, NEG_INF)
    return t.reshape(n_layers, n_heads, len(reps), QBLK, NA_SLAB_ROWS * GRID_W)


def _pad_gate_up(gate_up):
    r = gate_up.shape[1]
    out = jnp.zeros((2, 128, gate_up.shape[2]), F32)
    out = out.at[0, 0:r].set(gate_up[0])
    out = out.at[1, r:2 * r].set(gate_up[1])
    return out


BM = 512
BM_PLAIN = 1024
BN = 1024
BN_FF = 512
BN_A = 640
BN_C = 768
BN_MERGE = 512


def kernel(x, c, ctx, c_ctx, ada_down, ada_up, ada_bias, norm_gain, ffn_w_in, ffn_w_out, w_in,
           gla_gate_up, gla_gate_bias, gla_norm, swa_q_norm, swa_k_norm, swa_sink, na_q_norm,
           na_k_norm, na_rpb, w_branch, w_out):
    batch, seq, d = x.shape
    depth = ada_down.shape[0]
    assert (seq, d, ctx.shape[1]) == (SEQ, D_MODEL, CTX_LEN) and batch + 1 <= MOD_ROWS
    assert w_in.shape[2] == C_START + C_QKV_COLS + GATE_COLS
    lat_rows = batch * seq
    all_rows = lat_rows + batch * ctx.shape[1]

    v8 = jnp.concatenate([c, c_ctx[None], jnp.zeros((MOD_ROWS - batch - 1, d), F32)], axis=0)
    mods = _adaln(v8, ada_down, ada_up, ada_bias)
    mods = mods.reshape(depth, N_MOD, MOD_ROWS, 1, d)
    cos_tab, sin_tab = _rope_tables()
    na_bias = _na_bias_tables(na_rpb)

    ffn_in_bf = ffn_w_in.astype(BF16).reshape(depth * 2, d, -1)
    ffn_out_bf = ffn_w_out.astype(BF16).reshape(depth * 2, -1, d)
    w_a_bf = w_in[:, :, :A_COLS].astype(BF16)
    w_c_bf = w_in[:, :, C_START:].astype(BF16)
    w_branch_bf = w_branch.astype(BF16)
    w_out_bf = w_out.astype(BF16)

    h = jnp.concatenate([x.reshape(lat_rows, d), ctx.reshape(-1, d)], axis=0)
    mm = dict(seq=seq, bm=BM)
    bm_plain = BM_PLAIN if all_rows % BM_PLAIN == 0 else BM

    for l in range(depth):
        last = l == depth - 1
        rows_out = lat_rows if last else all_rows
        gain = norm_gain[l].reshape(3, 1, d)
        m = mods[l]

        g1 = _norm_swiglu(h, all_rows, gain[0], m[0], m[1], ffn_in_bf, 2 * l, bn=BN_FF, **mm)
        h = _resid_matmul(g1, ffn_out_bf, 2 * l, h, all_rows, m[2], weight=MACARON_W, bn=BN, **mm)

        pa, xn = _norm_matmul(h, all_rows, gain[1], m[3], m[4], w_a_bf, l, bn=BN_A, **mm)
        pc = _matmul(xn, w_c_bf, l, all_rows, 0, C_QKV_COLS, bm=bm_plain, bn=BN_C, out_dtype=F32)
        gates = _matmul(xn, w_c_bf, l, rows_out, C_QKV_COLS, GATE_COLS, bm=bm_plain, bn=BN_C,
                        out_dtype=BF16, sigmoid=True)
        a_lat, a_ctx = _gla(pa, _pad_gate_up(gla_gate_up[l]), gla_gate_bias[l].reshape(2, 1, -1),
                            gla_norm[l].reshape(1, -1), batch=batch)
        o_a = a_lat if last else jnp.concatenate([a_lat, a_ctx], axis=0)
        o_b = _swa(pc, swa_sink[l], cos_tab, sin_tab, swa_q_norm[l].reshape(1, -1), swa_k_norm[l].reshape(1, -1),
                   batch=batch, with_ctx_out=not last)
        o_c = _na(pc, na_bias[l], na_q_norm[l].reshape(1, -1), na_k_norm[l].reshape(1, -1),
                  batch=batch, with_ctx_out=not last)
        y = _merge(o_a, o_b, o_c, w_branch_bf, l, gates, rows_out, bm=bm_plain, bn=BN_MERGE)
        h = _resid_matmul(y, w_out_bf, l, h, rows_out, m[5], weight=1.0, bn=BN, **mm)

        g2 = _norm_swiglu(h, rows_out, gain[2], m[6], m[7], ffn_in_bf, 2 * l + 1, bn=BN_FF, **mm)
        h = _resid_matmul(g2, ffn_out_bf, 2 * l + 1, h, rows_out, m[8], weight=MACARON_W, bn=BN, **mm)

    return h.reshape(batch, seq, d)
```

```python
import functools

import jax
import jax.numpy as jnp
import numpy as np
from jax import lax
from jax.experimental import pallas as pl
from jax.experimental.pallas import tpu as pltpu

F32 = jnp.float32
BF16 = jnp.bfloat16
HIGHEST = lax.Precision.HIGHEST

D_MODEL = 4096
SEQ = 2048
CTX_LEN = 256
GRID_W = 64
HEAD_DIM = 128
GLA_HEADS = 4
GLA_DK = 128
GLA_DV = 256
GLA_GATE_RANK = 16
GLA_GATE_NORM = 16.0
GLA_CHUNK = 64
SWA_HEADS = 8
SWA_KV_HEADS = 2
SWA_GROUP = SWA_HEADS // SWA_KV_HEADS
SWA_WINDOW = 128
NA_HEADS = 8
NA_KH = 8
NA_KW = 16
N_BRANCH = 3
BRANCH_W = 1024
D_FF = 4096
MACARON_W = 0.5
N_MOD = 9
ROPE_BASE = 10000.0
EPS = 1e-6
NEG_INF = -1e30
MOD_ROWS = 8

A_GQ, A_GK, A_GV, A_GO, A_GD = 0, 512, 1024, 2048, 3072
A_COLS = 3200
C_START = 3104
C_SQ, C_SK, C_SV = 0, 1024, 1280
C_NQ, C_NK, C_NV = 1536, 2560, 3584
C_QKV_COLS = 4608
GATE_COLS = N_BRANCH * D_MODEL

V7X_VMEM_BYTES = 64 * 1024 * 1024
V7X_VMEM_RESERVE = 6 * 1024 * 1024
QBLK = 128
NA_SLAB_ROWS = 10


def _vmem_limit(pipelined_bytes, scratch_bytes=0, temp_bytes=0):
    need = 2 * pipelined_bytes + scratch_bytes + temp_bytes + (4 << 20)
    return int(min(max(need, 16 << 20), V7X_VMEM_BYTES - V7X_VMEM_RESERVE))


def _nbytes(shape, dtype):
    return int(np.prod(shape)) * jnp.dtype(dtype).itemsize


def _adaln_kernel(v_ref, down_ref, up_ref, bias_ref, o_ref):
    v = v_ref[...]
    t = jnp.dot(jax.nn.silu(v), down_ref[...], preferred_element_type=F32, precision=HIGHEST)
    o_ref[...] = jnp.dot(t, up_ref[...], preferred_element_type=F32, precision=HIGHEST) + bias_ref[...]


def _adaln(v8, ada_down, ada_up, ada_bias):
    depth, d, rank = ada_down.shape
    blocks = (_nbytes((MOD_ROWS, d), F32) + _nbytes((d, rank), F32) + _nbytes((rank, d), F32)
              + _nbytes((1, d), F32) + _nbytes((MOD_ROWS, d), F32))
    return pl.pallas_call(
        _adaln_kernel,
        grid=(depth, N_MOD),
        in_specs=[
            pl.BlockSpec((MOD_ROWS, d), lambda l, j: (0, 0)),
            pl.BlockSpec((None, d, rank), lambda l, j: (l, 0, 0)),
            pl.BlockSpec((None, rank, d), lambda l, j: (l, 0, j)),
            pl.BlockSpec((None, 1, d), lambda l, j: (l, 0, j)),
        ],
        out_specs=pl.BlockSpec((None, None, MOD_ROWS, d), lambda l, j: (l, j, 0, 0)),
        out_shape=jax.ShapeDtypeStruct((depth, N_MOD, MOD_ROWS, d), F32),
        compiler_params=pltpu.CompilerParams(
            dimension_semantics=("arbitrary", "arbitrary"), vmem_limit_bytes=_vmem_limit(blocks)),
        name="adaln",
    )(v8, ada_down, ada_up, ada_bias.reshape(depth, 1, -1))


def _mod_row_index(i, bm, seq):
    return jnp.minimum((i * bm) // seq, MOD_ROWS - 1)


def _mod_spec(bm, seq, d):
    return pl.BlockSpec((None, 1, d), lambda i, j: (_mod_row_index(i, bm, seq), 0, 0))


NORM_ROWS = 32


def _modulate_kernel(h_ref, g_ref, sh_ref, sc_ref, xn_ref, *, bm):
    gain = g_ref[...]
    scale1 = 1.0 + sc_ref[...]
    shift = sh_ref[...]

    def body(c, carry):
        r0 = pl.multiple_of(c * NORM_ROWS, NORM_ROWS)
        x = h_ref[pl.ds(r0, NORM_ROWS), :]
        y = x * lax.rsqrt(jnp.mean(x * x, axis=-1, keepdims=True) + EPS)
        xn_ref[pl.ds(r0, NORM_ROWS), :] = ((y * gain) * scale1 + shift).astype(BF16)
        return carry

    lax.fori_loop(0, bm // NORM_ROWS, body, 0)


def _modulate(h, rows, gain, shift, scale, *, seq, bm):
    d = h.shape[1]
    blocks = _nbytes((bm, d), F32) + _nbytes((bm, d), BF16) + 3 * _nbytes((1, d), F32)
    return pl.pallas_call(
        functools.partial(_modulate_kernel, bm=bm),
        grid=(rows // bm, 1),
        in_specs=[
            pl.BlockSpec((bm, d), lambda i, j: (i, 0)),
            pl.BlockSpec((1, d), lambda i, j: (0, 0)),
            _mod_spec(bm, seq, d),
            _mod_spec(bm, seq, d),
        ],
        out_specs=pl.BlockSpec((bm, d), lambda i, j: (i, 0)),
        out_shape=jax.ShapeDtypeStruct((rows, d), BF16),
        compiler_params=pltpu.CompilerParams(
            dimension_semantics=("arbitrary", "arbitrary"), vmem_limit_bytes=_vmem_limit(blocks)),
        name="modulate",
    )(h, gain, shift, scale)


def _swiglu_kernel(x_ref, wa_ref, wb_ref, o_ref):
    x = x_ref[...]
    a = jnp.dot(x, wa_ref[...], preferred_element_type=F32)
    b = jnp.dot(x, wb_ref[...], preferred_element_type=F32)
    o_ref[...] = (jax.nn.silu(a) * b).astype(o_ref.dtype)


def _swiglu(x, w_in, widx, rows, *, bm, bn):
    d = x.shape[1]
    f = w_in.shape[2] // 2
    nb = f // bn
    blocks = _nbytes((bm, d), BF16) + 2 * _nbytes((d, bn), BF16) + _nbytes((bm, bn), BF16)
    return pl.pallas_call(
        _swiglu_kernel,
        grid=(rows // bm, nb),
        in_specs=[
            pl.BlockSpec((bm, d), lambda i, j: (i, 0)),
            pl.BlockSpec((None, d, bn), lambda i, j: (widx, 0, j)),
            pl.BlockSpec((None, d, bn), lambda i, j: (widx, 0, j + nb)),
        ],
        out_specs=pl.BlockSpec((bm, bn), lambda i, j: (i, j)),
        out_shape=jax.ShapeDtypeStruct((rows, f), BF16),
        compiler_params=pltpu.CompilerParams(
            dimension_semantics=("arbitrary", "arbitrary"),
            vmem_limit_bytes=_vmem_limit(blocks, 0, 4 * _nbytes((bm, bn), F32))),
        name="swiglu",
    )(x, w_in, w_in)


def _mm_kernel(x_ref, w_ref, o_ref, *, sigmoid):
    y = jnp.dot(x_ref[...], w_ref[...], preferred_element_type=F32)
    if sigmoid:
        y = jax.nn.sigmoid(y)
    o_ref[...] = y.astype(o_ref.dtype)


def _matmul(x, w, widx, rows, col0, n, *, bm, bn, out_dtype, sigmoid=False):
    k = x.shape[1]
    jb0 = col0 // bn
    blocks = _nbytes((bm, k), BF16) + _nbytes((k, bn), BF16) + _nbytes((bm, bn), out_dtype)
    return pl.pallas_call(
        functools.partial(_mm_kernel, sigmoid=sigmoid),
        grid=(rows // bm, n // bn),
        in_specs=[
            pl.BlockSpec((bm, k), lambda i, j: (i, 0)),
            pl.BlockSpec((None, k, bn), lambda i, j: (widx, 0, jb0 + j)),
        ],
        out_specs=pl.BlockSpec((bm, bn), lambda i, j: (i, j)),
        out_shape=jax.ShapeDtypeStruct((rows, n), out_dtype),
        compiler_params=pltpu.CompilerParams(
            dimension_semantics=("arbitrary", "arbitrary"),
            vmem_limit_bytes=_vmem_limit(blocks, 0, 2 * _nbytes((bm, bn), F32))),
        name="matmul_sigmoid" if sigmoid else "matmul",
    )(x, w)


def _resid_mm_kernel(x_ref, w_ref, h_ref, gate_ref, o_ref, *, weight):
    y = jnp.dot(x_ref[...], w_ref[...], preferred_element_type=F32)
    if weight != 1.0:
        o_ref[...] = h_ref[...] + weight * gate_ref[...] * y
    else:
        o_ref[...] = h_ref[...] + gate_ref[...] * y


def _resid_matmul(x, w, widx, h, rows, gate, *, weight, seq, bm, bn):
    k = x.shape[1]
    n = w.shape[2]
    blocks = (_nbytes((bm, k), BF16) + _nbytes((k, bn), BF16) + 2 * _nbytes((bm, bn), F32)
              + _nbytes((1, bn), F32))
    return pl.pallas_call(
        functools.partial(_resid_mm_kernel, weight=weight),
        grid=(rows // bm, n // bn),
        in_specs=[
            pl.BlockSpec((bm, k), lambda i, j: (i, 0)),
            pl.BlockSpec((None, k, bn), lambda i, j: (widx, 0, j)),
            pl.BlockSpec((bm, bn), lambda i, j: (i, j)),
            pl.BlockSpec((None, 1, bn), lambda i, j: (_mod_row_index(i, bm, seq), 0, j)),
        ],
        out_specs=pl.BlockSpec((bm, bn), lambda i, j: (i, j)),
        out_shape=jax.ShapeDtypeStruct((rows, n), F32),
        compiler_params=pltpu.CompilerParams(
            dimension_semantics=("arbitrary", "arbitrary"),
            vmem_limit_bytes=_vmem_limit(blocks, 0, _nbytes((bm, bn), F32))),
        name="resid_matmul",
    )(x, w, h, gate)


def _merge_kernel(oa_ref, ob_ref, oc_ref, w_ref, ga_ref, gb_ref, gc_ref, y_ref):
    y = ga_ref[...].astype(F32) * jnp.dot(oa_ref[...], w_ref[0], preferred_element_type=F32)
    y = y + gb_ref[...].astype(F32) * jnp.dot(ob_ref[...], w_ref[1], preferred_element_type=F32)
    y = y + gc_ref[...].astype(F32) * jnp.dot(oc_ref[...], w_ref[2], preferred_element_type=F32)
    y_ref[...] = y.astype(y_ref.dtype)


def _merge(oa, ob, oc, w_branch, widx, gates, rows, *, bm, bn):
    kb = oa.shape[1]
    d = w_branch.shape[3]
    per_branch = d // bn
    blocks = (3 * _nbytes((bm, kb), BF16) + _nbytes((N_BRANCH, kb, bn), BF16) + 4 * _nbytes((bm, bn), BF16))
    o_spec = pl.BlockSpec((bm, kb), lambda i, j: (i, 0))

    def gate_spec(br):
        return pl.BlockSpec((bm, bn), lambda i, j: (i, br * per_branch + j))

    return pl.pallas_call(
        _merge_kernel,
        grid=(rows // bm, d // bn),
        in_specs=[o_spec, o_spec, o_spec,
                  pl.BlockSpec((None, N_BRANCH, kb, bn), lambda i, j: (widx, 0, 0, j)),
                  gate_spec(0), gate_spec(1), gate_spec(2)],
        out_specs=pl.BlockSpec((bm, bn), lambda i, j: (i, j)),
        out_shape=jax.ShapeDtypeStruct((rows, d), BF16),
        compiler_params=pltpu.CompilerParams(
            dimension_semantics=("arbitrary", "arbitrary"),
            vmem_limit_bytes=_vmem_limit(blocks, 0, 4 * _nbytes((bm, bn), F32))),
        name="merge",
    )(oa, ob, oc, w_branch, gates, gates, gates)


GLA_BLOCK = 256
GLA_UNROLL = 4
_NT = (((1,), (1,)), ((), ()))
_TN = (((0,), (0,)), ((), ()))


def _split_bf16(x):
    hi = x.astype(BF16)
    lo = (x - hi.astype(F32)).astype(BF16)
    return hi, lo


def _sum_dot(op, parts):
    acc = jnp.dot(op, parts[0], preferred_element_type=F32)
    for p in parts[1:]:
        acc = acc + jnp.dot(op, p, preferred_element_type=F32)
    return acc


def _gla_kernel(ql_ref, kl_ref, vl_ref, ogl_ref, gdl_ref, qc_ref, kc_ref, vc_ref, ogc_ref, gdc_ref,
                up_ref, gb_ref, gn_ref, al_ref, ac_ref,
                qe_ref, edec_ref, of_ref, ob_ref, kv_ref, st_ref, *, t_lat, t_ctx):
    c = GLA_CHUNK
    br = GLA_BLOCK
    cpb = br // c
    row = lax.broadcasted_iota(jnp.int32, (br, br), 0)
    col = lax.broadcasted_iota(jnp.int32, (br, br), 1)
    same = (row // c) == (col // c)
    keep = (same & (col <= row), same & (col >= row))
    tri = (keep[0].astype(BF16), keep[1].astype(BF16))
    blk = same.astype(BF16)
    q_scale = GLA_DK ** -0.5
    gain = gn_ref[...]
    o_refs = (of_ref, ob_ref)

    st_ref[...] = jnp.zeros_like(st_ref)

    def segment(q_ref, k_ref, v_ref, og_ref, gd_ref, out_ref, t):
        nc = t // c

        def block_body(r, carry):
            rows = pl.ds(pl.multiple_of(r * br, br), br)
            gd = gd_ref[rows, :]
            q = q_ref[rows, :] * q_scale
            k = k_ref[rows, :]
            v = v_ref[rows, :].astype(BF16)
            for d in range(2):
                z = jnp.dot(gd, up_ref[d], preferred_element_type=F32) + gb_ref[d]
                parts = _split_bf16(jax.nn.log_sigmoid(z) * (1.0 / GLA_GATE_NORM))
                cum = _sum_dot(tri[d], parts)
                cum_end = _sum_dot(blk, parts)
                qe = (q * jnp.exp(cum)).astype(BF16)
                kinv = (k * jnp.exp(-cum)).astype(BF16)
                kdec = (k * jnp.exp(cum_end - cum)).astype(BF16)
                qe_ref[d, rows, :] = qe
                edec_ref[d, rows, :] = jnp.exp(cum_end)
                a = lax.dot_general(qe, kinv, _NT, preferred_element_type=F32)
                a = jnp.where(keep[d], a, 0.0).astype(BF16)
                o_refs[d][rows, :] = jnp.dot(a, v, preferred_element_type=F32)
                for j in range(cpb):
                    sl = slice(j * c, (j + 1) * c)
                    kv_ref[d, r * cpb + j] = lax.dot_general(v[sl], kdec[sl], _TN, preferred_element_type=F32)
            return carry

        lax.fori_loop(0, t // br, block_body, 0)

        def scan_body(n, carry):
            for d in range(2):
                m = n if d == 0 else nc - 1 - n
                rows = pl.ds(pl.multiple_of(m * c, c), c)
                st = st_ref[d]
                o_refs[d][rows, :] += lax.dot_general(qe_ref[d, rows, :], st.astype(BF16), _NT,
                                                      preferred_element_type=F32)
                st_ref[d] = st * edec_ref[d, pl.ds(pl.multiple_of(m * c, c), 1), :] + kv_ref[d, m]
            return carry

        lax.fori_loop(0, nc, scan_body, 0, unroll=GLA_UNROLL)

        def out_body(n, carry):
            rows = pl.ds(pl.multiple_of(n * c, c), c)
            o = of_ref[rows, :] + ob_ref[rows, :]
            y = o * lax.rsqrt(jnp.mean(o * o, axis=-1, keepdims=True) + EPS)
            y = (y * gain) * jax.nn.silu(og_ref[rows, :])
            out_ref[rows, :] = y.astype(out_ref.dtype)
            return carry

        lax.fori_loop(0, nc, out_body, 0, unroll=GLA_UNROLL)

    segment(qc_ref, kc_ref, vc_ref, ogc_ref, gdc_ref, ac_ref, t_ctx)
    segment(ql_ref, kl_ref, vl_ref, ogl_ref, gdl_ref, al_ref, t_lat)


def _gla(pa, gate_up_pad, gate_bias, gla_norm, *, batch):
    t_lat, t_ctx = SEQ, CTX_LEN
    ctx0 = batch * t_lat // t_ctx
    dk, dv = GLA_DK, GLA_DV
    nc = t_lat // GLA_CHUNK

    def lat(width, col0):
        return pl.BlockSpec((t_lat, width), lambda b, h: (b, col0 // width + h))

    def ctx(width, col0):
        return pl.BlockSpec((t_ctx, width), lambda b, h: (ctx0 + b, col0 // width + h))

    lat_gd = pl.BlockSpec((t_lat, 128), lambda b, h: (b, A_GD // 128))
    ctx_gd = pl.BlockSpec((t_ctx, 128), lambda b, h: (ctx0 + b, A_GD // 128))
    blocks = ((t_lat + t_ctx) * (3 * dk + 2 * dv) * 4 + _nbytes((2, 128, dk), F32)
              + (t_lat + t_ctx) * dv * 2)
    scratch_shapes = [pltpu.VMEM((2, t_lat, dk), BF16), pltpu.VMEM((2, t_lat, dk), F32),
                      pltpu.VMEM((t_lat, dv), F32), pltpu.VMEM((t_lat, dv), F32),
                      pltpu.VMEM((2, nc, dv, dk), F32), pltpu.VMEM((2, dv, dk), F32)]
    scratch = (_nbytes((2, t_lat, dk), BF16) + _nbytes((2, t_lat, dk), F32) + 2 * _nbytes((t_lat, dv), F32)
               + _nbytes((2, nc + 1, dv, dk), F32))
    return pl.pallas_call(
        functools.partial(_gla_kernel, t_lat=t_lat, t_ctx=t_ctx),
        grid=(batch, GLA_HEADS),
        in_specs=[lat(dk, A_GQ), lat(dk, A_GK), lat(dv, A_GV), lat(dv, A_GO), lat_gd,
                  ctx(dk, A_GQ), ctx(dk, A_GK), ctx(dv, A_GV), ctx(dv, A_GO), ctx_gd,
                  pl.BlockSpec((2, 128, dk), lambda b, h: (0, 0, h)),
                  pl.BlockSpec((2, 1, dk), lambda b, h: (0, 0, h)),
                  pl.BlockSpec((1, dv), lambda b, h: (0, 0))],
        out_specs=[pl.BlockSpec((t_lat, dv), lambda b, h: (b, h)),
                   pl.BlockSpec((t_ctx, dv), lambda b, h: (b, h))],
        out_shape=[jax.ShapeDtypeStruct((batch * t_lat, GLA_HEADS * dv), BF16),
                   jax.ShapeDtypeStruct((batch * t_ctx, GLA_HEADS * dv), BF16)],
        scratch_shapes=scratch_shapes,
        compiler_params=pltpu.CompilerParams(
            dimension_semantics=("arbitrary", "arbitrary"),
            vmem_limit_bytes=_vmem_limit(blocks, scratch, 6 << 20)),
        name="gla",
    )(pa, pa, pa, pa, pa, pa, pa, pa, pa, pa, gate_up_pad, gate_bias, gla_norm)


def _rms(x, gain):
    return (x * lax.rsqrt(jnp.mean(x * x, axis=-1, keepdims=True) + EPS)) * gain


def _rope(x, cos, sin_signed):
    lane = lax.broadcasted_iota(jnp.int32, x.shape, 1)
    partner = jnp.where((lane % 64) < 32, pltpu.roll(x, 96, 1), pltpu.roll(x, 32, 1))
    return x * cos + partner * sin_signed


def _q_row_block(b, qb, *, batch, n_lat, n_ctx):
    return jnp.where(qb < n_lat, b * n_lat + qb, batch * n_lat + b * n_ctx + (qb - n_lat))


KV_PREP_ROWS = 256


def _swa_kernel(sink_ref, q_ref, kl_ref, vl_ref, kc_ref, vc_ref, cq_ref, sq_ref, ck_ref, sk_ref,
                qg_ref, kg_ref, o_ref, kn_ref, vb_ref, kcn_ref, vcb_ref, *, t_lat, n_lat):
    qb = pl.program_id(1)
    g = SWA_GROUP
    dh = HEAD_DIM
    scale = dh ** -0.5

    @pl.when(qb == 0)
    def _prep():
        k_gain = kg_ref[...]

        def body(r, carry):
            r0 = pl.multiple_of(r * KV_PREP_ROWS, KV_PREP_ROWS)
            rows = pl.ds(r0, KV_PREP_ROWS)
            cos, sin = ck_ref[rows, :], sk_ref[rows, :]
            for kv in range(SWA_KV_HEADS):
                cols = slice(kv * dh, (kv + 1) * dh)
                kn_ref[rows, cols] = _rope(_rms(kl_ref[rows, cols], k_gain), cos, sin).astype(BF16)
            vb_ref[rows, :] = vl_ref[rows, :].astype(BF16)
            return carry

        lax.fori_loop(0, t_lat // KV_PREP_ROWS, body, 0)
        for kv in range(SWA_KV_HEADS):
            cols = slice(kv * dh, (kv + 1) * dh)
            kcn_ref[:, cols] = _rms(kc_ref[:, cols], k_gain).astype(BF16)
        vcb_ref[...] = vc_ref[...].astype(BF16)

    def attend(latent):
        cos, sin = cq_ref[...], sq_ref[...]
        q_gain = qg_ref[...]
        head_of_row = lax.broadcasted_iota(jnp.int32, (g * QBLK, 1), 0) // QBLK
        for kv in range(SWA_KV_HEADS):
            cols = slice(kv * dh, (kv + 1) * dh)
            heads = [_rope(_rms(q_ref[:, (kv * g + i) * dh:(kv * g + i + 1) * dh], q_gain), cos, sin).astype(BF16)
                     for i in range(g)]
            qs = jnp.concatenate(heads, axis=0)
            sink = jnp.full((g * QBLK, 1), sink_ref[kv * g], F32)
            for i in range(1, g):
                sink = jnp.where(head_of_row == i, sink_ref[kv * g + i], sink)
            s_ctx = lax.dot_general(qs, kcn_ref[:, cols], _NT, preferred_element_type=F32) * scale
            m = jnp.maximum(jnp.max(s_ctx, axis=-1, keepdims=True), sink)
            if latent:
                span = 3 * QBLK
                start = pl.multiple_of(jnp.clip((qb - 1) * QBLK, 0, t_lat - span), QBLK)
                s_loc = lax.dot_general(qs, kn_ref[pl.ds(start, span), cols], _NT,
                                        preferred_element_type=F32) * scale
                qpos = qb * QBLK + lax.broadcasted_iota(jnp.int32, (g * QBLK, span), 0) % QBLK
                kpos = start + lax.broadcasted_iota(jnp.int32, (g * QBLK, span), 1)
                s_loc = jnp.where(jnp.abs(kpos - qpos) <= SWA_WINDOW, s_loc, NEG_INF)
                m = jnp.maximum(m, jnp.max(s_loc, axis=-1, keepdims=True))
                p_loc = jnp.exp(s_loc - m)
                den = jnp.sum(p_loc, axis=-1, keepdims=True)
                o = jnp.dot(p_loc.astype(BF16), vb_ref[pl.ds(start, span), cols], preferred_element_type=F32)
            p_ctx = jnp.exp(s_ctx - m)
            o_ctx = jnp.dot(p_ctx.astype(BF16), vcb_ref[:, cols], preferred_element_type=F32)
            den_ctx = jnp.sum(p_ctx, axis=-1, keepdims=True) + jnp.exp(sink - m)
            o, den = (o + o_ctx, den + den_ctx) if latent else (o_ctx, den_ctx)
            o = o / den
            for i in range(g):
                o_ref[:, (kv * g + i) * dh:(kv * g + i + 1) * dh] = o[i * QBLK:(i + 1) * QBLK, :].astype(o_ref.dtype)

    pl.when(qb < n_lat)(functools.partial(attend, True))
    pl.when(qb >= n_lat)(functools.partial(attend, False))


def _swa(pc, sink, cos_tab, sin_tab, q_gain, k_gain, *, batch, with_ctx_out):
    t_lat, t_ctx, dh = SEQ, CTX_LEN, HEAD_DIM
    qw, kw = SWA_HEADS * dh, SWA_KV_HEADS * dh
    n_lat, n_ctx = t_lat // QBLK, t_ctx // QBLK
    n_q = n_lat + (n_ctx if with_ctx_out else 0)
    rows_out = batch * (t_lat + (t_ctx if with_ctx_out else 0))
    ctx0 = batch * t_lat // t_ctx
    qmap = functools.partial(_q_row_block, batch=batch, n_lat=n_lat, n_ctx=n_ctx)
    blocks = (_nbytes((QBLK, qw), F32) + 2 * _nbytes((t_lat + t_ctx, kw), F32) + 2 * _nbytes((QBLK, dh), F32)
              + 2 * _nbytes((t_lat, dh), F32) + _nbytes((QBLK, qw), BF16))
    scratch = 2 * _nbytes((t_lat + t_ctx, kw), BF16)
    return pl.pallas_call(
        functools.partial(_swa_kernel, t_lat=t_lat, n_lat=n_lat),
        grid=(batch, n_q),
        in_specs=[
            pl.BlockSpec(memory_space=pltpu.SMEM),
            pl.BlockSpec((QBLK, qw), lambda b, qb: (qmap(b, qb), C_SQ // qw)),
            pl.BlockSpec((t_lat, kw), lambda b, qb: (b, C_SK // kw)),
            pl.BlockSpec((t_lat, kw), lambda b, qb: (b, C_SV // kw)),
            pl.BlockSpec((t_ctx, kw), lambda b, qb: (ctx0 + b, C_SK // kw)),
            pl.BlockSpec((t_ctx, kw), lambda b, qb: (ctx0 + b, C_SV // kw)),
            pl.BlockSpec((QBLK, dh), lambda b, qb: (qb, 0)),
            pl.BlockSpec((QBLK, dh), lambda b, qb: (qb, 0)),
            pl.BlockSpec((t_lat, dh), lambda b, qb: (0, 0)),
            pl.BlockSpec((t_lat, dh), lambda b, qb: (0, 0)),
            pl.BlockSpec((1, dh), lambda b, qb: (0, 0)),
            pl.BlockSpec((1, dh), lambda b, qb: (0, 0)),
        ],
        out_specs=pl.BlockSpec((QBLK, qw), lambda b, qb: (qmap(b, qb), 0)),
        out_shape=jax.ShapeDtypeStruct((rows_out, qw), BF16),
        scratch_shapes=[pltpu.VMEM((t_lat, kw), BF16), pltpu.VMEM((t_lat, kw), BF16),
                        pltpu.VMEM((t_ctx, kw), BF16), pltpu.VMEM((t_ctx, kw), BF16)],
        compiler_params=pltpu.CompilerParams(
            dimension_semantics=("arbitrary", "arbitrary"),
            vmem_limit_bytes=_vmem_limit(blocks, scratch, 12 << 20)),
        name="swa",
    )(sink, pc, pc, pc, pc, pc, cos_tab, sin_tab, cos_tab, sin_tab, q_gain, k_gain)


NA_HEADS_PER_STEP = 4


def _na_slab_start(qb):
    r = qb * (QBLK // GRID_W)
    rows = SEQ // GRID_W
    return jnp.minimum(jnp.clip(r - NA_KH // 2, 0, rows - NA_KH), rows - NA_SLAB_ROWS)


def _na_bias_variant(qb, n_lat):
    return jnp.where(qb < 2, qb, jnp.where(qb < n_lat - 2, 2, jnp.minimum(qb, n_lat - 1) - (n_lat - 5)))


def _na_kernel(q_ref, kl_ref, vl_ref, kc_ref, vc_ref, bias_ref, qg_ref, kg_ref, o_ref,
               kn_ref, vb_ref, kcn_ref, vcb_ref, *, t_lat, n_lat):
    qb = pl.program_id(2)
    dh = HEAD_DIM
    scale = dh ** -0.5

    @pl.when(qb == 0)
    def _prep():
        k_gain = kg_ref[...]

        def body(r, carry):
            r0 = pl.multiple_of(r * KV_PREP_ROWS, KV_PREP_ROWS)
            rows = pl.ds(r0, KV_PREP_ROWS)
            for h in range(NA_HEADS_PER_STEP):
                cols = slice(h * dh, (h + 1) * dh)
                kn_ref[rows, cols] = _rms(kl_ref[rows, cols], k_gain).astype(BF16)
            vb_ref[rows, :] = vl_ref[rows, :].astype(BF16)
            return carry

        lax.fori_loop(0, t_lat // KV_PREP_ROWS, body, 0)
        for h in range(NA_HEADS_PER_STEP):
            cols = slice(h * dh, (h + 1) * dh)
            kcn_ref[:, cols] = _rms(kc_ref[:, cols], k_gain).astype(BF16)
        vcb_ref[...] = vc_ref[...].astype(BF16)

    def attend(latent):
        q_gain = qg_ref[...]
        for h in range(NA_HEADS_PER_STEP):
            cols = slice(h * dh, (h + 1) * dh)
            qn = _rms(q_ref[:, cols], q_gain).astype(BF16)
            s_ctx = lax.dot_general(qn, kcn_ref[:, cols], _NT, preferred_element_type=F32) * scale
            m = jnp.max(s_ctx, axis=-1, keepdims=True)
            if latent:
                span = NA_SLAB_ROWS * GRID_W
                start = pl.multiple_of(_na_slab_start(qb) * GRID_W, GRID_W)
                s_loc = lax.dot_general(qn, kn_ref[pl.ds(start, span), cols], _NT,
                                        preferred_element_type=F32) * scale
                s_loc = s_loc + bias_ref[h]
                m = jnp.maximum(m, jnp.max(s_loc, axis=-1, keepdims=True))
                p_loc = jnp.exp(s_loc - m)
                den = jnp.sum(p_loc, axis=-1, keepdims=True)
                o = jnp.dot(p_loc.astype(BF16), vb_ref[pl.ds(start, span), cols], preferred_element_type=F32)
            p_ctx = jnp.exp(s_ctx - m)
            o_ctx = jnp.dot(p_ctx.astype(BF16), vcb_ref[:, cols], preferred_element_type=F32)
            den_ctx = jnp.sum(p_ctx, axis=-1, keepdims=True)
            o, den = (o + o_ctx, den + den_ctx) if latent else (o_ctx, den_ctx)
            o_ref[:, cols] = (o / den).astype(o_ref.dtype)

    pl.when(qb < n_lat)(functools.partial(attend, True))
    pl.when(qb >= n_lat)(functools.partial(attend, False))


def _na(pc, bias_tab, q_gain, k_gain, *, batch, with_ctx_out):
    t_lat, t_ctx, dh = SEQ, CTX_LEN, HEAD_DIM
    hb = NA_HEADS_PER_STEP
    w = hb * dh
    n_lat, n_ctx = t_lat // QBLK, t_ctx // QBLK
    n_q = n_lat + (n_ctx if with_ctx_out else 0)
    rows_out = batch * (t_lat + (t_ctx if with_ctx_out else 0))
    ctx0 = batch * t_lat // t_ctx
    span = NA_SLAB_ROWS * GRID_W
    qmap = functools.partial(_q_row_block, batch=batch, n_lat=n_lat, n_ctx=n_ctx)
    blocks = (_nbytes((QBLK, w), F32) + 2 * _nbytes((t_lat + t_ctx, w), F32) + _nbytes((hb, QBLK, span), F32)
              + _nbytes((QBLK, w), BF16))
    scratch = 2 * _nbytes((t_lat + t_ctx, w), BF16)
    return pl.pallas_call(
        functools.partial(_na_kernel, t_lat=t_lat, n_lat=n_lat),
        grid=(batch, NA_HEADS // hb, n_q),
        in_specs=[
            pl.BlockSpec((QBLK, w), lambda b, hg, qb: (qmap(b, qb), C_NQ // w + hg)),
            pl.BlockSpec((t_lat, w), lambda b, hg, qb: (b, C_NK // w + hg)),
            pl.BlockSpec((t_lat, w), lambda b, hg, qb: (b, C_NV // w + hg)),
            pl.BlockSpec((t_ctx, w), lambda b, hg, qb: (ctx0 + b, C_NK // w + hg)),
            pl.BlockSpec((t_ctx, w), lambda b, hg, qb: (ctx0 + b, C_NV // w + hg)),
            pl.BlockSpec((hb, None, QBLK, span), lambda b, hg, qb: (hg, _na_bias_variant(qb, n_lat), 0, 0)),
            pl.BlockSpec((1, dh), lambda b, hg, qb: (0, 0)),
            pl.BlockSpec((1, dh), lambda b, hg, qb: (0, 0)),
        ],
        out_specs=pl.BlockSpec((QBLK, w), lambda b, hg, qb: (qmap(b, qb), hg)),
        out_shape=jax.ShapeDtypeStruct((rows_out, NA_HEADS * dh), BF16),
        scratch_shapes=[pltpu.VMEM((t_lat, w), BF16), pltpu.VMEM((t_lat, w), BF16),
                        pltpu.VMEM((t_ctx, w), BF16), pltpu.VMEM((t_ctx, w), BF16)],
        compiler_params=pltpu.CompilerParams(
            dimension_semantics=("arbitrary", "arbitrary", "arbitrary"),
            vmem_limit_bytes=_vmem_limit(blocks, scratch, 12 << 20)),
        name="natten",
    )(pc, pc, pc, pc, pc, bias_tab, q_gain, k_gain)


def _rope_tables():
    quarter = HEAD_DIM // 4
    pos = jnp.arange(SEQ)
    rows = (pos // GRID_W).astype(F32)
    cols = (pos % GRID_W).astype(F32)
    inv = ROPE_BASE ** (-jnp.arange(quarter, dtype=F32) / quarter)
    ang_r = rows[:, None] * inv[None, :]
    ang_c = cols[:, None] * inv[None, :]
    cos = jnp.concatenate([jnp.cos(ang_r), jnp.cos(ang_r), jnp.cos(ang_c), jnp.cos(ang_c)], axis=-1)
    sin = jnp.concatenate([-jnp.sin(ang_r), jnp.sin(ang_r), -jnp.sin(ang_c), jnp.sin(ang_c)], axis=-1)
    cos = jnp.concatenate([cos, jnp.ones((CTX_LEN, HEAD_DIM), F32)], axis=0)
    sin = jnp.concatenate([sin, jnp.zeros((CTX_LEN, HEAD_DIM), F32)], axis=0)
    return cos, sin


def _na_bias_tables(rpb):
    n_layers, n_heads = rpb.shape[:2]
    rows = SEQ // GRID_W
    n_lat = SEQ // QBLK
    rq_per = QBLK // GRID_W
    reps = np.array([0, 1, 2, n_lat - 2, n_lat - 1])
    r = reps * rq_per
    start = np.minimum(np.clip(r - NA_KH // 2, 0, rows - NA_KH), rows - NA_SLAB_ROWS)
    rq = r[:, None] + np.arange(rq_per)[None, :]
    kr = start[:, None] + np.arange(NA_SLAB_ROWS)[None, :]
    r0 = np.clip(rq - NA_KH // 2, 0, rows - NA_KH)
    valid_r = (kr[:, None, :] >= r0[:, :, None]) & (kr[:, None, :] < r0[:, :, None] + NA_KH)
    dr = np.clip(kr[:, None, :] - rq[:, :, None] + NA_KH - 1, 0, 2 * NA_KH - 2)
    qc = np.arange(GRID_W)
    kc = np.arange(GRID_W)
    cs = np.clip(qc - NA_KW // 2, 0, GRID_W - NA_KW)
    valid_c = (kc[None, :] >= cs[:, None]) & (kc[None, :] < cs[:, None] + NA_KW)
    dc = np.clip(kc[None, :] - qc[:, None] + NA_KW - 1, 0, 2 * NA_KW - 2)
    pick_r = np.eye(2 * NA_KH - 1, dtype=np.float32)[dr.reshape(-1)]
    pick_c = np.eye(2 * NA_KW - 1, dtype=np.float32)[dc.reshape(-1)]
    t = jnp.einsum("nr,lhrc->lhnc", pick_r, rpb.astype(F32), precision=HIGHEST)
    t = jnp.einsum("lhnc,xc->lhnx", t, pick_c, precision=HIGHEST)
    t = t.reshape(n_layers, n_heads, len(reps), rq_per, NA_SLAB_ROWS, GRID_W, GRID_W)
    t = t.transpose(0, 1, 2, 3, 5, 4, 6)
    valid = valid_r[:, :, None, :, None] & valid_c[None, None, :, None, :]
    t = jnp.where(valid[None, None], t, NEG_INF)
    return t.reshape(n_layers, n_heads, len(reps), QBLK, NA_SLAB_ROWS * GRID_W)


LANES = 128
CAST_ROWS = 2048
CAST_CHUNK = 256


def _cast_cols_kernel(main_ref, next_ref, o_ref, *, shift):
    nblk = o_ref.shape[1] // LANES
    lane = lax.broadcasted_iota(jnp.int32, (CAST_CHUNK, LANES), 1)

    def body(c, carry):
        rows = pl.ds(pl.multiple_of(c * CAST_CHUNK, CAST_CHUNK), CAST_CHUNK)
        if shift == 0:
            o_ref[rows, :] = main_ref[rows, :].astype(o_ref.dtype)
            return carry
        for b in range(nblk):
            cur = main_ref[rows, b * LANES:(b + 1) * LANES]
            nxt = next_ref[rows, :] if b == nblk - 1 else main_ref[rows, (b + 1) * LANES:(b + 2) * LANES]
            blk = jnp.where(lane < LANES - shift, pltpu.roll(cur, LANES - shift, 1),
                            pltpu.roll(nxt, LANES - shift, 1))
            o_ref[rows, b * LANES:(b + 1) * LANES] = blk.astype(o_ref.dtype)
        return carry

    lax.fori_loop(0, o_ref.shape[0] // CAST_CHUNK, body, 0)


def _cast_cols(w, col0, n, *, bn):
    n_layers, k, _ = w.shape
    shift = col0 % LANES
    base = col0 - shift
    assert base % bn == 0 and n % bn == 0 and bn % LANES == 0 and k % CAST_ROWS == 0
    blocks = _nbytes((CAST_ROWS, bn + LANES), F32) + _nbytes((CAST_ROWS, bn), BF16)
    return pl.pallas_call(
        functools.partial(_cast_cols_kernel, shift=shift),
        grid=(n_layers, k // CAST_ROWS, n // bn),
        in_specs=[
            pl.BlockSpec((None, CAST_ROWS, bn), lambda l, i, j: (l, i, base // bn + j)),
            pl.BlockSpec((None, CAST_ROWS, LANES), lambda l, i, j: (l, i, (base + (j + 1) * bn) // LANES)),
        ],
        out_specs=pl.BlockSpec((None, CAST_ROWS, bn), lambda l, i, j: (l, i, j)),
        out_shape=jax.ShapeDtypeStruct((n_layers, k, n), BF16),
        compiler_params=pltpu.CompilerParams(
            dimension_semantics=("arbitrary", "arbitrary", "arbitrary"),
            vmem_limit_bytes=_vmem_limit(blocks, 0, 4 << 20)),
        name="cast_cols",
    )(w, w)


def _pad_gate_up(gate_up):
    r = gate_up.shape[1]
    out = jnp.zeros((2, 128, gate_up.shape[2]), F32)
    out = out.at[0, 0:r].set(gate_up[0])
    out = out.at[1, r:2 * r].set(gate_up[1])
    return out


BM = 1024
BM_SMALL = 256
BN = 512
BN_FF = 512
BN_A = 640
BN_C = 768
BN_MERGE = 512


def kernel(x, c, ctx, c_ctx, ada_down, ada_up, ada_bias, norm_gain, ffn_w_in, ffn_w_out, w_in,
           gla_gate_up, gla_gate_bias, gla_norm, swa_q_norm, swa_k_norm, swa_sink, na_q_norm,
           na_k_norm, na_rpb, w_branch, w_out):
    batch, seq, d = x.shape
    depth = ada_down.shape[0]
    assert (seq, d, ctx.shape[1]) == (SEQ, D_MODEL, CTX_LEN) and batch + 1 <= MOD_ROWS
    assert w_in.shape[2] == C_START + C_QKV_COLS + GATE_COLS
    lat_rows = batch * seq
    all_rows = lat_rows + batch * ctx.shape[1]

    v8 = jnp.concatenate([c, c_ctx[None], jnp.zeros((MOD_ROWS - batch - 1, d), F32)], axis=0)
    mods = _adaln(v8, ada_down, ada_up, ada_bias)
    mods = mods.reshape(depth, N_MOD, MOD_ROWS, 1, d)
    cos_tab, sin_tab = _rope_tables()
    na_bias = _na_bias_tables(na_rpb)

    ffn_in_bf = ffn_w_in.astype(BF16).reshape(depth * 2, d, -1)
    ffn_out_bf = ffn_w_out.astype(BF16).reshape(depth * 2, -1, d)
    w_a_bf = _cast_cols(w_in, 0, A_COLS, bn=BN_A)
    w_c_bf = _cast_cols(w_in, C_START, C_QKV_COLS + GATE_COLS, bn=BN_C)
    w_branch_bf = w_branch.astype(BF16)
    w_out_bf = w_out.astype(BF16)

    h = jnp.concatenate([x.reshape(lat_rows, d), ctx.reshape(-1, d)], axis=0)
    bm = BM if all_rows % BM == 0 and lat_rows % BM == 0 else BM_SMALL
    mm = dict(seq=seq, bm=bm)
    norm = dict(seq=seq, bm=BM_SMALL)

    for l in range(depth):
        last = l == depth - 1
        rows_out = lat_rows if last else all_rows
        gain = norm_gain[l].reshape(3, 1, d)
        m = mods[l]

        xn = _modulate(h, all_rows, gain[0], m[0], m[1], **norm)
        g1 = _swiglu(xn, ffn_in_bf, 2 * l, all_rows, bm=bm, bn=BN_FF)
        h = _resid_matmul(g1, ffn_out_bf, 2 * l, h, all_rows, m[2], weight=MACARON_W, bn=BN, **mm)

        xn = _modulate(h, all_rows, gain[1], m[3], m[4], **norm)
        pa = _matmul(xn, w_a_bf, l, all_rows, 0, A_COLS, bm=bm, bn=BN_A, out_dtype=F32)
        pc = _matmul(xn, w_c_bf, l, all_rows, 0, C_QKV_COLS, bm=bm, bn=BN_C, out_dtype=F32)
        gates = _matmul(xn, w_c_bf, l, rows_out, C_QKV_COLS, GATE_COLS, bm=bm, bn=BN_C,
                        out_dtype=BF16, sigmoid=True)
        a_lat, a_ctx = _gla(pa, _pad_gate_up(gla_gate_up[l]), gla_gate_bias[l].reshape(2, 1, -1),
                            gla_norm[l].reshape(1, -1), batch=batch)
        o_a = a_lat if last else jnp.concatenate([a_lat, a_ctx], axis=0)
        o_b = _swa(pc, swa_sink[l], cos_tab, sin_tab, swa_q_norm[l].reshape(1, -1), swa_k_norm[l].reshape(1, -1),
                   batch=batch, with_ctx_out=not last)
        o_c = _na(pc, na_bias[l], na_q_norm[l].reshape(1, -1), na_k_norm[l].reshape(1, -1),
                  batch=batch, with_ctx_out=not last)
        y = _merge(o_a, o_b, o_c, w_branch_bf, l, gates, rows_out, bm=bm, bn=BN_MERGE)
        h = _resid_matmul(y, w_out_bf, l, h, rows_out, m[5], weight=1.0, bn=BN, **mm)

        xn = _modulate(h, rows_out, gain[2], m[6], m[7], **norm)
        g2 = _swiglu(xn, ffn_in_bf, 2 * l + 1, rows_out, bm=bm, bn=BN_FF)
        h = _resid_matmul(g2, ffn_out_bf, 2 * l + 1, h, rows_out, m[8], weight=MACARON_W, bn=BN, **mm)

    return h.reshape(batch, seq, d)
```

```python
import functools

import jax
import jax.numpy as jnp
import numpy as np
from jax import lax
from jax.experimental import pallas as pl
from jax.experimental.pallas import tpu as pltpu

F32 = jnp.float32
BF16 = jnp.bfloat16
HIGHEST = lax.Precision.HIGHEST

D_MODEL = 4096
SEQ = 2048
CTX_LEN = 256
GRID_W = 64
HEAD_DIM = 128
GLA_HEADS = 4
GLA_DK = 128
GLA_DV = 256
GLA_GATE_RANK = 16
GLA_GATE_NORM = 16.0
GLA_CHUNK = 64
SWA_HEADS = 8
SWA_KV_HEADS = 2
SWA_GROUP = SWA_HEADS // SWA_KV_HEADS
SWA_WINDOW = 128
NA_HEADS = 8
NA_KH = 8
NA_KW = 16
N_BRANCH = 3
BRANCH_W = 1024
D_FF = 4096
MACARON_W = 0.5
N_MOD = 9
ROPE_BASE = 10000.0
EPS = 1e-6
NEG_INF = -1e30
MOD_ROWS = 8

A_GQ, A_GK, A_GV, A_GO, A_GD = 0, 512, 1024, 2048, 3072
A_COLS = 3200
C_START = 3104
C_SQ, C_SK, C_SV = 0, 1024, 1280
C_NQ, C_NK, C_NV = 1536, 2560, 3584
C_QKV_COLS = 4608
GATE_COLS = N_BRANCH * D_MODEL

V7X_VMEM_BYTES = 64 * 1024 * 1024
V7X_VMEM_RESERVE = 6 * 1024 * 1024
QBLK = 128
NA_SLAB_ROWS = 10


def _vmem_limit(pipelined_bytes, scratch_bytes=0, temp_bytes=0):
    need = 2 * pipelined_bytes + scratch_bytes + temp_bytes + (4 << 20)
    return int(min(max(need, 16 << 20), V7X_VMEM_BYTES - V7X_VMEM_RESERVE))


def _nbytes(shape, dtype):
    return int(np.prod(shape)) * jnp.dtype(dtype).itemsize


def _adaln_kernel(v_ref, down_ref, up_ref, bias_ref, o_ref, t_ref):
    @pl.when(pl.program_id(1) == 0)
    def _():
        t_ref[...] = jnp.dot(jax.nn.silu(v_ref[...]), down_ref[...], preferred_element_type=F32,
                             precision=HIGHEST)

    o_ref[...] = jnp.dot(t_ref[...], up_ref[...], preferred_element_type=F32, precision=HIGHEST) + bias_ref[...]


def _adaln(v8, ada_down, ada_up, ada_bias):
    depth, d, rank = ada_down.shape
    blocks = (_nbytes((MOD_ROWS, d), F32) + _nbytes((d, rank), F32) + _nbytes((rank, d), F32)
              + _nbytes((1, d), F32) + _nbytes((MOD_ROWS, d), F32))
    return pl.pallas_call(
        _adaln_kernel,
        grid=(depth, N_MOD),
        in_specs=[
            pl.BlockSpec((MOD_ROWS, d), lambda l, j: (0, 0)),
            pl.BlockSpec((None, d, rank), lambda l, j: (l, 0, 0)),
            pl.BlockSpec((None, rank, d), lambda l, j: (l, 0, j)),
            pl.BlockSpec((None, 1, d), lambda l, j: (l, 0, j)),
        ],
        out_specs=pl.BlockSpec((None, None, MOD_ROWS, d), lambda l, j: (l, j, 0, 0)),
        out_shape=jax.ShapeDtypeStruct((depth, N_MOD, MOD_ROWS, d), F32),
        scratch_shapes=[pltpu.VMEM((MOD_ROWS, rank), F32)],
        compiler_params=pltpu.CompilerParams(
            dimension_semantics=("arbitrary", "arbitrary"), vmem_limit_bytes=_vmem_limit(blocks)),
        name="adaln",
    )(v8, ada_down, ada_up, ada_bias.reshape(depth, 1, -1))


def _mod_row_index(i, bm, seq):
    return jnp.minimum((i * bm) // seq, MOD_ROWS - 1)


def _mod_spec(bm, seq, d):
    return pl.BlockSpec((None, 1, d), lambda i, j: (_mod_row_index(i, bm, seq), 0, 0))


NORM_ROWS = 32


def _modulate_kernel(h_ref, g_ref, sh_ref, sc_ref, xn_ref, *, bm):
    gain = g_ref[...]
    scale1 = 1.0 + sc_ref[...]
    shift = sh_ref[...]

    def body(c, carry):
        r0 = pl.multiple_of(c * NORM_ROWS, NORM_ROWS)
        x = h_ref[pl.ds(r0, NORM_ROWS), :]
        y = x * lax.rsqrt(jnp.mean(x * x, axis=-1, keepdims=True) + EPS)
        xn_ref[pl.ds(r0, NORM_ROWS), :] = ((y * gain) * scale1 + shift).astype(BF16)
        return carry

    lax.fori_loop(0, bm // NORM_ROWS, body, 0)


def _modulate(h, rows, gain, shift, scale, *, seq, bm):
    d = h.shape[1]
    blocks = _nbytes((bm, d), F32) + _nbytes((bm, d), BF16) + 3 * _nbytes((1, d), F32)
    return pl.pallas_call(
        functools.partial(_modulate_kernel, bm=bm),
        grid=(rows // bm, 1),
        in_specs=[
            pl.BlockSpec((bm, d), lambda i, j: (i, 0)),
            pl.BlockSpec((1, d), lambda i, j: (0, 0)),
            _mod_spec(bm, seq, d),
            _mod_spec(bm, seq, d),
        ],
        out_specs=pl.BlockSpec((bm, d), lambda i, j: (i, 0)),
        out_shape=jax.ShapeDtypeStruct((rows, d), BF16),
        compiler_params=pltpu.CompilerParams(
            dimension_semantics=("arbitrary", "arbitrary"), vmem_limit_bytes=_vmem_limit(blocks)),
        name="modulate",
    )(h, gain, shift, scale)


def _swiglu_kernel(x_ref, wa_ref, wb_ref, o_ref):
    x = x_ref[...]
    a = jnp.dot(x, wa_ref[...], preferred_element_type=F32)
    b = jnp.dot(x, wb_ref[...], preferred_element_type=F32)
    o_ref[...] = (jax.nn.silu(a) * b).astype(o_ref.dtype)


def _swiglu(x, w_in, widx, rows, *, bm, bn):
    d = x.shape[1]
    f = w_in.shape[2] // 2
    nb = f // bn
    blocks = _nbytes((bm, d), BF16) + 2 * _nbytes((d, bn), BF16) + _nbytes((bm, bn), BF16)
    return pl.pallas_call(
        _swiglu_kernel,
        grid=(rows // bm, nb),
        in_specs=[
            pl.BlockSpec((bm, d), lambda i, j: (i, 0)),
            pl.BlockSpec((None, d, bn), lambda i, j: (widx, 0, j)),
            pl.BlockSpec((None, d, bn), lambda i, j: (widx, 0, j + nb)),
        ],
        out_specs=pl.BlockSpec((bm, bn), lambda i, j: (i, j)),
        out_shape=jax.ShapeDtypeStruct((rows, f), BF16),
        compiler_params=pltpu.CompilerParams(
            dimension_semantics=("arbitrary", "arbitrary"),
            vmem_limit_bytes=_vmem_limit(blocks, 0, 4 * _nbytes((bm, bn), F32))),
        name="swiglu",
    )(x, w_in, w_in)


def _mm_kernel(x_ref, w_ref, o_ref, *, sigmoid):
    y = jnp.dot(x_ref[...], w_ref[...], preferred_element_type=F32)
    if sigmoid:
        y = jax.nn.sigmoid(y)
    o_ref[...] = y.astype(o_ref.dtype)


def _matmul(x, w, widx, rows, col0, n, *, bm, bn, out_dtype, sigmoid=False):
    k = x.shape[1]
    jb0 = col0 // bn
    blocks = _nbytes((bm, k), BF16) + _nbytes((k, bn), BF16) + _nbytes((bm, bn), out_dtype)
    return pl.pallas_call(
        functools.partial(_mm_kernel, sigmoid=sigmoid),
        grid=(rows // bm, n // bn),
        in_specs=[
            pl.BlockSpec((bm, k), lambda i, j: (i, 0)),
            pl.BlockSpec((None, k, bn), lambda i, j: (widx, 0, jb0 + j)),
        ],
        out_specs=pl.BlockSpec((bm, bn), lambda i, j: (i, j)),
        out_shape=jax.ShapeDtypeStruct((rows, n), out_dtype),
        compiler_params=pltpu.CompilerParams(
            dimension_semantics=("arbitrary", "arbitrary"),
            vmem_limit_bytes=_vmem_limit(blocks, 0, 2 * _nbytes((bm, bn), F32))),
        name="matmul_sigmoid" if sigmoid else "matmul",
    )(x, w)


def _resid_mm_kernel(x_ref, w_ref, h_ref, gate_ref, o_ref, *, weight):
    y = jnp.dot(x_ref[...], w_ref[...], preferred_element_type=F32)
    if weight != 1.0:
        o_ref[...] = h_ref[...] + weight * gate_ref[...] * y
    else:
        o_ref[...] = h_ref[...] + gate_ref[...] * y


def _resid_matmul(x, w, widx, h, rows, gate, *, weight, seq, bm, bn):
    k = x.shape[1]
    n = w.shape[2]
    blocks = (_nbytes((bm, k), BF16) + _nbytes((k, bn), BF16) + 2 * _nbytes((bm, bn), F32)
              + _nbytes((1, bn), F32))
    return pl.pallas_call(
        functools.partial(_resid_mm_kernel, weight=weight),
        grid=(rows // bm, n // bn),
        in_specs=[
            pl.BlockSpec((bm, k), lambda i, j: (i, 0)),
            pl.BlockSpec((None, k, bn), lambda i, j: (widx, 0, j)),
            pl.BlockSpec((bm, bn), lambda i, j: (i, j)),
            pl.BlockSpec((None, 1, bn), lambda i, j: (_mod_row_index(i, bm, seq), 0, j)),
        ],
        out_specs=pl.BlockSpec((bm, bn), lambda i, j: (i, j)),
        out_shape=jax.ShapeDtypeStruct((rows, n), F32),
        compiler_params=pltpu.CompilerParams(
            dimension_semantics=("arbitrary", "arbitrary"),
            vmem_limit_bytes=_vmem_limit(blocks, 0, _nbytes((bm, bn), F32))),
        name="resid_matmul",
    )(x, w, h, gate)


def _merge_kernel(oa_ref, ob_ref, oc_ref, w_ref, ga_ref, gb_ref, gc_ref, y_ref):
    y = ga_ref[...].astype(F32) * jnp.dot(oa_ref[...], w_ref[0], preferred_element_type=F32)
    y = y + gb_ref[...].astype(F32) * jnp.dot(ob_ref[...], w_ref[1], preferred_element_type=F32)
    y = y + gc_ref[...].astype(F32) * jnp.dot(oc_ref[...], w_ref[2], preferred_element_type=F32)
    y_ref[...] = y.astype(y_ref.dtype)


def _merge(oa, ob, oc, w_branch, widx, gates, rows, *, bm, bn):
    kb = oa.shape[1]
    d = w_branch.shape[3]
    per_branch = d // bn
    blocks = (3 * _nbytes((bm, kb), BF16) + _nbytes((N_BRANCH, kb, bn), BF16) + 4 * _nbytes((bm, bn), BF16))
    o_spec = pl.BlockSpec((bm, kb), lambda i, j: (i, 0))

    def gate_spec(br):
        return pl.BlockSpec((bm, bn), lambda i, j: (i, br * per_branch + j))

    return pl.pallas_call(
        _merge_kernel,
        grid=(rows // bm, d // bn),
        in_specs=[o_spec, o_spec, o_spec,
                  pl.BlockSpec((None, N_BRANCH, kb, bn), lambda i, j: (widx, 0, 0, j)),
                  gate_spec(0), gate_spec(1), gate_spec(2)],
        out_specs=pl.BlockSpec((bm, bn), lambda i, j: (i, j)),
        out_shape=jax.ShapeDtypeStruct((rows, d), BF16),
        compiler_params=pltpu.CompilerParams(
            dimension_semantics=("arbitrary", "arbitrary"),
            vmem_limit_bytes=_vmem_limit(blocks, 0, 4 * _nbytes((bm, bn), F32))),
        name="merge",
    )(oa, ob, oc, w_branch, gates, gates, gates)


GLA_BLOCK = 256
GLA_UNROLL = 4
_NT = (((1,), (1,)), ((), ()))
_TN = (((0,), (0,)), ((), ()))


def _split_bf16(x):
    hi = x.astype(BF16)
    lo = (x - hi.astype(F32)).astype(BF16)
    return hi, lo


def _sum_dot(op, parts):
    acc = jnp.dot(op, parts[0], preferred_element_type=F32)
    for p in parts[1:]:
        acc = acc + jnp.dot(op, p, preferred_element_type=F32)
    return acc


def _gla_kernel(ql_ref, kl_ref, vl_ref, ogl_ref, gdl_ref, qc_ref, kc_ref, vc_ref, ogc_ref, gdc_ref,
                up_ref, gb_ref, gn_ref, al_ref, ac_ref,
                qe_ref, edec_ref, of_ref, ob_ref, kv_ref, st_ref, *, t_lat, t_ctx):
    c = GLA_CHUNK
    br = GLA_BLOCK
    cpb = br // c
    row = lax.broadcasted_iota(jnp.int32, (br, br), 0)
    col = lax.broadcasted_iota(jnp.int32, (br, br), 1)
    same = (row // c) == (col // c)
    keep = (same & (col <= row), same & (col >= row))
    tri = (keep[0].astype(BF16), keep[1].astype(BF16))
    chunk_of_row = lax.broadcasted_iota(jnp.int32, (br, GLA_DK), 0) // c
    q_scale = GLA_DK ** -0.5
    gain = gn_ref[...]
    o_refs = (of_ref, ob_ref)

    st_ref[...] = jnp.zeros_like(st_ref)

    def segment(q_ref, k_ref, v_ref, og_ref, gd_ref, out_ref, t):
        nc = t // c

        def block_body(r, carry):
            rows = pl.ds(pl.multiple_of(r * br, br), br)
            gd = gd_ref[rows, :]
            q = q_ref[rows, :] * q_scale
            k = k_ref[rows, :]
            v = v_ref[rows, :].astype(BF16)
            for d in range(2):
                z = jnp.dot(gd, up_ref[d], preferred_element_type=F32) + gb_ref[d]
                parts = _split_bf16(jax.nn.log_sigmoid(z) * (1.0 / GLA_GATE_NORM))
                cum = _sum_dot(tri[d], parts)
                ends = [cum[j * c + c - 1:j * c + c] if d == 0 else cum[j * c:j * c + 1] for j in range(cpb)]
                cum_end = jnp.concatenate([jnp.broadcast_to(e, (c, e.shape[1])) for e in ends], axis=0)
                qe = (q * jnp.exp(cum)).astype(BF16)
                kinv = (k * jnp.exp(-cum)).astype(BF16)
                kdec = (k * jnp.exp(cum_end - cum)).astype(BF16)
                qe_ref[d, rows, :] = qe
                edec_ref[d, rows, :] = jnp.exp(cum_end)
                a = lax.dot_general(qe, kinv, _NT, preferred_element_type=F32)
                a = jnp.where(keep[d], a, 0.0).astype(BF16)
                o_refs[d][rows, :] = jnp.dot(a, v, preferred_element_type=F32)
                kdec_by_chunk = jnp.concatenate(
                    [jnp.where(chunk_of_row == j, kdec, jnp.zeros_like(kdec)) for j in range(cpb)], axis=1)
                kv = lax.dot_general(v, kdec_by_chunk, _TN, preferred_element_type=F32)
                for j in range(cpb):
                    kv_ref[d, r * cpb + j] = kv[:, j * GLA_DK:(j + 1) * GLA_DK]
            return carry

        lax.fori_loop(0, t // br, block_body, 0)

        def scan_body(n, carry):
            for d in range(2):
                m = n if d == 0 else nc - 1 - n
                rows = pl.ds(pl.multiple_of(m * c, c), c)
                st = st_ref[d]
                o_refs[d][rows, :] += lax.dot_general(qe_ref[d, rows, :], st.astype(BF16), _NT,
                                                      preferred_element_type=F32)
                st_ref[d] = st * edec_ref[d, pl.ds(pl.multiple_of(m * c, c), 1), :] + kv_ref[d, m]
            return carry

        lax.fori_loop(0, nc, scan_body, 0, unroll=GLA_UNROLL)

        def out_body(n, carry):
            rows = pl.ds(pl.multiple_of(n * c, c), c)
            o = of_ref[rows, :] + ob_ref[rows, :]
            y = o * lax.rsqrt(jnp.mean(o * o, axis=-1, keepdims=True) + EPS)
            y = (y * gain) * jax.nn.silu(og_ref[rows, :])
            out_ref[rows, :] = y.astype(out_ref.dtype)
            return carry

        lax.fori_loop(0, nc, out_body, 0, unroll=GLA_UNROLL)

    segment(qc_ref, kc_ref, vc_ref, ogc_ref, gdc_ref, ac_ref, t_ctx)
    segment(ql_ref, kl_ref, vl_ref, ogl_ref, gdl_ref, al_ref, t_lat)


def _gla(pa, gate_up_pad, gate_bias, gla_norm, *, batch):
    t_lat, t_ctx = SEQ, CTX_LEN
    ctx0 = batch * t_lat // t_ctx
    dk, dv = GLA_DK, GLA_DV
    nc = t_lat // GLA_CHUNK

    def lat(width, col0):
        return pl.BlockSpec((t_lat, width), lambda b, h: (b, col0 // width + h))

    def ctx(width, col0):
        return pl.BlockSpec((t_ctx, width), lambda b, h: (ctx0 + b, col0 // width + h))

    lat_gd = pl.BlockSpec((t_lat, 128), lambda b, h: (b, A_GD // 128))
    ctx_gd = pl.BlockSpec((t_ctx, 128), lambda b, h: (ctx0 + b, A_GD // 128))
    blocks = ((t_lat + t_ctx) * (3 * dk + 2 * dv) * 4 + _nbytes((2, 128, dk), F32)
              + (t_lat + t_ctx) * dv * 2)
    scratch_shapes = [pltpu.VMEM((2, t_lat, dk), BF16), pltpu.VMEM((2, t_lat, dk), F32),
                      pltpu.VMEM((t_lat, dv), F32), pltpu.VMEM((t_lat, dv), F32),
                      pltpu.VMEM((2, nc, dv, dk), F32), pltpu.VMEM((2, dv, dk), F32)]
    scratch = (_nbytes((2, t_lat, dk), BF16) + _nbytes((2, t_lat, dk), F32) + 2 * _nbytes((t_lat, dv), F32)
               + _nbytes((2, nc + 1, dv, dk), F32))
    return pl.pallas_call(
        functools.partial(_gla_kernel, t_lat=t_lat, t_ctx=t_ctx),
        grid=(batch, GLA_HEADS),
        in_specs=[lat(dk, A_GQ), lat(dk, A_GK), lat(dv, A_GV), lat(dv, A_GO), lat_gd,
                  ctx(dk, A_GQ), ctx(dk, A_GK), ctx(dv, A_GV), ctx(dv, A_GO), ctx_gd,
                  pl.BlockSpec((2, 128, dk), lambda b, h: (0, 0, h)),
                  pl.BlockSpec((2, 1, dk), lambda b, h: (0, 0, h)),
                  pl.BlockSpec((1, dv), lambda b, h: (0, 0))],
        out_specs=[pl.BlockSpec((t_lat, dv), lambda b, h: (b, h)),
                   pl.BlockSpec((t_ctx, dv), lambda b, h: (b, h))],
        out_shape=[jax.ShapeDtypeStruct((batch * t_lat, GLA_HEADS * dv), BF16),
                   jax.ShapeDtypeStruct((batch * t_ctx, GLA_HEADS * dv), BF16)],
        scratch_shapes=scratch_shapes,
        compiler_params=pltpu.CompilerParams(
            dimension_semantics=("arbitrary", "arbitrary"),
            vmem_limit_bytes=_vmem_limit(blocks, scratch, 6 << 20)),
        name="gla",
    )(pa, pa, pa, pa, pa, pa, pa, pa, pa, pa, gate_up_pad, gate_bias, gla_norm)


def _rms(x, gain):
    return (x * lax.rsqrt(jnp.mean(x * x, axis=-1, keepdims=True) + EPS)) * gain


def _rope(x, cos, sin_signed):
    lane = lax.broadcasted_iota(jnp.int32, x.shape, 1)
    partner = jnp.where((lane % 64) < 32, pltpu.roll(x, 96, 1), pltpu.roll(x, 32, 1))
    return x * cos + partner * sin_signed


def _q_row_block(b, qb, *, batch, n_lat, n_ctx):
    return jnp.where(qb < n_lat, b * n_lat + qb, batch * n_lat + b * n_ctx + (qb - n_lat))


KV_PREP_ROWS = 256


def _swa_window_masks(t_lat):
    span = 3 * QBLK
    q_off = np.array([0, QBLK, 2 * QBLK])[:, None, None] + np.arange(QBLK)[None, :, None]
    k_off = np.arange(span)[None, None, :]
    return jnp.asarray(np.where(np.abs(k_off - q_off) <= SWA_WINDOW, 0.0, NEG_INF), F32)


def _swa_mask_variant(qb, n_lat):
    return jnp.where(qb == 0, 0, jnp.where(qb >= n_lat - 1, 2, 1))


def _swa_kernel(sink_ref, q_ref, kl_ref, vl_ref, kc_ref, vc_ref, cq_ref, sq_ref, ck_ref, sk_ref,
                qg_ref, kg_ref, mask_ref, o_ref, kn_ref, vb_ref, kcn_ref, vcb_ref, *, t_lat, n_lat):
    qb = pl.program_id(1)
    g = SWA_GROUP
    dh = HEAD_DIM
    scale = dh ** -0.5

    @pl.when(qb == 0)
    def _prep():
        k_gain = kg_ref[...]

        def body(r, carry):
            r0 = pl.multiple_of(r * KV_PREP_ROWS, KV_PREP_ROWS)
            rows = pl.ds(r0, KV_PREP_ROWS)
            cos, sin = ck_ref[rows, :], sk_ref[rows, :]
            for kv in range(SWA_KV_HEADS):
                cols = slice(kv * dh, (kv + 1) * dh)
                kn_ref[rows, cols] = _rope(_rms(kl_ref[rows, cols], k_gain), cos, sin).astype(BF16)
            vb_ref[rows, :] = vl_ref[rows, :].astype(BF16)
            return carry

        lax.fori_loop(0, t_lat // KV_PREP_ROWS, body, 0)
        for kv in range(SWA_KV_HEADS):
            cols = slice(kv * dh, (kv + 1) * dh)
            kcn_ref[:, cols] = _rms(kc_ref[:, cols], k_gain).astype(BF16)
        vcb_ref[...] = vc_ref[...].astype(BF16)

    def attend(latent):
        cos, sin = cq_ref[...], sq_ref[...]
        q_gain = qg_ref[...]
        span = 3 * QBLK
        start = pl.multiple_of(jnp.clip((qb - 1) * QBLK, 0, t_lat - span), QBLK)
        for kv in range(SWA_KV_HEADS):
            cols = slice(kv * dh, (kv + 1) * dh)
            heads = [_rope(_rms(q_ref[:, (kv * g + i) * dh:(kv * g + i + 1) * dh], q_gain), cos, sin).astype(BF16)
                     for i in range(g)]
            qs = jnp.concatenate(heads, axis=0)
            s_ctx = lax.dot_general(qs, kcn_ref[:, cols], _NT, preferred_element_type=F32)
            if latent:
                s_loc = lax.dot_general(qs, kn_ref[pl.ds(start, span), cols], _NT, preferred_element_type=F32)
            p_loc, p_ctx, dens = [], [], []
            for i in range(g):
                rows = slice(i * QBLK, (i + 1) * QBLK)
                sink = sink_ref[kv * g + i]
                sc = s_ctx[rows] * scale
                m = jnp.maximum(jnp.max(sc, axis=-1, keepdims=True), sink)
                if latent:
                    sl = s_loc[rows] * scale + mask_ref[...]
                    m = jnp.maximum(m, jnp.max(sl, axis=-1, keepdims=True))
                    pl_i = jnp.exp(sl - m)
                    p_loc.append(pl_i.astype(BF16))
                pc_i = jnp.exp(sc - m)
                p_ctx.append(pc_i.astype(BF16))
                den = jnp.sum(pc_i, axis=-1, keepdims=True) + jnp.exp(sink - m)
                dens.append(den + jnp.sum(pl_i, axis=-1, keepdims=True) if latent else den)
            o = jnp.dot(jnp.concatenate(p_ctx, axis=0), vcb_ref[:, cols], preferred_element_type=F32)
            if latent:
                o = o + jnp.dot(jnp.concatenate(p_loc, axis=0), vb_ref[pl.ds(start, span), cols],
                                preferred_element_type=F32)
            for i in range(g):
                o_i = o[i * QBLK:(i + 1) * QBLK, :] / dens[i]
                o_ref[:, (kv * g + i) * dh:(kv * g + i + 1) * dh] = o_i.astype(o_ref.dtype)

    pl.when(qb < n_lat)(functools.partial(attend, True))
    pl.when(qb >= n_lat)(functools.partial(attend, False))


def _swa(pc, sink, cos_tab, sin_tab, q_gain, k_gain, *, batch, with_ctx_out):
    t_lat, t_ctx, dh = SEQ, CTX_LEN, HEAD_DIM
    qw, kw = SWA_HEADS * dh, SWA_KV_HEADS * dh
    n_lat, n_ctx = t_lat // QBLK, t_ctx // QBLK
    n_q = n_lat + (n_ctx if with_ctx_out else 0)
    rows_out = batch * (t_lat + (t_ctx if with_ctx_out else 0))
    ctx0 = batch * t_lat // t_ctx
    qmap = functools.partial(_q_row_block, batch=batch, n_lat=n_lat, n_ctx=n_ctx)
    blocks = (_nbytes((QBLK, qw), F32) + 2 * _nbytes((t_lat + t_ctx, kw), F32) + 2 * _nbytes((QBLK, dh), F32)
              + 2 * _nbytes((t_lat, dh), F32) + _nbytes((QBLK, qw), BF16))
    scratch = 2 * _nbytes((t_lat + t_ctx, kw), BF16)
    return pl.pallas_call(
        functools.partial(_swa_kernel, t_lat=t_lat, n_lat=n_lat),
        grid=(batch, n_q),
        in_specs=[
            pl.BlockSpec(memory_space=pltpu.SMEM),
            pl.BlockSpec((QBLK, qw), lambda b, qb: (qmap(b, qb), C_SQ // qw)),
            pl.BlockSpec((t_lat, kw), lambda b, qb: (b, C_SK // kw)),
            pl.BlockSpec((t_lat, kw), lambda b, qb: (b, C_SV // kw)),
            pl.BlockSpec((t_ctx, kw), lambda b, qb: (ctx0 + b, C_SK // kw)),
            pl.BlockSpec((t_ctx, kw), lambda b, qb: (ctx0 + b, C_SV // kw)),
            pl.BlockSpec((QBLK, dh), lambda b, qb: (qb, 0)),
            pl.BlockSpec((QBLK, dh), lambda b, qb: (qb, 0)),
            pl.BlockSpec((t_lat, dh), lambda b, qb: (0, 0)),
            pl.BlockSpec((t_lat, dh), lambda b, qb: (0, 0)),
            pl.BlockSpec((1, dh), lambda b, qb: (0, 0)),
            pl.BlockSpec((1, dh), lambda b, qb: (0, 0)),
            pl.BlockSpec((None, QBLK, 3 * QBLK), lambda b, qb: (_swa_mask_variant(qb, n_lat), 0, 0)),
        ],
        out_specs=pl.BlockSpec((QBLK, qw), lambda b, qb: (qmap(b, qb), 0)),
        out_shape=jax.ShapeDtypeStruct((rows_out, qw), BF16),
        scratch_shapes=[pltpu.VMEM((t_lat, kw), BF16), pltpu.VMEM((t_lat, kw), BF16),
                        pltpu.VMEM((t_ctx, kw), BF16), pltpu.VMEM((t_ctx, kw), BF16)],
        compiler_params=pltpu.CompilerParams(
            dimension_semantics=("arbitrary", "arbitrary"),
            vmem_limit_bytes=_vmem_limit(blocks, scratch, 12 << 20)),
        name="swa",
    )(sink, pc, pc, pc, pc, pc, cos_tab, sin_tab, cos_tab, sin_tab, q_gain, k_gain, _swa_window_masks(t_lat))


NA_HEADS_PER_STEP = 4


def _na_slab_start(qb):
    r = qb * (QBLK // GRID_W)
    rows = SEQ // GRID_W
    return jnp.minimum(jnp.clip(r - NA_KH // 2, 0, rows - NA_KH), rows - NA_SLAB_ROWS)


def _na_bias_variant(qb, n_lat):
    return jnp.where(qb < 2, qb, jnp.where(qb < n_lat - 2, 2, jnp.minimum(qb, n_lat - 1) - (n_lat - 5)))


def _na_kernel(q_ref, kl_ref, vl_ref, kc_ref, vc_ref, bias_ref, qg_ref, kg_ref, o_ref,
               kn_ref, vb_ref, kcn_ref, vcb_ref, *, t_lat, n_lat):
    qb = pl.program_id(2)
    dh = HEAD_DIM
    scale = dh ** -0.5

    @pl.when(qb == 0)
    def _prep():
        k_gain = kg_ref[...]

        def body(r, carry):
            r0 = pl.multiple_of(r * KV_PREP_ROWS, KV_PREP_ROWS)
            rows = pl.ds(r0, KV_PREP_ROWS)
            for h in range(NA_HEADS_PER_STEP):
                cols = slice(h * dh, (h + 1) * dh)
                kn_ref[rows, cols] = _rms(kl_ref[rows, cols], k_gain).astype(BF16)
            vb_ref[rows, :] = vl_ref[rows, :].astype(BF16)
            return carry

        lax.fori_loop(0, t_lat // KV_PREP_ROWS, body, 0)
        for h in range(NA_HEADS_PER_STEP):
            cols = slice(h * dh, (h + 1) * dh)
            kcn_ref[:, cols] = _rms(kc_ref[:, cols], k_gain).astype(BF16)
        vcb_ref[...] = vc_ref[...].astype(BF16)

    def attend(latent):
        q_gain = qg_ref[...]
        for h in range(NA_HEADS_PER_STEP):
            cols = slice(h * dh, (h + 1) * dh)
            qn = _rms(q_ref[:, cols], q_gain).astype(BF16)
            s_ctx = lax.dot_general(qn, kcn_ref[:, cols], _NT, preferred_element_type=F32) * scale
            m = jnp.max(s_ctx, axis=-1, keepdims=True)
            if latent:
                span = NA_SLAB_ROWS * GRID_W
                start = pl.multiple_of(_na_slab_start(qb) * GRID_W, GRID_W)
                s_loc = lax.dot_general(qn, kn_ref[pl.ds(start, span), cols], _NT,
                                        preferred_element_type=F32) * scale
                s_loc = s_loc + bias_ref[h]
                m = jnp.maximum(m, jnp.max(s_loc, axis=-1, keepdims=True))
                p_loc = jnp.exp(s_loc - m)
                den = jnp.sum(p_loc, axis=-1, keepdims=True)
                o = jnp.dot(p_loc.astype(BF16), vb_ref[pl.ds(start, span), cols], preferred_element_type=F32)
            p_ctx = jnp.exp(s_ctx - m)
            o_ctx = jnp.dot(p_ctx.astype(BF16), vcb_ref[:, cols], preferred_element_type=F32)
            den_ctx = jnp.sum(p_ctx, axis=-1, keepdims=True)
            o, den = (o + o_ctx, den + den_ctx) if latent else (o_ctx, den_ctx)
            o_ref[:, cols] = (o / den).astype(o_ref.dtype)

    pl.when(qb < n_lat)(functools.partial(attend, True))
    pl.when(qb >= n_lat)(functools.partial(attend, False))


def _na(pc, bias_tab, q_gain, k_gain, *, batch, with_ctx_out):
    t_lat, t_ctx, dh = SEQ, CTX_LEN, HEAD_DIM
    hb = NA_HEADS_PER_STEP
    w = hb * dh
    n_lat, n_ctx = t_lat // QBLK, t_ctx // QBLK
    n_q = n_lat + (n_ctx if with_ctx_out else 0)
    rows_out = batch * (t_lat + (t_ctx if with_ctx_out else 0))
    ctx0 = batch * t_lat // t_ctx
    span = NA_SLAB_ROWS * GRID_W
    qmap = functools.partial(_q_row_block, batch=batch, n_lat=n_lat, n_ctx=n_ctx)
    blocks = (_nbytes((QBLK, w), F32) + 2 * _nbytes((t_lat + t_ctx, w), F32) + _nbytes((hb, QBLK, span), F32)
              + _nbytes((QBLK, w), BF16))
    scratch = 2 * _nbytes((t_lat + t_ctx, w), BF16)
    return pl.pallas_call(
        functools.partial(_na_kernel, t_lat=t_lat, n_lat=n_lat),
        grid=(batch, NA_HEADS // hb, n_q),
        in_specs=[
            pl.BlockSpec((QBLK, w), lambda b, hg, qb: (qmap(b, qb), C_NQ // w + hg)),
            pl.BlockSpec((t_lat, w), lambda b, hg, qb: (b, C_NK // w + hg)),
            pl.BlockSpec((t_lat, w), lambda b, hg, qb: (b, C_NV // w + hg)),
            pl.BlockSpec((t_ctx, w), lambda b, hg, qb: (ctx0 + b, C_NK // w + hg)),
            pl.BlockSpec((t_ctx, w), lambda b, hg, qb: (ctx0 + b, C_NV // w + hg)),
            pl.BlockSpec((hb, None, QBLK, span), lambda b, hg, qb: (hg, _na_bias_variant(qb, n_lat), 0, 0)),
            pl.BlockSpec((1, dh), lambda b, hg, qb: (0, 0)),
            pl.BlockSpec((1, dh), lambda b, hg, qb: (0, 0)),
        ],
        out_specs=pl.BlockSpec((QBLK, w), lambda b, hg, qb: (qmap(b, qb), hg)),
        out_shape=jax.ShapeDtypeStruct((rows_out, NA_HEADS * dh), BF16),
        scratch_shapes=[pltpu.VMEM((t_lat, w), BF16), pltpu.VMEM((t_lat, w), BF16),
                        pltpu.VMEM((t_ctx, w), BF16), pltpu.VMEM((t_ctx, w), BF16)],
        compiler_params=pltpu.CompilerParams(
            dimension_semantics=("arbitrary", "arbitrary", "arbitrary"),
            vmem_limit_bytes=_vmem_limit(blocks, scratch, 12 << 20)),
        name="natten",
    )(pc, pc, pc, pc, pc, bias_tab, q_gain, k_gain)


def _rope_tables():
    quarter = HEAD_DIM // 4
    pos = jnp.arange(SEQ)
    rows = (pos // GRID_W).astype(F32)
    cols = (pos % GRID_W).astype(F32)
    inv = ROPE_BASE ** (-jnp.arange(quarter, dtype=F32) / quarter)
    ang_r = rows[:, None] * inv[None, :]
    ang_c = cols[:, None] * inv[None, :]
    cos = jnp.concatenate([jnp.cos(ang_r), jnp.cos(ang_r), jnp.cos(ang_c), jnp.cos(ang_c)], axis=-1)
    sin = jnp.concatenate([-jnp.sin(ang_r), jnp.sin(ang_r), -jnp.sin(ang_c), jnp.sin(ang_c)], axis=-1)
    cos = jnp.concatenate([cos, jnp.ones((CTX_LEN, HEAD_DIM), F32)], axis=0)
    sin = jnp.concatenate([sin, jnp.zeros((CTX_LEN, HEAD_DIM), F32)], axis=0)
    return cos, sin


def _na_bias_tables(rpb):
    n_layers, n_heads = rpb.shape[:2]
    rows = SEQ // GRID_W
    n_lat = SEQ // QBLK
    rq_per = QBLK // GRID_W
    reps = np.array([0, 1, 2, n_lat - 2, n_lat - 1])
    r = reps * rq_per
    start = np.minimum(np.clip(r - NA_KH // 2, 0, rows - NA_KH), rows - NA_SLAB_ROWS)
    rq = r[:, None] + np.arange(rq_per)[None, :]
    kr = start[:, None] + np.arange(NA_SLAB_ROWS)[None, :]
    r0 = np.clip(rq - NA_KH // 2, 0, rows - NA_KH)
    valid_r = (kr[:, None, :] >= r0[:, :, None]) & (kr[:, None, :] < r0[:, :, None] + NA_KH)
    dr = np.clip(kr[:, None, :] - rq[:, :, None] + NA_KH - 1, 0, 2 * NA_KH - 2)
    qc = np.arange(GRID_W)
    kc = np.arange(GRID_W)
    cs = np.clip(qc - NA_KW // 2, 0, GRID_W - NA_KW)
    valid_c = (kc[None, :] >= cs[:, None]) & (kc[None, :] < cs[:, None] + NA_KW)
    dc = np.clip(kc[None, :] - qc[:, None] + NA_KW - 1, 0, 2 * NA_KW - 2)
    pick_r = np.eye(2 * NA_KH - 1, dtype=np.float32)[dr.reshape(-1)]
    pick_c = np.eye(2 * NA_KW - 1, dtype=np.float32)[dc.reshape(-1)]
    t = jnp.einsum("nr,lhrc->lhnc", pick_r, rpb.astype(F32), precision=HIGHEST)
    t = jnp.einsum("lhnc,xc->lhnx", t, pick_c, precision=HIGHEST)
    t = t.reshape(n_layers, n_heads, len(reps), rq_per, NA_SLAB_ROWS, GRID_W, GRID_W)
    t = t.transpose(0, 1, 2, 3, 5, 4, 6)
    valid = valid_r[:, :, None, :, None] & valid_c[None, None, :, None, :]
    t = jnp.where(valid[None, None], t, NEG_INF)
    return t.reshape(n_layers, n_heads, len(reps), QBLK, NA_SLAB_ROWS * GRID_W)


SUBLANES = 8
CAST_K = 2048
CAST_CHUNK = 256


def _cast_cols_kernel(main_ref, next_ref, o_ref, *, shift):
    bn = o_ref.shape[1]
    for c in range(o_ref.shape[0] // CAST_CHUNK):
        cols = slice(c * CAST_CHUNK, (c + 1) * CAST_CHUNK)
        if shift == 0:
            x = main_ref[:, cols]
        else:
            x = jnp.concatenate([main_ref[shift:bn, cols], next_ref[:, cols]], axis=0)
        o_ref[cols, :] = x.T.astype(o_ref.dtype)


def _cast_cols(w_t, col0, n, *, bn):
    n_layers, _, k = w_t.shape
    shift = col0 % bn
    base = col0 - shift
    nxt = shift if shift else SUBLANES
    assert n % bn == 0 and k % CAST_K == 0 and nxt % SUBLANES == 0 and bn % nxt == 0 and base % nxt == 0
    blocks = _nbytes((bn + nxt, CAST_K), F32) + _nbytes((CAST_K, bn), BF16)
    return pl.pallas_call(
        functools.partial(_cast_cols_kernel, shift=shift),
        grid=(n_layers, k // CAST_K, n // bn),
        in_specs=[
            pl.BlockSpec((None, bn, CAST_K), lambda l, i, j: (l, base // bn + j, i)),
            pl.BlockSpec((None, nxt, CAST_K), lambda l, i, j: (l, (base + (j + 1) * bn) // nxt, i)),
        ],
        out_specs=pl.BlockSpec((None, CAST_K, bn), lambda l, i, j: (l, i, j)),
        out_shape=jax.ShapeDtypeStruct((n_layers, k, n), BF16),
        compiler_params=pltpu.CompilerParams(
            dimension_semantics=("arbitrary", "arbitrary", "arbitrary"),
            vmem_limit_bytes=_vmem_limit(blocks, 0, 4 << 20)),
        name="cast_cols",
    )(w_t, w_t)


def _pad_gate_up(gate_up):
    r = gate_up.shape[1]
    out = jnp.zeros((2, 128, gate_up.shape[2]), F32)
    out = out.at[0, 0:r].set(gate_up[0])
    out = out.at[1, r:2 * r].set(gate_up[1])
    return out


BM = 1024
BM_SMALL = 256
BM_NORM = 512
BN = 512
BN_FF = 512
BN_A = 640
BN_C = 768
BN_MERGE = 512


def kernel(x, c, ctx, c_ctx, ada_down, ada_up, ada_bias, norm_gain, ffn_w_in, ffn_w_out, w_in,
           gla_gate_up, gla_gate_bias, gla_norm, swa_q_norm, swa_k_norm, swa_sink, na_q_norm,
           na_k_norm, na_rpb, w_branch, w_out):
    batch, seq, d = x.shape
    depth = ada_down.shape[0]
    assert (seq, d, ctx.shape[1]) == (SEQ, D_MODEL, CTX_LEN) and batch + 1 <= MOD_ROWS
    assert w_in.shape[2] == C_START + C_QKV_COLS + GATE_COLS
    lat_rows = batch * seq
    all_rows = lat_rows + batch * ctx.shape[1]

    v8 = jnp.concatenate([c, c_ctx[None], jnp.zeros((MOD_ROWS - batch - 1, d), F32)], axis=0)
    mods = _adaln(v8, ada_down, ada_up, ada_bias)
    mods = mods.reshape(depth, N_MOD, MOD_ROWS, 1, d)
    cos_tab, sin_tab = _rope_tables()
    na_bias = _na_bias_tables(na_rpb)

    ffn_in_bf = ffn_w_in.astype(BF16).reshape(depth * 2, d, -1)
    ffn_out_bf = ffn_w_out.astype(BF16).reshape(depth * 2, -1, d)
    w_in_t = jnp.swapaxes(w_in, 1, 2)
    w_a_bf = _cast_cols(w_in_t, 0, A_COLS, bn=BN_A)
    w_c_bf = _cast_cols(w_in_t, C_START, C_QKV_COLS + GATE_COLS, bn=BN_C)
    w_branch_bf = w_branch.astype(BF16)
    w_out_bf = w_out.astype(BF16)

    h = jnp.concatenate([x.reshape(lat_rows, d), ctx.reshape(-1, d)], axis=0)
    bm = BM if all_rows % BM == 0 and lat_rows % BM == 0 else BM_SMALL
    mm = dict(seq=seq, bm=bm)
    norm = dict(seq=seq, bm=BM_NORM if all_rows % BM_NORM == 0 and lat_rows % BM_NORM == 0 else BM_SMALL)

    for l in range(depth):
        last = l == depth - 1
        rows_out = lat_rows if last else all_rows
        gain = norm_gain[l].reshape(3, 1, d)
        m = mods[l]

        xn = _modulate(h, all_rows, gain[0], m[0], m[1], **norm)
        g1 = _swiglu(xn, ffn_in_bf, 2 * l, all_rows, bm=bm, bn=BN_FF)
        h = _resid_matmul(g1, ffn_out_bf, 2 * l, h, all_rows, m[2], weight=MACARON_W, bn=BN, **mm)

        xn = _modulate(h, all_rows, gain[1], m[3], m[4], **norm)
        pa = _matmul(xn, w_a_bf, l, all_rows, 0, A_COLS, bm=bm, bn=BN_A, out_dtype=F32)
        pc = _matmul(xn, w_c_bf, l, all_rows, 0, C_QKV_COLS, bm=bm, bn=BN_C, out_dtype=F32)
        gates = _matmul(xn, w_c_bf, l, rows_out, C_QKV_COLS, GATE_COLS, bm=bm, bn=BN_C,
                        out_dtype=BF16, sigmoid=True)
        a_lat, a_ctx = _gla(pa, _pad_gate_up(gla_gate_up[l]), gla_gate_bias[l].reshape(2, 1, -1),
                            gla_norm[l].reshape(1, -1), batch=batch)
        o_a = a_lat if last else jnp.concatenate([a_lat, a_ctx], axis=0)
        o_b = _swa(pc, swa_sink[l], cos_tab, sin_tab, swa_q_norm[l].reshape(1, -1), swa_k_norm[l].reshape(1, -1),
                   batch=batch, with_ctx_out=not last)
        o_c = _na(pc, na_bias[l], na_q_norm[l].reshape(1, -1), na_k_norm[l].reshape(1, -1),
                  batch=batch, with_ctx_out=not last)
        y = _merge(o_a, o_b, o_c, w_branch_bf, l, gates, rows_out, bm=bm, bn=BN_MERGE)
        h = _resid_matmul(y, w_out_bf, l, h, rows_out, m[5], weight=1.0, bn=BN, **mm)

        xn = _modulate(h, rows_out, gain[2], m[6], m[7], **norm)
        g2 = _swiglu(xn, ffn_in_bf, 2 * l + 1, rows_out, bm=bm, bn=BN_FF)
        h = _resid_matmul(g2, ffn_out_bf, 2 * l + 1, h, rows_out, m[8], weight=MACARON_W, bn=BN, **mm)

    return h.reshape(batch, seq, d)
```

```python
import functools

import jax
import jax.numpy as jnp
import numpy as np
from jax import lax
from jax.experimental import pallas as pl
from jax.experimental.pallas import tpu as pltpu

F32 = jnp.float32
BF16 = jnp.bfloat16
HIGHEST = lax.Precision.HIGHEST

D_MODEL = 4096
SEQ = 2048
CTX_LEN = 256
GRID_W = 64
HEAD_DIM = 128
GLA_HEADS = 4
GLA_DK = 128
GLA_DV = 256
GLA_GATE_RANK = 16
GLA_GATE_NORM = 16.0
GLA_CHUNK = 64
SWA_HEADS = 8
SWA_KV_HEADS = 2
SWA_GROUP = SWA_HEADS // SWA_KV_HEADS
SWA_WINDOW = 128
NA_HEADS = 8
NA_KH = 8
NA_KW = 16
N_BRANCH = 3
BRANCH_W = 1024
D_FF = 4096
MACARON_W = 0.5
N_MOD = 9
ROPE_BASE = 10000.0
EPS = 1e-6
NEG_INF = -1e30
MOD_ROWS = 8

A_GQ, A_GK, A_GV, A_GO, A_GD = 0, 512, 1024, 2048, 3072
A_COLS = 3200
C_START = 3104
C_SQ, C_SK, C_SV = 0, 1024, 1280
C_NQ, C_NK, C_NV = 1536, 2560, 3584
C_QKV_COLS = 4608
GATE_COLS = N_BRANCH * D_MODEL

V7X_VMEM_BYTES = 64 * 1024 * 1024
V7X_VMEM_RESERVE = 6 * 1024 * 1024
QBLK = 128
NA_SLAB_ROWS = 10


def _vmem_limit(pipelined_bytes, scratch_bytes=0, temp_bytes=0):
    need = 2 * pipelined_bytes + scratch_bytes + temp_bytes + (4 << 20)
    return int(min(max(need, 16 << 20), V7X_VMEM_BYTES - V7X_VMEM_RESERVE))


def _nbytes(shape, dtype):
    return int(np.prod(shape)) * jnp.dtype(dtype).itemsize


def _adaln_kernel(v_ref, down_ref, up_ref, bias_ref, o_ref, t_ref):
    @pl.when(pl.program_id(1) == 0)
    def _():
        t_ref[...] = jnp.dot(jax.nn.silu(v_ref[...]), down_ref[...], preferred_element_type=F32,
                             precision=HIGHEST)

    o_ref[...] = jnp.dot(t_ref[...], up_ref[...], preferred_element_type=F32, precision=HIGHEST) + bias_ref[...]


def _adaln(v8, ada_down, ada_up, ada_bias):
    depth, d, rank = ada_down.shape
    blocks = (_nbytes((MOD_ROWS, d), F32) + _nbytes((d, rank), F32) + _nbytes((rank, d), F32)
              + _nbytes((1, d), F32) + _nbytes((MOD_ROWS, d), F32))
    return pl.pallas_call(
        _adaln_kernel,
        grid=(depth, N_MOD),
        in_specs=[
            pl.BlockSpec((MOD_ROWS, d), lambda l, j: (0, 0)),
            pl.BlockSpec((None, d, rank), lambda l, j: (l, 0, 0)),
            pl.BlockSpec((None, rank, d), lambda l, j: (l, 0, j)),
            pl.BlockSpec((None, 1, d), lambda l, j: (l, 0, j)),
        ],
        out_specs=pl.BlockSpec((None, None, MOD_ROWS, d), lambda l, j: (l, j, 0, 0)),
        out_shape=jax.ShapeDtypeStruct((depth, N_MOD, MOD_ROWS, d), F32),
        scratch_shapes=[pltpu.VMEM((MOD_ROWS, rank), F32)],
        compiler_params=pltpu.CompilerParams(
            dimension_semantics=("arbitrary", "arbitrary"), vmem_limit_bytes=_vmem_limit(blocks)),
        name="adaln",
    )(v8, ada_down, ada_up, ada_bias.reshape(depth, 1, -1))


def _mod_row_index(i, bm, seq):
    return jnp.minimum((i * bm) // seq, MOD_ROWS - 1)


def _mod_spec(bm, seq, d):
    return pl.BlockSpec((None, 1, d), lambda i, j: (_mod_row_index(i, bm, seq), 0, 0))


NORM_ROWS = 32


def _modulate_kernel(h_ref, g_ref, sh_ref, sc_ref, xn_ref, *, bm):
    gain = g_ref[...]
    scale1 = 1.0 + sc_ref[...]
    shift = sh_ref[...]

    def body(c, carry):
        r0 = pl.multiple_of(c * NORM_ROWS, NORM_ROWS)
        x = h_ref[pl.ds(r0, NORM_ROWS), :]
        y = x * lax.rsqrt(jnp.mean(x * x, axis=-1, keepdims=True) + EPS)
        xn_ref[pl.ds(r0, NORM_ROWS), :] = ((y * gain) * scale1 + shift).astype(BF16)
        return carry

    lax.fori_loop(0, bm // NORM_ROWS, body, 0)


def _modulate(h, rows, gain, shift, scale, *, seq, bm):
    d = h.shape[1]
    blocks = _nbytes((bm, d), F32) + _nbytes((bm, d), BF16) + 3 * _nbytes((1, d), F32)
    return pl.pallas_call(
        functools.partial(_modulate_kernel, bm=bm),
        grid=(rows // bm, 1),
        in_specs=[
            pl.BlockSpec((bm, d), lambda i, j: (i, 0)),
            pl.BlockSpec((1, d), lambda i, j: (0, 0)),
            _mod_spec(bm, seq, d),
            _mod_spec(bm, seq, d),
        ],
        out_specs=pl.BlockSpec((bm, d), lambda i, j: (i, 0)),
        out_shape=jax.ShapeDtypeStruct((rows, d), BF16),
        compiler_params=pltpu.CompilerParams(
            dimension_semantics=("arbitrary", "arbitrary"), vmem_limit_bytes=_vmem_limit(blocks)),
        name="modulate",
    )(h, gain, shift, scale)


def _swiglu_kernel(x_ref, wa_ref, wb_ref, o_ref):
    x = x_ref[...]
    a = jnp.dot(x, wa_ref[...], preferred_element_type=F32)
    b = jnp.dot(x, wb_ref[...], preferred_element_type=F32)
    o_ref[...] = (jax.nn.silu(a) * b).astype(o_ref.dtype)


def _swiglu(x, w_in, widx, rows, *, bm, bn):
    d = x.shape[1]
    f = w_in.shape[2] // 2
    nb = f // bn
    blocks = _nbytes((bm, d), BF16) + 2 * _nbytes((d, bn), BF16) + _nbytes((bm, bn), BF16)
    return pl.pallas_call(
        _swiglu_kernel,
        grid=(rows // bm, nb),
        in_specs=[
            pl.BlockSpec((bm, d), lambda i, j: (i, 0)),
            pl.BlockSpec((None, d, bn), lambda i, j: (widx, 0, j)),
            pl.BlockSpec((None, d, bn), lambda i, j: (widx, 0, j + nb)),
        ],
        out_specs=pl.BlockSpec((bm, bn), lambda i, j: (i, j)),
        out_shape=jax.ShapeDtypeStruct((rows, f), BF16),
        compiler_params=pltpu.CompilerParams(
            dimension_semantics=("arbitrary", "arbitrary"),
            vmem_limit_bytes=_vmem_limit(blocks, 0, 4 * _nbytes((bm, bn), F32))),
        name="swiglu",
    )(x, w_in, w_in)


def _mm_kernel(x_ref, w_ref, o_ref, *, sigmoid):
    y = jnp.dot(x_ref[...], w_ref[...], preferred_element_type=F32)
    if sigmoid:
        y = jax.nn.sigmoid(y)
    o_ref[...] = y.astype(o_ref.dtype)


def _matmul(x, w, widx, rows, col0, n, *, bm, bn, out_dtype, sigmoid=False):
    k = x.shape[1]
    jb0 = col0 // bn
    blocks = _nbytes((bm, k), BF16) + _nbytes((k, bn), BF16) + _nbytes((bm, bn), out_dtype)
    return pl.pallas_call(
        functools.partial(_mm_kernel, sigmoid=sigmoid),
        grid=(rows // bm, n // bn),
        in_specs=[
            pl.BlockSpec((bm, k), lambda i, j: (i, 0)),
            pl.BlockSpec((None, k, bn), lambda i, j: (widx, 0, jb0 + j)),
        ],
        out_specs=pl.BlockSpec((bm, bn), lambda i, j: (i, j)),
        out_shape=jax.ShapeDtypeStruct((rows, n), out_dtype),
        compiler_params=pltpu.CompilerParams(
            dimension_semantics=("arbitrary", "arbitrary"),
            vmem_limit_bytes=_vmem_limit(blocks, 0, 2 * _nbytes((bm, bn), F32))),
        name="matmul_sigmoid" if sigmoid else "matmul",
    )(x, w)


def _resid_mm_kernel(x_ref, w_ref, h_ref, gate_ref, o_ref, *, weight):
    y = jnp.dot(x_ref[...], w_ref[...], preferred_element_type=F32)
    if weight != 1.0:
        o_ref[...] = h_ref[...] + weight * gate_ref[...] * y
    else:
        o_ref[...] = h_ref[...] + gate_ref[...] * y


def _resid_matmul(x, w, widx, h, rows, gate, *, weight, seq, bm, bn):
    k = x.shape[1]
    n = w.shape[2]
    blocks = (_nbytes((bm, k), BF16) + _nbytes((k, bn), BF16) + 2 * _nbytes((bm, bn), F32)
              + _nbytes((1, bn), F32))
    return pl.pallas_call(
        functools.partial(_resid_mm_kernel, weight=weight),
        grid=(rows // bm, n // bn),
        in_specs=[
            pl.BlockSpec((bm, k), lambda i, j: (i, 0)),
            pl.BlockSpec((None, k, bn), lambda i, j: (widx, 0, j)),
            pl.BlockSpec((bm, bn), lambda i, j: (i, j)),
            pl.BlockSpec((None, 1, bn), lambda i, j: (_mod_row_index(i, bm, seq), 0, j)),
        ],
        out_specs=pl.BlockSpec((bm, bn), lambda i, j: (i, j)),
        out_shape=jax.ShapeDtypeStruct((rows, n), F32),
        compiler_params=pltpu.CompilerParams(
            dimension_semantics=("arbitrary", "arbitrary"),
            vmem_limit_bytes=_vmem_limit(blocks, 0, _nbytes((bm, bn), F32))),
        name="resid_matmul",
    )(x, w, h, gate)


def _merge_kernel(oa_ref, ob_ref, oc_ref, w_ref, ga_ref, gb_ref, gc_ref, y_ref):
    y = ga_ref[...].astype(F32) * jnp.dot(oa_ref[...], w_ref[0], preferred_element_type=F32)
    y = y + gb_ref[...].astype(F32) * jnp.dot(ob_ref[...], w_ref[1], preferred_element_type=F32)
    y = y + gc_ref[...].astype(F32) * jnp.dot(oc_ref[...], w_ref[2], preferred_element_type=F32)
    y_ref[...] = y.astype(y_ref.dtype)


def _merge(oa, ob, oc, w_branch, widx, gates, rows, *, bm, bn):
    kb = oa.shape[1]
    d = w_branch.shape[3]
    per_branch = d // bn
    blocks = (3 * _nbytes((bm, kb), BF16) + _nbytes((N_BRANCH, kb, bn), BF16) + 4 * _nbytes((bm, bn), BF16))
    o_spec = pl.BlockSpec((bm, kb), lambda i, j: (i, 0))

    def gate_spec(br):
        return pl.BlockSpec((bm, bn), lambda i, j: (i, br * per_branch + j))

    return pl.pallas_call(
        _merge_kernel,
        grid=(rows // bm, d // bn),
        in_specs=[o_spec, o_spec, o_spec,
                  pl.BlockSpec((None, N_BRANCH, kb, bn), lambda i, j: (widx, 0, 0, j)),
                  gate_spec(0), gate_spec(1), gate_spec(2)],
        out_specs=pl.BlockSpec((bm, bn), lambda i, j: (i, j)),
        out_shape=jax.ShapeDtypeStruct((rows, d), BF16),
        compiler_params=pltpu.CompilerParams(
            dimension_semantics=("arbitrary", "arbitrary"),
            vmem_limit_bytes=_vmem_limit(blocks, 0, 4 * _nbytes((bm, bn), F32))),
        name="merge",
    )(oa, ob, oc, w_branch, gates, gates, gates)


GLA_BLOCK = 256
GLA_UNROLL = 4
GLA_BLOCKS_PER_STEP = 2
_NT = (((1,), (1,)), ((), ()))
_TN = (((0,), (0,)), ((), ()))


def _split_bf16(x):
    hi = x.astype(BF16)
    lo = (x - hi.astype(F32)).astype(BF16)
    return hi, lo


def _sum_dot(op, parts):
    acc = jnp.dot(op, parts[0], preferred_element_type=F32)
    for p in parts[1:]:
        acc = acc + jnp.dot(op, p, preferred_element_type=F32)
    return acc


def _gla_kernel(ql_ref, kl_ref, vl_ref, ogl_ref, gdl_ref, qc_ref, kc_ref, vc_ref, ogc_ref, gdc_ref,
                up_ref, gb_ref, gn_ref, al_ref, ac_ref,
                qe_ref, edec_ref, of_ref, ob_ref, kv_ref, st_ref, *, t_lat, t_ctx):
    c = GLA_CHUNK
    br = GLA_BLOCK
    cpb = br // c
    row = lax.broadcasted_iota(jnp.int32, (br, br), 0)
    col = lax.broadcasted_iota(jnp.int32, (br, br), 1)
    same = (row // c) == (col // c)
    keep = (same & (col <= row), same & (col >= row))
    tri = (keep[0].astype(BF16), keep[1].astype(BF16))
    chunk_of_row = lax.broadcasted_iota(jnp.int32, (br, GLA_DK), 0) // c
    q_scale = GLA_DK ** -0.5
    gain = gn_ref[...]
    o_refs = (of_ref, ob_ref)

    st_ref[...] = jnp.zeros_like(st_ref)

    def segment(q_ref, k_ref, v_ref, og_ref, gd_ref, out_ref, t):
        nc = t // c

        per_step = min(GLA_BLOCKS_PER_STEP, t // br)

        def block_body(r, carry):
            blocks = [r * per_step + i for i in range(per_step)]
            rows = [pl.ds(pl.multiple_of(b * br, br), br) for b in blocks]
            chains = [(i, d) for i in range(per_step) for d in range(2)]
            gd = [gd_ref[rw, :] for rw in rows]
            q = [q_ref[rw, :] * q_scale for rw in rows]
            k = [k_ref[rw, :] for rw in rows]
            v = [v_ref[rw, :].astype(BF16) for rw in rows]
            z = [jnp.dot(gd[i], up_ref[d], preferred_element_type=F32) + gb_ref[d] for i, d in chains]
            parts = [_split_bf16(jax.nn.log_sigmoid(zz) * (1.0 / GLA_GATE_NORM)) for zz in z]
            cum = [_sum_dot(tri[d], p) for (i, d), p in zip(chains, parts)]
            cum_end = []
            for (i, d), cm in zip(chains, cum):
                ends = [cm[j * c + c - 1:j * c + c] if d == 0 else cm[j * c:j * c + 1] for j in range(cpb)]
                cum_end.append(jnp.concatenate([jnp.broadcast_to(e, (c, e.shape[1])) for e in ends], axis=0))
            qe = [(q[i] * jnp.exp(cm)).astype(BF16) for (i, d), cm in zip(chains, cum)]
            kinv = [(k[i] * jnp.exp(-cm)).astype(BF16) for (i, d), cm in zip(chains, cum)]
            kdec = [(k[i] * jnp.exp(ce - cm)).astype(BF16) for (i, d), cm, ce in zip(chains, cum, cum_end)]
            a = [lax.dot_general(qq, kk, _NT, preferred_element_type=F32) for qq, kk in zip(qe, kinv)]
            kdec_by_chunk = [jnp.concatenate(
                [jnp.where(chunk_of_row == j, kd, jnp.zeros_like(kd)) for j in range(cpb)], axis=1) for kd in kdec]
            kv = [lax.dot_general(v[i], kd, _TN, preferred_element_type=F32)
                  for (i, d), kd in zip(chains, kdec_by_chunk)]
            a = [jnp.where(keep[d], aa, 0.0).astype(BF16) for (i, d), aa in zip(chains, a)]
            o = [jnp.dot(aa, v[i], preferred_element_type=F32) for (i, d), aa in zip(chains, a)]
            for n, (i, d) in enumerate(chains):
                qe_ref[d, rows[i], :] = qe[n]
                edec_ref[d, rows[i], :] = jnp.exp(cum_end[n])
                o_refs[d][rows[i], :] = o[n]
                for j in range(cpb):
                    kv_ref[d, blocks[i] * cpb + j] = kv[n][:, j * GLA_DK:(j + 1) * GLA_DK]
            return carry

        lax.fori_loop(0, t // (br * per_step), block_body, 0)

        def scan_body(n, carry):
            for d in range(2):
                m = n if d == 0 else nc - 1 - n
                rows = pl.ds(pl.multiple_of(m * c, c), c)
                st = st_ref[d]
                o_refs[d][rows, :] += lax.dot_general(qe_ref[d, rows, :], st.astype(BF16), _NT,
                                                      preferred_element_type=F32)
                st_ref[d] = st * edec_ref[d, pl.ds(pl.multiple_of(m * c, c), 1), :] + kv_ref[d, m]
            return carry

        lax.fori_loop(0, nc, scan_body, 0, unroll=GLA_UNROLL)

        def out_body(n, carry):
            rows = pl.ds(pl.multiple_of(n * c, c), c)
            o = of_ref[rows, :] + ob_ref[rows, :]
            y = o * lax.rsqrt(jnp.mean(o * o, axis=-1, keepdims=True) + EPS)
            y = (y * gain) * jax.nn.silu(og_ref[rows, :])
            out_ref[rows, :] = y.astype(out_ref.dtype)
            return carry

        lax.fori_loop(0, nc, out_body, 0, unroll=GLA_UNROLL)

    segment(qc_ref, kc_ref, vc_ref, ogc_ref, gdc_ref, ac_ref, t_ctx)
    segment(ql_ref, kl_ref, vl_ref, ogl_ref, gdl_ref, al_ref, t_lat)


def _gla(pa, gate_up_pad, gate_bias, gla_norm, *, batch):
    t_lat, t_ctx = SEQ, CTX_LEN
    ctx0 = batch * t_lat // t_ctx
    dk, dv = GLA_DK, GLA_DV
    nc = t_lat // GLA_CHUNK

    def lat(width, col0):
        return pl.BlockSpec((t_lat, width), lambda b, h: (b, col0 // width + h))

    def ctx(width, col0):
        return pl.BlockSpec((t_ctx, width), lambda b, h: (ctx0 + b, col0 // width + h))

    lat_gd = pl.BlockSpec((t_lat, 128), lambda b, h: (b, A_GD // 128))
    ctx_gd = pl.BlockSpec((t_ctx, 128), lambda b, h: (ctx0 + b, A_GD // 128))
    blocks = ((t_lat + t_ctx) * (3 * dk + 2 * dv) * 4 + _nbytes((2, 128, dk), F32)
              + (t_lat + t_ctx) * dv * 2)
    scratch_shapes = [pltpu.VMEM((2, t_lat, dk), BF16), pltpu.VMEM((2, t_lat, dk), F32),
                      pltpu.VMEM((t_lat, dv), F32), pltpu.VMEM((t_lat, dv), F32),
                      pltpu.VMEM((2, nc, dv, dk), F32), pltpu.VMEM((2, dv, dk), F32)]
    scratch = (_nbytes((2, t_lat, dk), BF16) + _nbytes((2, t_lat, dk), F32) + 2 * _nbytes((t_lat, dv), F32)
               + _nbytes((2, nc + 1, dv, dk), F32))
    return pl.pallas_call(
        functools.partial(_gla_kernel, t_lat=t_lat, t_ctx=t_ctx),
        grid=(batch, GLA_HEADS),
        in_specs=[lat(dk, A_GQ), lat(dk, A_GK), lat(dv, A_GV), lat(dv, A_GO), lat_gd,
                  ctx(dk, A_GQ), ctx(dk, A_GK), ctx(dv, A_GV), ctx(dv, A_GO), ctx_gd,
                  pl.BlockSpec((2, 128, dk), lambda b, h: (0, 0, h)),
                  pl.BlockSpec((2, 1, dk), lambda b, h: (0, 0, h)),
                  pl.BlockSpec((1, dv), lambda b, h: (0, 0))],
        out_specs=[pl.BlockSpec((t_lat, dv), lambda b, h: (b, h)),
                   pl.BlockSpec((t_ctx, dv), lambda b, h: (b, h))],
        out_shape=[jax.ShapeDtypeStruct((batch * t_lat, GLA_HEADS * dv), BF16),
                   jax.ShapeDtypeStruct((batch * t_ctx, GLA_HEADS * dv), BF16)],
        scratch_shapes=scratch_shapes,
        compiler_params=pltpu.CompilerParams(
            dimension_semantics=("arbitrary", "arbitrary"),
            vmem_limit_bytes=_vmem_limit(blocks, scratch, 6 << 20)),
        name="gla",
    )(pa, pa, pa, pa, pa, pa, pa, pa, pa, pa, gate_up_pad, gate_bias, gla_norm)


def _rms(x, gain):
    return (x * lax.rsqrt(jnp.mean(x * x, axis=-1, keepdims=True) + EPS)) * gain


def _rope(x, cos, sin_signed):
    lane = lax.broadcasted_iota(jnp.int32, x.shape, 1)
    partner = jnp.where((lane % 64) < 32, pltpu.roll(x, 96, 1), pltpu.roll(x, 32, 1))
    return x * cos + partner * sin_signed


def _q_row_block(b, qb, *, batch, n_lat, n_ctx):
    return jnp.where(qb < n_lat, b * n_lat + qb, batch * n_lat + b * n_ctx + (qb - n_lat))


KV_PREP_ROWS = 256


def _swa_window_masks(t_lat):
    span = 3 * QBLK
    q_off = np.array([0, QBLK, 2 * QBLK])[:, None, None] + np.arange(QBLK)[None, :, None]
    k_off = np.arange(span)[None, None, :]
    return jnp.asarray(np.where(np.abs(k_off - q_off) <= SWA_WINDOW, 0.0, NEG_INF), F32)


def _swa_mask_variant(qb, n_lat):
    return jnp.where(qb == 0, 0, jnp.where(qb >= n_lat - 1, 2, 1))


def _swa_kernel(sink_ref, q_ref, kl_ref, vl_ref, kc_ref, vc_ref, cq_ref, sq_ref, ck_ref, sk_ref,
                qg_ref, kg_ref, mask_ref, o_ref, kn_ref, vb_ref, kcn_ref, vcb_ref, *, t_lat, n_lat):
    qb = pl.program_id(1)
    g = SWA_GROUP
    dh = HEAD_DIM
    scale = dh ** -0.5

    @pl.when(qb == 0)
    def _prep():
        k_gain = kg_ref[...]

        def body(r, carry):
            r0 = pl.multiple_of(r * KV_PREP_ROWS, KV_PREP_ROWS)
            rows = pl.ds(r0, KV_PREP_ROWS)
            cos, sin = ck_ref[rows, :], sk_ref[rows, :]
            for kv in range(SWA_KV_HEADS):
                cols = slice(kv * dh, (kv + 1) * dh)
                kn_ref[rows, cols] = _rope(_rms(kl_ref[rows, cols], k_gain), cos, sin).astype(BF16)
            vb_ref[rows, :] = vl_ref[rows, :].astype(BF16)
            return carry

        lax.fori_loop(0, t_lat // KV_PREP_ROWS, body, 0)
        for kv in range(SWA_KV_HEADS):
            cols = slice(kv * dh, (kv + 1) * dh)
            kcn_ref[:, cols] = _rms(kc_ref[:, cols], k_gain).astype(BF16)
        vcb_ref[...] = vc_ref[...].astype(BF16)

    def attend(latent):
        cos, sin = cq_ref[...], sq_ref[...]
        q_gain = qg_ref[...]
        span = 3 * QBLK
        start = pl.multiple_of(jnp.clip((qb - 1) * QBLK, 0, t_lat - span), QBLK)
        kvs = range(SWA_KV_HEADS)
        cols = [slice(kv * dh, (kv + 1) * dh) for kv in kvs]
        qs = []
        for kv in kvs:
            heads = [_rope(_rms(q_ref[:, (kv * g + i) * dh:(kv * g + i + 1) * dh], q_gain), cos, sin).astype(BF16)
                     for i in range(g)]
            qs.append(jnp.concatenate(heads, axis=0))
        s_ctx = [lax.dot_general(qs[kv], kcn_ref[:, cols[kv]], _NT, preferred_element_type=F32) for kv in kvs]
        if latent:
            s_loc = [lax.dot_general(qs[kv], kn_ref[pl.ds(start, span), cols[kv]], _NT,
                                     preferred_element_type=F32) for kv in kvs]
        p_loc, p_ctx, dens = ([[] for _ in kvs] for _ in range(3))
        for i in range(g):
            rows = slice(i * QBLK, (i + 1) * QBLK)
            for kv in kvs:
                sink = sink_ref[kv * g + i]
                sc = s_ctx[kv][rows] * scale
                m = jnp.maximum(jnp.max(sc, axis=-1, keepdims=True), sink)
                if latent:
                    sl = s_loc[kv][rows] * scale + mask_ref[...]
                    m = jnp.maximum(m, jnp.max(sl, axis=-1, keepdims=True))
                    pl_i = jnp.exp(sl - m)
                    p_loc[kv].append(pl_i.astype(BF16))
                pc_i = jnp.exp(sc - m)
                p_ctx[kv].append(pc_i.astype(BF16))
                den = jnp.sum(pc_i, axis=-1, keepdims=True) + jnp.exp(sink - m)
                dens[kv].append(den + jnp.sum(pl_i, axis=-1, keepdims=True) if latent else den)
        o = [jnp.dot(jnp.concatenate(p_ctx[kv], axis=0), vcb_ref[:, cols[kv]], preferred_element_type=F32)
             for kv in kvs]
        if latent:
            o = [o[kv] + jnp.dot(jnp.concatenate(p_loc[kv], axis=0), vb_ref[pl.ds(start, span), cols[kv]],
                                 preferred_element_type=F32) for kv in kvs]
        for kv in kvs:
            for i in range(g):
                o_i = o[kv][i * QBLK:(i + 1) * QBLK, :] / dens[kv][i]
                o_ref[:, (kv * g + i) * dh:(kv * g + i + 1) * dh] = o_i.astype(o_ref.dtype)

    pl.when(qb < n_lat)(functools.partial(attend, True))
    pl.when(qb >= n_lat)(functools.partial(attend, False))


def _swa(pc, sink, cos_tab, sin_tab, q_gain, k_gain, *, batch, with_ctx_out):
    t_lat, t_ctx, dh = SEQ, CTX_LEN, HEAD_DIM
    qw, kw = SWA_HEADS * dh, SWA_KV_HEADS * dh
    n_lat, n_ctx = t_lat // QBLK, t_ctx // QBLK
    n_q = n_lat + (n_ctx if with_ctx_out else 0)
    rows_out = batch * (t_lat + (t_ctx if with_ctx_out else 0))
    ctx0 = batch * t_lat // t_ctx
    qmap = functools.partial(_q_row_block, batch=batch, n_lat=n_lat, n_ctx=n_ctx)
    blocks = (_nbytes((QBLK, qw), F32) + 2 * _nbytes((t_lat + t_ctx, kw), F32) + 2 * _nbytes((QBLK, dh), F32)
              + 2 * _nbytes((t_lat, dh), F32) + _nbytes((QBLK, qw), BF16))
    scratch = 2 * _nbytes((t_lat + t_ctx, kw), BF16)
    return pl.pallas_call(
        functools.partial(_swa_kernel, t_lat=t_lat, n_lat=n_lat),
        grid=(batch, n_q),
        in_specs=[
            pl.BlockSpec(memory_space=pltpu.SMEM),
            pl.BlockSpec((QBLK, qw), lambda b, qb: (qmap(b, qb), C_SQ // qw)),
            pl.BlockSpec((t_lat, kw), lambda b, qb: (b, C_SK // kw)),
            pl.BlockSpec((t_lat, kw), lambda b, qb: (b, C_SV // kw)),
            pl.BlockSpec((t_ctx, kw), lambda b, qb: (ctx0 + b, C_SK // kw)),
            pl.BlockSpec((t_ctx, kw), lambda b, qb: (ctx0 + b, C_SV // kw)),
            pl.BlockSpec((QBLK, dh), lambda b, qb: (qb, 0)),
            pl.BlockSpec((QBLK, dh), lambda b, qb: (qb, 0)),
            pl.BlockSpec((t_lat, dh), lambda b, qb: (0, 0)),
            pl.BlockSpec((t_lat, dh), lambda b, qb: (0, 0)),
            pl.BlockSpec((1, dh), lambda b, qb: (0, 0)),
            pl.BlockSpec((1, dh), lambda b, qb: (0, 0)),
            pl.BlockSpec((None, QBLK, 3 * QBLK), lambda b, qb: (_swa_mask_variant(qb, n_lat), 0, 0)),
        ],
        out_specs=pl.BlockSpec((QBLK, qw), lambda b, qb: (qmap(b, qb), 0)),
        out_shape=jax.ShapeDtypeStruct((rows_out, qw), BF16),
        scratch_shapes=[pltpu.VMEM((t_lat, kw), BF16), pltpu.VMEM((t_lat, kw), BF16),
                        pltpu.VMEM((t_ctx, kw), BF16), pltpu.VMEM((t_ctx, kw), BF16)],
        compiler_params=pltpu.CompilerParams(
            dimension_semantics=("arbitrary", "arbitrary"),
            vmem_limit_bytes=_vmem_limit(blocks, scratch, 12 << 20)),
        name="swa",
    )(sink, pc, pc, pc, pc, pc, cos_tab, sin_tab, cos_tab, sin_tab, q_gain, k_gain, _swa_window_masks(t_lat))


NA_HEADS_PER_STEP = 4


def _na_slab_start(qb):
    r = qb * (QBLK // GRID_W)
    rows = SEQ // GRID_W
    return jnp.minimum(jnp.clip(r - NA_KH // 2, 0, rows - NA_KH), rows - NA_SLAB_ROWS)


def _na_bias_variant(qb, n_lat):
    return jnp.where(qb < 2, qb, jnp.where(qb < n_lat - 2, 2, jnp.minimum(qb, n_lat - 1) - (n_lat - 5)))


def _na_kernel(q_ref, kl_ref, vl_ref, kc_ref, vc_ref, bias_ref, qg_ref, kg_ref, o_ref,
               kn_ref, vb_ref, kcn_ref, vcb_ref, *, t_lat, n_lat):
    qb = pl.program_id(2)
    dh = HEAD_DIM
    scale = dh ** -0.5

    @pl.when(qb == 0)
    def _prep():
        k_gain = kg_ref[...]

        def body(r, carry):
            r0 = pl.multiple_of(r * KV_PREP_ROWS, KV_PREP_ROWS)
            rows = pl.ds(r0, KV_PREP_ROWS)
            for h in range(NA_HEADS_PER_STEP):
                cols = slice(h * dh, (h + 1) * dh)
                kn_ref[rows, cols] = _rms(kl_ref[rows, cols], k_gain).astype(BF16)
            vb_ref[rows, :] = vl_ref[rows, :].astype(BF16)
            return carry

        lax.fori_loop(0, t_lat // KV_PREP_ROWS, body, 0)
        for h in range(NA_HEADS_PER_STEP):
            cols = slice(h * dh, (h + 1) * dh)
            kcn_ref[:, cols] = _rms(kc_ref[:, cols], k_gain).astype(BF16)
        vcb_ref[...] = vc_ref[...].astype(BF16)

    def attend(latent):
        q_gain = qg_ref[...]
        heads = range(NA_HEADS_PER_STEP)
        cols = [slice(h * dh, (h + 1) * dh) for h in heads]
        span = NA_SLAB_ROWS * GRID_W
        start = pl.multiple_of(_na_slab_start(qb) * GRID_W, GRID_W)
        qn = [_rms(q_ref[:, cols[h]], q_gain).astype(BF16) for h in heads]
        s_ctx = [lax.dot_general(qn[h], kcn_ref[:, cols[h]], _NT, preferred_element_type=F32) * scale
                 for h in heads]
        m = [jnp.max(s, axis=-1, keepdims=True) for s in s_ctx]
        if latent:
            s_loc = [lax.dot_general(qn[h], kn_ref[pl.ds(start, span), cols[h]], _NT,
                                     preferred_element_type=F32) for h in heads]
            s_loc = [s_loc[h] * scale + bias_ref[h] for h in heads]
            m = [jnp.maximum(m[h], jnp.max(s_loc[h], axis=-1, keepdims=True)) for h in heads]
            p_loc = [jnp.exp(s_loc[h] - m[h]) for h in heads]
        p_ctx = [jnp.exp(s_ctx[h] - m[h]) for h in heads]
        o = [jnp.dot(p_ctx[h].astype(BF16), vcb_ref[:, cols[h]], preferred_element_type=F32) for h in heads]
        den = [jnp.sum(p_ctx[h], axis=-1, keepdims=True) for h in heads]
        if latent:
            o = [o[h] + jnp.dot(p_loc[h].astype(BF16), vb_ref[pl.ds(start, span), cols[h]],
                                preferred_element_type=F32) for h in heads]
            den = [den[h] + jnp.sum(p_loc[h], axis=-1, keepdims=True) for h in heads]
        for h in heads:
            o_ref[:, cols[h]] = (o[h] / den[h]).astype(o_ref.dtype)

    pl.when(qb < n_lat)(functools.partial(attend, True))
    pl.when(qb >= n_lat)(functools.partial(attend, False))


def _na(pc, bias_tab, q_gain, k_gain, *, batch, with_ctx_out):
    t_lat, t_ctx, dh = SEQ, CTX_LEN, HEAD_DIM
    hb = NA_HEADS_PER_STEP
    w = hb * dh
    n_lat, n_ctx = t_lat // QBLK, t_ctx // QBLK
    n_q = n_lat + (n_ctx if with_ctx_out else 0)
    rows_out = batch * (t_lat + (t_ctx if with_ctx_out else 0))
    ctx0 = batch * t_lat // t_ctx
    span = NA_SLAB_ROWS * GRID_W
    qmap = functools.partial(_q_row_block, batch=batch, n_lat=n_lat, n_ctx=n_ctx)
    blocks = (_nbytes((QBLK, w), F32) + 2 * _nbytes((t_lat + t_ctx, w), F32) + _nbytes((hb, QBLK, span), F32)
              + _nbytes((QBLK, w), BF16))
    scratch = 2 * _nbytes((t_lat + t_ctx, w), BF16)
    return pl.pallas_call(
        functools.partial(_na_kernel, t_lat=t_lat, n_lat=n_lat),
        grid=(batch, NA_HEADS // hb, n_q),
        in_specs=[
            pl.BlockSpec((QBLK, w), lambda b, hg, qb: (qmap(b, qb), C_NQ // w + hg)),
            pl.BlockSpec((t_lat, w), lambda b, hg, qb: (b, C_NK // w + hg)),
            pl.BlockSpec((t_lat, w), lambda b, hg, qb: (b, C_NV // w + hg)),
            pl.BlockSpec((t_ctx, w), lambda b, hg, qb: (ctx0 + b, C_NK // w + hg)),
            pl.BlockSpec((t_ctx, w), lambda b, hg, qb: (ctx0 + b, C_NV // w + hg)),
            pl.BlockSpec((hb, None, QBLK, span), lambda b, hg, qb: (hg, _na_bias_variant(qb, n_lat), 0, 0)),
            pl.BlockSpec((1, dh), lambda b, hg, qb: (0, 0)),
            pl.BlockSpec((1, dh), lambda b, hg, qb: (0, 0)),
        ],
        out_specs=pl.BlockSpec((QBLK, w), lambda b, hg, qb: (qmap(b, qb), hg)),
        out_shape=jax.ShapeDtypeStruct((rows_out, NA_HEADS * dh), BF16),
        scratch_shapes=[pltpu.VMEM((t_lat, w), BF16), pltpu.VMEM((t_lat, w), BF16),
                        pltpu.VMEM((t_ctx, w), BF16), pltpu.VMEM((t_ctx, w), BF16)],
        compiler_params=pltpu.CompilerParams(
            dimension_semantics=("arbitrary", "arbitrary", "arbitrary"),
            vmem_limit_bytes=_vmem_limit(blocks, scratch, 12 << 20)),
        name="natten",
    )(pc, pc, pc, pc, pc, bias_tab, q_gain, k_gain)


def _rope_tables():
    quarter = HEAD_DIM // 4
    pos = jnp.arange(SEQ)
    rows = (pos // GRID_W).astype(F32)
    cols = (pos % GRID_W).astype(F32)
    inv = ROPE_BASE ** (-jnp.arange(quarter, dtype=F32) / quarter)
    ang_r = rows[:, None] * inv[None, :]
    ang_c = cols[:, None] * inv[None, :]
    cos = jnp.concatenate([jnp.cos(ang_r), jnp.cos(ang_r), jnp.cos(ang_c), jnp.cos(ang_c)], axis=-1)
    sin = jnp.concatenate([-jnp.sin(ang_r), jnp.sin(ang_r), -jnp.sin(ang_c), jnp.sin(ang_c)], axis=-1)
    cos = jnp.concatenate([cos, jnp.ones((CTX_LEN, HEAD_DIM), F32)], axis=0)
    sin = jnp.concatenate([sin, jnp.zeros((CTX_LEN, HEAD_DIM), F32)], axis=0)
    return cos, sin


def _na_bias_tables(rpb):
    n_layers, n_heads = rpb.shape[:2]
    rows = SEQ // GRID_W
    n_lat = SEQ // QBLK
    rq_per = QBLK // GRID_W
    reps = np.array([0, 1, 2, n_lat - 2, n_lat - 1])
    r = reps * rq_per
    start = np.minimum(np.clip(r - NA_KH // 2, 0, rows - NA_KH), rows - NA_SLAB_ROWS)
    rq = r[:, None] + np.arange(rq_per)[None, :]
    kr = start[:, None] + np.arange(NA_SLAB_ROWS)[None, :]
    r0 = np.clip(rq - NA_KH // 2, 0, rows - NA_KH)
    valid_r = (kr[:, None, :] >= r0[:, :, None]) & (kr[:, None, :] < r0[:, :, None] + NA_KH)
    dr = np.clip(kr[:, None, :] - rq[:, :, None] + NA_KH - 1, 0, 2 * NA_KH - 2)
    qc = np.arange(GRID_W)
    kc = np.arange(GRID_W)
    cs = np.clip(qc - NA_KW // 2, 0, GRID_W - NA_KW)
    valid_c = (kc[None, :] >= cs[:, None]) & (kc[None, :] < cs[:, None] + NA_KW)
    dc = np.clip(kc[None, :] - qc[:, None] + NA_KW - 1, 0, 2 * NA_KW - 2)
    pick_r = np.eye(2 * NA_KH - 1, dtype=np.float32)[dr.reshape(-1)]
    pick_c = np.eye(2 * NA_KW - 1, dtype=np.float32)[dc.reshape(-1)]
    t = jnp.einsum("nr,lhrc->lhnc", pick_r, rpb.astype(F32), precision=HIGHEST)
    t = jnp.einsum("lhnc,xc->lhnx", t, pick_c, precision=HIGHEST)
    t = t.reshape(n_layers, n_heads, len(reps), rq_per, NA_SLAB_ROWS, GRID_W, GRID_W)
    t = t.transpose(0, 1, 2, 3, 5, 4, 6)
    valid = valid_r[:, :, None, :, None] & valid_c[None, None, :, None, :]
    t = jnp.where(valid[None, None], t, NEG_INF)
    return t.reshape(n_layers, n_heads, len(reps), QBLK, NA_SLAB_ROWS * GRID_W)


SUBLANES = 8
CAST_K = 2048
CAST_CHUNK = 256


def _cast_cols_kernel(main_ref, next_ref, o_ref, *, shift):
    bn = o_ref.shape[1]
    for c in range(o_ref.shape[0] // CAST_CHUNK):
        cols = slice(c * CAST_CHUNK, (c + 1) * CAST_CHUNK)
        if shift == 0:
            x = main_ref[:, cols]
        else:
            x = jnp.concatenate([main_ref[shift:bn, cols], next_ref[:, cols]], axis=0)
        o_ref[cols, :] = x.T.astype(o_ref.dtype)


def _cast_cols(w_t, col0, n, *, bn):
    n_layers, _, k = w_t.shape
    shift = col0 % bn
    base = col0 - shift
    nxt = shift if shift else SUBLANES
    assert n % bn == 0 and k % CAST_K == 0 and nxt % SUBLANES == 0 and bn % nxt == 0 and base % nxt == 0
    blocks = _nbytes((bn + nxt, CAST_K), F32) + _nbytes((CAST_K, bn), BF16)
    return pl.pallas_call(
        functools.partial(_cast_cols_kernel, shift=shift),
        grid=(n_layers, k // CAST_K, n // bn),
        in_specs=[
            pl.BlockSpec((None, bn, CAST_K), lambda l, i, j: (l, base // bn + j, i)),
            pl.BlockSpec((None, nxt, CAST_K), lambda l, i, j: (l, (base + (j + 1) * bn) // nxt, i)),
        ],
        out_specs=pl.BlockSpec((None, CAST_K, bn), lambda l, i, j: (l, i, j)),
        out_shape=jax.ShapeDtypeStruct((n_layers, k, n), BF16),
        compiler_params=pltpu.CompilerParams(
            dimension_semantics=("arbitrary", "arbitrary", "arbitrary"),
            vmem_limit_bytes=_vmem_limit(blocks, 0, 4 << 20)),
        name="cast_cols",
    )(w_t, w_t)


def _pad_gate_up(gate_up):
    r = gate_up.shape[1]
    out = jnp.zeros((2, 128, gate_up.shape[2]), F32)
    out = out.at[0, 0:r].set(gate_up[0])
    out = out.at[1, r:2 * r].set(gate_up[1])
    return out


BM = 1024
BM_SMALL = 256
BM_NORM = 512
BN = 512
BN_FF = 512
BN_A = 640
BN_C = 768
BN_MERGE = 512


def kernel(x, c, ctx, c_ctx, ada_down, ada_up, ada_bias, norm_gain, ffn_w_in, ffn_w_out, w_in,
           gla_gate_up, gla_gate_bias, gla_norm, swa_q_norm, swa_k_norm, swa_sink, na_q_norm,
           na_k_norm, na_rpb, w_branch, w_out):
    batch, seq, d = x.shape
    depth = ada_down.shape[0]
    assert (seq, d, ctx.shape[1]) == (SEQ, D_MODEL, CTX_LEN) and batch + 1 <= MOD_ROWS
    assert w_in.shape[2] == C_START + C_QKV_COLS + GATE_COLS
    lat_rows = batch * seq
    all_rows = lat_rows + batch * ctx.shape[1]

    v8 = jnp.concatenate([c, c_ctx[None], jnp.zeros((MOD_ROWS - batch - 1, d), F32)], axis=0)
    mods = _adaln(v8, ada_down, ada_up, ada_bias)
    mods = mods.reshape(depth, N_MOD, MOD_ROWS, 1, d)
    cos_tab, sin_tab = _rope_tables()
    na_bias = _na_bias_tables(na_rpb)

    ffn_in_bf = ffn_w_in.astype(BF16).reshape(depth * 2, d, -1)
    ffn_out_bf = ffn_w_out.astype(BF16).reshape(depth * 2, -1, d)
    w_in_t = jnp.swapaxes(w_in, 1, 2)
    w_a_bf = _cast_cols(w_in_t, 0, A_COLS, bn=BN_A)
    w_c_bf = _cast_cols(w_in_t, C_START, C_QKV_COLS + GATE_COLS, bn=BN_C)
    w_branch_bf = w_branch.astype(BF16)
    w_out_bf = w_out.astype(BF16)

    h = jnp.concatenate([x.reshape(lat_rows, d), ctx.reshape(-1, d)], axis=0)
    bm = BM if all_rows % BM == 0 and lat_rows % BM == 0 else BM_SMALL
    mm = dict(seq=seq, bm=bm)
    norm = dict(seq=seq, bm=BM_NORM if all_rows % BM_NORM == 0 and lat_rows % BM_NORM == 0 else BM_SMALL)

    for l in range(depth):
        last = l == depth - 1
        rows_out = lat_rows if last else all_rows
        gain = norm_gain[l].reshape(3, 1, d)
        m = mods[l]

        xn = _modulate(h, all_rows, gain[0], m[0], m[1], **norm)
        g1 = _swiglu(xn, ffn_in_bf, 2 * l, all_rows, bm=bm, bn=BN_FF)
        h = _resid_matmul(g1, ffn_out_bf, 2 * l, h, all_rows, m[2], weight=MACARON_W, bn=BN, **mm)

        xn = _modulate(h, all_rows, gain[1], m[3], m[4], **norm)
        pa = _matmul(xn, w_a_bf, l, all_rows, 0, A_COLS, bm=bm, bn=BN_A, out_dtype=F32)
        pc = _matmul(xn, w_c_bf, l, all_rows, 0, C_QKV_COLS, bm=bm, bn=BN_C, out_dtype=F32)
        gates = _matmul(xn, w_c_bf, l, rows_out, C_QKV_COLS, GATE_COLS, bm=bm, bn=BN_C,
                        out_dtype=BF16, sigmoid=True)
        a_lat, a_ctx = _gla(pa, _pad_gate_up(gla_gate_up[l]), gla_gate_bias[l].reshape(2, 1, -1),
                            gla_norm[l].reshape(1, -1), batch=batch)
        o_a = a_lat if last else jnp.concatenate([a_lat, a_ctx], axis=0)
        o_b = _swa(pc, swa_sink[l], cos_tab, sin_tab, swa_q_norm[l].reshape(1, -1), swa_k_norm[l].reshape(1, -1),
                   batch=batch, with_ctx_out=not last)
        o_c = _na(pc, na_bias[l], na_q_norm[l].reshape(1, -1), na_k_norm[l].reshape(1, -1),
                  batch=batch, with_ctx_out=not last)
        y = _merge(o_a, o_b, o_c, w_branch_bf, l, gates, rows_out, bm=bm, bn=BN_MERGE)
        h = _resid_matmul(y, w_out_bf, l, h, rows_out, m[5], weight=1.0, bn=BN, **mm)

        xn = _modulate(h, rows_out, gain[2], m[6], m[7], **norm)
        g2 = _swiglu(xn, ffn_in_bf, 2 * l + 1, rows_out, bm=bm, bn=BN_FF)
        h = _resid_matmul(g2, ffn_out_bf, 2 * l + 1, h, rows_out, m[8], weight=MACARON_W, bn=BN, **mm)

    return h.reshape(batch, seq, d)
```

```python
import functools

import jax
import jax.numpy as jnp
import numpy as np
from jax import lax
from jax.experimental import pallas as pl
from jax.experimental.pallas import tpu as pltpu

F32 = jnp.float32
BF16 = jnp.bfloat16
HIGHEST = lax.Precision.HIGHEST

D_MODEL = 4096
SEQ = 2048
CTX_LEN = 256
GRID_W = 64
HEAD_DIM = 128
GLA_HEADS = 4
GLA_DK = 128
GLA_DV = 256
GLA_GATE_RANK = 16
GLA_GATE_NORM = 16.0
GLA_CHUNK = 64
SWA_HEADS = 8
SWA_KV_HEADS = 2
SWA_GROUP = SWA_HEADS // SWA_KV_HEADS
SWA_WINDOW = 128
NA_HEADS = 8
NA_KH = 8
NA_KW = 16
N_BRANCH = 3
BRANCH_W = 1024
D_FF = 4096
MACARON_W = 0.5
N_MOD = 9
ROPE_BASE = 10000.0
EPS = 1e-6
NEG_INF = -1e30
MOD_ROWS = 8

A_GQ, A_GK, A_GV, A_GO, A_GD = 0, 512, 1024, 2048, 3072
A_COLS = 3200
C_START = 3104
C_SQ, C_SK, C_SV = 0, 1024, 1280
C_NQ, C_NK, C_NV = 1536, 2560, 3584
C_QKV_COLS = 4608
GATE_COLS = N_BRANCH * D_MODEL

V7X_VMEM_BYTES = 64 * 1024 * 1024
V7X_VMEM_RESERVE = 6 * 1024 * 1024
QBLK = 128
NA_SLAB_ROWS = 10


def _vmem_limit(pipelined_bytes, scratch_bytes=0, temp_bytes=0):
    need = 2 * pipelined_bytes + scratch_bytes + temp_bytes + (4 << 20)
    return int(min(max(need, 16 << 20), V7X_VMEM_BYTES - V7X_VMEM_RESERVE))


def _nbytes(shape, dtype):
    return int(np.prod(shape)) * jnp.dtype(dtype).itemsize


def _adaln_kernel(v_ref, down_ref, up_ref, bias_ref, o_ref, t_ref):
    @pl.when(pl.program_id(1) == 0)
    def _():
        t_ref[...] = jnp.dot(jax.nn.silu(v_ref[...]), down_ref[...], preferred_element_type=F32,
                             precision=HIGHEST)

    o_ref[...] = jnp.dot(t_ref[...], up_ref[...], preferred_element_type=F32, precision=HIGHEST) + bias_ref[...]


def _adaln(v8, ada_down, ada_up, ada_bias):
    depth, d, rank = ada_down.shape
    blocks = (_nbytes((MOD_ROWS, d), F32) + _nbytes((d, rank), F32) + _nbytes((rank, d), F32)
              + _nbytes((1, d), F32) + _nbytes((MOD_ROWS, d), F32))
    return pl.pallas_call(
        _adaln_kernel,
        grid=(depth, N_MOD),
        in_specs=[
            pl.BlockSpec((MOD_ROWS, d), lambda l, j: (0, 0)),
            pl.BlockSpec((None, d, rank), lambda l, j: (l, 0, 0)),
            pl.BlockSpec((None, rank, d), lambda l, j: (l, 0, j)),
            pl.BlockSpec((None, 1, d), lambda l, j: (l, 0, j)),
        ],
        out_specs=pl.BlockSpec((None, None, MOD_ROWS, d), lambda l, j: (l, j, 0, 0)),
        out_shape=jax.ShapeDtypeStruct((depth, N_MOD, MOD_ROWS, d), F32),
        scratch_shapes=[pltpu.VMEM((MOD_ROWS, rank), F32)],
        compiler_params=pltpu.CompilerParams(
            dimension_semantics=("arbitrary", "arbitrary"), vmem_limit_bytes=_vmem_limit(blocks)),
        name="adaln",
    )(v8, ada_down, ada_up, ada_bias.reshape(depth, 1, -1))


def _mod_row_index(i, bm, seq):
    return jnp.minimum((i * bm) // seq, MOD_ROWS - 1)


def _mod_spec(bm, seq, d):
    return pl.BlockSpec((None, 1, d), lambda i, j: (_mod_row_index(i, bm, seq), 0, 0))


NORM_ROWS = 32


def _modulate_kernel(h_ref, g_ref, sh_ref, sc_ref, xn_ref, *, bm):
    gain = g_ref[...]
    scale1 = 1.0 + sc_ref[...]
    shift = sh_ref[...]

    def body(c, carry):
        r0 = pl.multiple_of(c * NORM_ROWS, NORM_ROWS)
        x = h_ref[pl.ds(r0, NORM_ROWS), :]
        y = x * lax.rsqrt(jnp.mean(x * x, axis=-1, keepdims=True) + EPS)
        xn_ref[pl.ds(r0, NORM_ROWS), :] = ((y * gain) * scale1 + shift).astype(BF16)
        return carry

    lax.fori_loop(0, bm // NORM_ROWS, body, 0)


def _modulate(h, rows, gain, shift, scale, *, seq, bm):
    d = h.shape[1]
    blocks = _nbytes((bm, d), F32) + _nbytes((bm, d), BF16) + 3 * _nbytes((1, d), F32)
    return pl.pallas_call(
        functools.partial(_modulate_kernel, bm=bm),
        grid=(rows // bm, 1),
        in_specs=[
            pl.BlockSpec((bm, d), lambda i, j: (i, 0)),
            pl.BlockSpec((1, d), lambda i, j: (0, 0)),
            _mod_spec(bm, seq, d),
            _mod_spec(bm, seq, d),
        ],
        out_specs=pl.BlockSpec((bm, d), lambda i, j: (i, 0)),
        out_shape=jax.ShapeDtypeStruct((rows, d), BF16),
        compiler_params=pltpu.CompilerParams(
            dimension_semantics=("arbitrary", "arbitrary"), vmem_limit_bytes=_vmem_limit(blocks)),
        name="modulate",
    )(h, gain, shift, scale)


SIDE_ROWS = 64


def _side_cast(side, n_steps, nj):
    src, sidx = side
    _, r, cdim = src.shape
    n_chunks = r // SIDE_ROWS
    assert r % SIDE_ROWS == 0 and n_chunks <= n_steps

    def chunk(i, j):
        return jnp.minimum(i * nj + j, n_chunks - 1)

    in_spec = pl.BlockSpec((None, SIDE_ROWS, cdim), lambda i, j: (sidx, chunk(i, j), 0))
    out_spec = pl.BlockSpec((None, SIDE_ROWS, cdim), lambda i, j: (0, chunk(i, j), 0))
    nbytes = _nbytes((SIDE_ROWS, cdim), F32) + _nbytes((SIDE_ROWS, cdim), BF16)
    return in_spec, out_spec, jax.ShapeDtypeStruct((1, r, cdim), BF16), nbytes


def _with_side(body, n_in, n_out):
    def kernel(*refs):
        if len(refs) == n_in + n_out:
            body(*refs)
        else:
            body(*refs[:n_in], *refs[n_in + 1:n_in + 1 + n_out])
            refs[-1][...] = refs[n_in][...].astype(refs[-1].dtype)
    return kernel


def _call_with_side(body, n_in, side, grid, in_specs, out_spec, out_shape, blocks, temp, name, args):
    out_specs, out_shapes = [out_spec], [out_shape]
    if side is not None:
        s_in, s_out, s_shape, s_bytes = _side_cast(side, grid[0] * grid[1], grid[1])
        in_specs, out_specs, out_shapes = in_specs + [s_in], out_specs + [s_out], out_shapes + [s_shape]
        blocks, args = blocks + s_bytes, args + (side[0],)
    res = pl.pallas_call(
        _with_side(body, n_in, 1),
        grid=grid, in_specs=in_specs, out_specs=out_specs, out_shape=out_shapes,
        compiler_params=pltpu.CompilerParams(
            dimension_semantics=("arbitrary", "arbitrary"), vmem_limit_bytes=_vmem_limit(blocks, 0, temp)),
        name=name,
    )(*args)
    return (res[0], res[1]) if side is not None else (res[0], None)


def _swiglu_kernel(x_ref, wa_ref, wb_ref, o_ref):
    x = x_ref[...]
    a = jnp.dot(x, wa_ref[...], preferred_element_type=F32)
    b = jnp.dot(x, wb_ref[...], preferred_element_type=F32)
    o_ref[...] = (jax.nn.silu(a) * b).astype(o_ref.dtype)


def _swiglu(x, w_in, widx, rows, *, bm, bn, side=None):
    d = x.shape[1]
    f = w_in.shape[2] // 2
    nb = f // bn
    blocks = _nbytes((bm, d), BF16) + 2 * _nbytes((d, bn), BF16) + _nbytes((bm, bn), BF16)
    in_specs = [
        pl.BlockSpec((bm, d), lambda i, j: (i, 0)),
        pl.BlockSpec((None, d, bn), lambda i, j: (widx, 0, j)),
        pl.BlockSpec((None, d, bn), lambda i, j: (widx, 0, j + nb)),
    ]
    return _call_with_side(_swiglu_kernel, 3, side, (rows // bm, nb), in_specs,
                           pl.BlockSpec((bm, bn), lambda i, j: (i, j)), jax.ShapeDtypeStruct((rows, f), BF16),
                           blocks, 4 * _nbytes((bm, bn), F32), "swiglu", (x, w_in, w_in))


def _mm_kernel(x_ref, w_ref, o_ref, *, sigmoid):
    y = jnp.dot(x_ref[...], w_ref[...], preferred_element_type=F32)
    if sigmoid:
        y = jax.nn.sigmoid(y)
    o_ref[...] = y.astype(o_ref.dtype)


def _matmul(x, w, widx, rows, col0, n, *, bm, bn, out_dtype, sigmoid=False):
    k = x.shape[1]
    jb0 = col0 // bn
    blocks = _nbytes((bm, k), BF16) + _nbytes((k, bn), BF16) + _nbytes((bm, bn), out_dtype)
    return pl.pallas_call(
        functools.partial(_mm_kernel, sigmoid=sigmoid),
        grid=(rows // bm, n // bn),
        in_specs=[
            pl.BlockSpec((bm, k), lambda i, j: (i, 0)),
            pl.BlockSpec((None, k, bn), lambda i, j: (widx, 0, jb0 + j)),
        ],
        out_specs=pl.BlockSpec((bm, bn), lambda i, j: (i, j)),
        out_shape=jax.ShapeDtypeStruct((rows, n), out_dtype),
        compiler_params=pltpu.CompilerParams(
            dimension_semantics=("arbitrary", "arbitrary"),
            vmem_limit_bytes=_vmem_limit(blocks, 0, 2 * _nbytes((bm, bn), F32))),
        name="matmul_sigmoid" if sigmoid else "matmul",
    )(x, w)


def _resid_mm_kernel(x_ref, w_ref, h_ref, gate_ref, o_ref, *, weight):
    y = jnp.dot(x_ref[...], w_ref[...], preferred_element_type=F32)
    if weight != 1.0:
        o_ref[...] = h_ref[...] + weight * gate_ref[...] * y
    else:
        o_ref[...] = h_ref[...] + gate_ref[...] * y


def _resid_matmul(x, w, widx, h, rows, gate, *, weight, seq, bm, bn, side=None):
    k = x.shape[1]
    n = w.shape[2]
    blocks = (_nbytes((bm, k), BF16) + _nbytes((k, bn), BF16) + 2 * _nbytes((bm, bn), F32)
              + _nbytes((1, bn), F32))
    in_specs = [
        pl.BlockSpec((bm, k), lambda i, j: (i, 0)),
        pl.BlockSpec((None, k, bn), lambda i, j: (widx, 0, j)),
        pl.BlockSpec((bm, bn), lambda i, j: (i, j)),
        pl.BlockSpec((None, 1, bn), lambda i, j: (_mod_row_index(i, bm, seq), 0, j)),
    ]
    return _call_with_side(functools.partial(_resid_mm_kernel, weight=weight), 4, side, (rows // bm, n // bn),
                           in_specs, pl.BlockSpec((bm, bn), lambda i, j: (i, j)),
                           jax.ShapeDtypeStruct((rows, n), F32), blocks, _nbytes((bm, bn), F32),
                           "resid_matmul", (x, w, h, gate))


def _merge_kernel(oa_ref, ob_ref, oc_ref, w_ref, ga_ref, gb_ref, gc_ref, y_ref):
    y = ga_ref[...].astype(F32) * jnp.dot(oa_ref[...], w_ref[0], preferred_element_type=F32)
    y = y + gb_ref[...].astype(F32) * jnp.dot(ob_ref[...], w_ref[1], preferred_element_type=F32)
    y = y + gc_ref[...].astype(F32) * jnp.dot(oc_ref[...], w_ref[2], preferred_element_type=F32)
    y_ref[...] = y.astype(y_ref.dtype)


def _merge(oa, ob, oc, w_branch, widx, gates, rows, *, bm, bn, side=None):
    kb = oa.shape[1]
    d = w_branch.shape[3]
    per_branch = d // bn
    blocks = (3 * _nbytes((bm, kb), BF16) + _nbytes((N_BRANCH, kb, bn), BF16) + 4 * _nbytes((bm, bn), BF16))
    o_spec = pl.BlockSpec((bm, kb), lambda i, j: (i, 0))

    def gate_spec(br):
        return pl.BlockSpec((bm, bn), lambda i, j: (i, br * per_branch + j))

    in_specs = [o_spec, o_spec, o_spec,
                pl.BlockSpec((None, N_BRANCH, kb, bn), lambda i, j: (widx, 0, 0, j)),
                gate_spec(0), gate_spec(1), gate_spec(2)]
    return _call_with_side(_merge_kernel, 7, side, (rows // bm, d // bn), in_specs,
                           pl.BlockSpec((bm, bn), lambda i, j: (i, j)), jax.ShapeDtypeStruct((rows, d), BF16),
                           blocks, 4 * _nbytes((bm, bn), F32), "merge",
                           (oa, ob, oc, w_branch, gates, gates, gates))


GLA_BLOCK = 256
GLA_UNROLL = 4
GLA_BLOCKS_PER_STEP = 2
_NT = (((1,), (1,)), ((), ()))
_TN = (((0,), (0,)), ((), ()))


def _split_bf16(x):
    hi = x.astype(BF16)
    lo = (x - hi.astype(F32)).astype(BF16)
    return hi, lo


def _sum_dot(op, parts):
    acc = jnp.dot(op, parts[0], preferred_element_type=F32)
    for p in parts[1:]:
        acc = acc + jnp.dot(op, p, preferred_element_type=F32)
    return acc


def _gla_kernel(ql_ref, kl_ref, vl_ref, ogl_ref, gdl_ref, qc_ref, kc_ref, vc_ref, ogc_ref, gdc_ref,
                up_ref, gb_ref, gn_ref, al_ref, ac_ref,
                qe_ref, edec_ref, of_ref, ob_ref, kv_ref, st_ref, *, t_lat, t_ctx):
    c = GLA_CHUNK
    br = GLA_BLOCK
    cpb = br // c
    row = lax.broadcasted_iota(jnp.int32, (br, br), 0)
    col = lax.broadcasted_iota(jnp.int32, (br, br), 1)
    same = (row // c) == (col // c)
    keep = (same & (col <= row), same & (col >= row))
    tri = (keep[0].astype(BF16), keep[1].astype(BF16))
    chunk_of_row = lax.broadcasted_iota(jnp.int32, (br, GLA_DK), 0) // c
    q_scale = GLA_DK ** -0.5
    gain = gn_ref[...]
    o_refs = (of_ref, ob_ref)

    st_ref[...] = jnp.zeros_like(st_ref)

    def segment(q_ref, k_ref, v_ref, og_ref, gd_ref, out_ref, t):
        nc = t // c

        per_step = min(GLA_BLOCKS_PER_STEP, t // br)

        def block_body(r, carry):
            blocks = [r * per_step + i for i in range(per_step)]
            rows = [pl.ds(pl.multiple_of(b * br, br), br) for b in blocks]
            chains = [(i, d) for i in range(per_step) for d in range(2)]
            gd = [gd_ref[rw, :] for rw in rows]
            q = [q_ref[rw, :] * q_scale for rw in rows]
            k = [k_ref[rw, :] for rw in rows]
            v = [v_ref[rw, :].astype(BF16) for rw in rows]
            z = [jnp.dot(gd[i], up_ref[d], preferred_element_type=F32) + gb_ref[d] for i, d in chains]
            parts = [_split_bf16(jax.nn.log_sigmoid(zz) * (1.0 / GLA_GATE_NORM)) for zz in z]
            cum = [_sum_dot(tri[d], p) for (i, d), p in zip(chains, parts)]
            cum_end = []
            for (i, d), cm in zip(chains, cum):
                ends = [cm[j * c + c - 1:j * c + c] if d == 0 else cm[j * c:j * c + 1] for j in range(cpb)]
                cum_end.append(jnp.concatenate([jnp.broadcast_to(e, (c, e.shape[1])) for e in ends], axis=0))
            qe = [(q[i] * jnp.exp(cm)).astype(BF16) for (i, d), cm in zip(chains, cum)]
            kinv = [(k[i] * jnp.exp(-cm)).astype(BF16) for (i, d), cm in zip(chains, cum)]
            kdec = [(k[i] * jnp.exp(ce - cm)).astype(BF16) for (i, d), cm, ce in zip(chains, cum, cum_end)]
            a = [lax.dot_general(qq, kk, _NT, preferred_element_type=F32) for qq, kk in zip(qe, kinv)]
            kdec_by_chunk = [jnp.concatenate(
                [jnp.where(chunk_of_row == j, kd, jnp.zeros_like(kd)) for j in range(cpb)], axis=1) for kd in kdec]
            kv = [lax.dot_general(v[i], kd, _TN, preferred_element_type=F32)
                  for (i, d), kd in zip(chains, kdec_by_chunk)]
            a = [jnp.where(keep[d], aa, 0.0).astype(BF16) for (i, d), aa in zip(chains, a)]
            o = [jnp.dot(aa, v[i], preferred_element_type=F32) for (i, d), aa in zip(chains, a)]
            for n, (i, d) in enumerate(chains):
                qe_ref[d, rows[i], :] = qe[n]
                edec_ref[d, rows[i], :] = jnp.exp(cum_end[n])
                o_refs[d][rows[i], :] = o[n]
                for j in range(cpb):
                    kv_ref[d, blocks[i] * cpb + j] = kv[n][:, j * GLA_DK:(j + 1) * GLA_DK]
            return carry

        lax.fori_loop(0, t // (br * per_step), block_body, 0)

        def scan_body(n, carry):
            for d in range(2):
                m = n if d == 0 else nc - 1 - n
                rows = pl.ds(pl.multiple_of(m * c, c), c)
                st = st_ref[d]
                o_refs[d][rows, :] += lax.dot_general(qe_ref[d, rows, :], st.astype(BF16), _NT,
                                                      preferred_element_type=F32)
                st_ref[d] = st * edec_ref[d, pl.ds(pl.multiple_of(m * c, c), 1), :] + kv_ref[d, m]
            return carry

        lax.fori_loop(0, nc, scan_body, 0, unroll=GLA_UNROLL)

        def out_body(n, carry):
            rows = pl.ds(pl.multiple_of(n * c, c), c)
            o = of_ref[rows, :] + ob_ref[rows, :]
            y = o * lax.rsqrt(jnp.mean(o * o, axis=-1, keepdims=True) + EPS)
            y = (y * gain) * jax.nn.silu(og_ref[rows, :])
            out_ref[rows, :] = y.astype(out_ref.dtype)
            return carry

        lax.fori_loop(0, nc, out_body, 0, unroll=GLA_UNROLL)

    segment(qc_ref, kc_ref, vc_ref, ogc_ref, gdc_ref, ac_ref, t_ctx)
    segment(ql_ref, kl_ref, vl_ref, ogl_ref, gdl_ref, al_ref, t_lat)


def _gla(pa, gate_up_pad, gate_bias, gla_norm, *, batch):
    t_lat, t_ctx = SEQ, CTX_LEN
    ctx0 = batch * t_lat // t_ctx
    dk, dv = GLA_DK, GLA_DV
    nc = t_lat // GLA_CHUNK

    def lat(width, col0):
        return pl.BlockSpec((t_lat, width), lambda b, h: (b, col0 // width + h))

    def ctx(width, col0):
        return pl.BlockSpec((t_ctx, width), lambda b, h: (ctx0 + b, col0 // width + h))

    lat_gd = pl.BlockSpec((t_lat, 128), lambda b, h: (b, A_GD // 128))
    ctx_gd = pl.BlockSpec((t_ctx, 128), lambda b, h: (ctx0 + b, A_GD // 128))
    blocks = ((t_lat + t_ctx) * (3 * dk + 2 * dv) * 4 + _nbytes((2, 128, dk), F32)
              + (t_lat + t_ctx) * dv * 2)
    scratch_shapes = [pltpu.VMEM((2, t_lat, dk), BF16), pltpu.VMEM((2, t_lat, dk), F32),
                      pltpu.VMEM((t_lat, dv), F32), pltpu.VMEM((t_lat, dv), F32),
                      pltpu.VMEM((2, nc, dv, dk), F32), pltpu.VMEM((2, dv, dk), F32)]
    scratch = (_nbytes((2, t_lat, dk), BF16) + _nbytes((2, t_lat, dk), F32) + 2 * _nbytes((t_lat, dv), F32)
               + _nbytes((2, nc + 1, dv, dk), F32))
    return pl.pallas_call(
        functools.partial(_gla_kernel, t_lat=t_lat, t_ctx=t_ctx),
        grid=(batch, GLA_HEADS),
        in_specs=[lat(dk, A_GQ), lat(dk, A_GK), lat(dv, A_GV), lat(dv, A_GO), lat_gd,
                  ctx(dk, A_GQ), ctx(dk, A_GK), ctx(dv, A_GV), ctx(dv, A_GO), ctx_gd,
                  pl.BlockSpec((2, 128, dk), lambda b, h: (0, 0, h)),
                  pl.BlockSpec((2, 1, dk), lambda b, h: (0, 0, h)),
                  pl.BlockSpec((1, dv), lambda b, h: (0, 0))],
        out_specs=[pl.BlockSpec((t_lat, dv), lambda b, h: (b, h)),
                   pl.BlockSpec((t_ctx, dv), lambda b, h: (b, h))],
        out_shape=[jax.ShapeDtypeStruct((batch * t_lat, GLA_HEADS * dv), BF16),
                   jax.ShapeDtypeStruct((batch * t_ctx, GLA_HEADS * dv), BF16)],
        scratch_shapes=scratch_shapes,
        compiler_params=pltpu.CompilerParams(
            dimension_semantics=("arbitrary", "arbitrary"),
            vmem_limit_bytes=_vmem_limit(blocks, scratch, 6 << 20)),
        name="gla",
    )(pa, pa, pa, pa, pa, pa, pa, pa, pa, pa, gate_up_pad, gate_bias, gla_norm)


def _rms(x, gain):
    return (x * lax.rsqrt(jnp.mean(x * x, axis=-1, keepdims=True) + EPS)) * gain


def _rope(x, cos, sin_signed):
    lane = lax.broadcasted_iota(jnp.int32, x.shape, 1)
    partner = jnp.where((lane % 64) < 32, pltpu.roll(x, 96, 1), pltpu.roll(x, 32, 1))
    return x * cos + partner * sin_signed


def _q_row_block(b, qb, *, batch, n_lat, n_ctx):
    return jnp.where(qb < n_lat, b * n_lat + qb, batch * n_lat + b * n_ctx + (qb - n_lat))


KV_PREP_ROWS = 256


def _swa_window_masks(t_lat):
    span = 3 * QBLK
    q_off = np.array([0, QBLK, 2 * QBLK])[:, None, None] + np.arange(QBLK)[None, :, None]
    k_off = np.arange(span)[None, None, :]
    return jnp.asarray(np.where(np.abs(k_off - q_off) <= SWA_WINDOW, 0.0, NEG_INF), F32)


def _swa_mask_variant(qb, n_lat):
    return jnp.where(qb == 0, 0, jnp.where(qb >= n_lat - 1, 2, 1))


def _swa_kernel(sink_ref, q_ref, kl_ref, vl_ref, kc_ref, vc_ref, cq_ref, sq_ref, ck_ref, sk_ref,
                qg_ref, kg_ref, mask_ref, o_ref, kn_ref, vb_ref, kcn_ref, vcb_ref, *, t_lat, n_lat):
    qb = pl.program_id(1)
    g = SWA_GROUP
    dh = HEAD_DIM
    scale = dh ** -0.5

    @pl.when(qb == 0)
    def _prep():
        k_gain = kg_ref[...]

        def body(r, carry):
            r0 = pl.multiple_of(r * KV_PREP_ROWS, KV_PREP_ROWS)
            rows = pl.ds(r0, KV_PREP_ROWS)
            cos, sin = ck_ref[rows, :], sk_ref[rows, :]
            for kv in range(SWA_KV_HEADS):
                cols = slice(kv * dh, (kv + 1) * dh)
                kn_ref[rows, cols] = _rope(_rms(kl_ref[rows, cols], k_gain), cos, sin).astype(BF16)
            vb_ref[rows, :] = vl_ref[rows, :].astype(BF16)
            return carry

        lax.fori_loop(0, t_lat // KV_PREP_ROWS, body, 0)
        for kv in range(SWA_KV_HEADS):
            cols = slice(kv * dh, (kv + 1) * dh)
            kcn_ref[:, cols] = _rms(kc_ref[:, cols], k_gain).astype(BF16)
        vcb_ref[...] = vc_ref[...].astype(BF16)

    def attend(latent):
        cos, sin = cq_ref[...], sq_ref[...]
        q_gain = qg_ref[...]
        span = 3 * QBLK
        start = pl.multiple_of(jnp.clip((qb - 1) * QBLK, 0, t_lat - span), QBLK)
        kvs = range(SWA_KV_HEADS)
        cols = [slice(kv * dh, (kv + 1) * dh) for kv in kvs]
        qs = []
        for kv in kvs:
            heads = [_rope(_rms(q_ref[:, (kv * g + i) * dh:(kv * g + i + 1) * dh], q_gain), cos, sin).astype(BF16)
                     for i in range(g)]
            qs.append(jnp.concatenate(heads, axis=0))
        s_ctx = [lax.dot_general(qs[kv], kcn_ref[:, cols[kv]], _NT, preferred_element_type=F32) for kv in kvs]
        if latent:
            s_loc = [lax.dot_general(qs[kv], kn_ref[pl.ds(start, span), cols[kv]], _NT,
                                     preferred_element_type=F32) for kv in kvs]
        p_loc, p_ctx, dens = ([[] for _ in kvs] for _ in range(3))
        for i in range(g):
            rows = slice(i * QBLK, (i + 1) * QBLK)
            for kv in kvs:
                sink = sink_ref[kv * g + i]
                sc = s_ctx[kv][rows] * scale
                m = jnp.maximum(jnp.max(sc, axis=-1, keepdims=True), sink)
                if latent:
                    sl = s_loc[kv][rows] * scale + mask_ref[...]
                    m = jnp.maximum(m, jnp.max(sl, axis=-1, keepdims=True))
                    pl_i = jnp.exp(sl - m)
                    p_loc[kv].append(pl_i.astype(BF16))
                pc_i = jnp.exp(sc - m)
                p_ctx[kv].append(pc_i.astype(BF16))
                den = jnp.sum(pc_i, axis=-1, keepdims=True) + jnp.exp(sink - m)
                dens[kv].append(den + jnp.sum(pl_i, axis=-1, keepdims=True) if latent else den)
        o = [jnp.dot(jnp.concatenate(p_ctx[kv], axis=0), vcb_ref[:, cols[kv]], preferred_element_type=F32)
             for kv in kvs]
        if latent:
            o = [o[kv] + jnp.dot(jnp.concatenate(p_loc[kv], axis=0), vb_ref[pl.ds(start, span), cols[kv]],
                                 preferred_element_type=F32) for kv in kvs]
        for kv in kvs:
            for i in range(g):
                o_i = o[kv][i * QBLK:(i + 1) * QBLK, :] / dens[kv][i]
                o_ref[:, (kv * g + i) * dh:(kv * g + i + 1) * dh] = o_i.astype(o_ref.dtype)

    pl.when(qb < n_lat)(functools.partial(attend, True))
    pl.when(qb >= n_lat)(functools.partial(attend, False))


def _swa(pc, sink, cos_tab, sin_tab, q_gain, k_gain, *, batch, with_ctx_out):
    t_lat, t_ctx, dh = SEQ, CTX_LEN, HEAD_DIM
    qw, kw = SWA_HEADS * dh, SWA_KV_HEADS * dh
    n_lat, n_ctx = t_lat // QBLK, t_ctx // QBLK
    n_q = n_lat + (n_ctx if with_ctx_out else 0)
    rows_out = batch * (t_lat + (t_ctx if with_ctx_out else 0))
    ctx0 = batch * t_lat // t_ctx
    qmap = functools.partial(_q_row_block, batch=batch, n_lat=n_lat, n_ctx=n_ctx)
    blocks = (_nbytes((QBLK, qw), F32) + 2 * _nbytes((t_lat + t_ctx, kw), F32) + 2 * _nbytes((QBLK, dh), F32)
              + 2 * _nbytes((t_lat, dh), F32) + _nbytes((QBLK, qw), BF16))
    scratch = 2 * _nbytes((t_lat + t_ctx, kw), BF16)
    return pl.pallas_call(
        functools.partial(_swa_kernel, t_lat=t_lat, n_lat=n_lat),
        grid=(batch, n_q),
        in_specs=[
            pl.BlockSpec(memory_space=pltpu.SMEM),
            pl.BlockSpec((QBLK, qw), lambda b, qb: (qmap(b, qb), C_SQ // qw)),
            pl.BlockSpec((t_lat, kw), lambda b, qb: (b, C_SK // kw)),
            pl.BlockSpec((t_lat, kw), lambda b, qb: (b, C_SV // kw)),
            pl.BlockSpec((t_ctx, kw), lambda b, qb: (ctx0 + b, C_SK // kw)),
            pl.BlockSpec((t_ctx, kw), lambda b, qb: (ctx0 + b, C_SV // kw)),
            pl.BlockSpec((QBLK, dh), lambda b, qb: (qb, 0)),
            pl.BlockSpec((QBLK, dh), lambda b, qb: (qb, 0)),
            pl.BlockSpec((t_lat, dh), lambda b, qb: (0, 0)),
            pl.BlockSpec((t_lat, dh), lambda b, qb: (0, 0)),
            pl.BlockSpec((1, dh), lambda b, qb: (0, 0)),
            pl.BlockSpec((1, dh), lambda b, qb: (0, 0)),
            pl.BlockSpec((None, QBLK, 3 * QBLK), lambda b, qb: (_swa_mask_variant(qb, n_lat), 0, 0)),
        ],
        out_specs=pl.BlockSpec((QBLK, qw), lambda b, qb: (qmap(b, qb), 0)),
        out_shape=jax.ShapeDtypeStruct((rows_out, qw), BF16),
        scratch_shapes=[pltpu.VMEM((t_lat, kw), BF16), pltpu.VMEM((t_lat, kw), BF16),
                        pltpu.VMEM((t_ctx, kw), BF16), pltpu.VMEM((t_ctx, kw), BF16)],
        compiler_params=pltpu.CompilerParams(
            dimension_semantics=("arbitrary", "arbitrary"),
            vmem_limit_bytes=_vmem_limit(blocks, scratch, 12 << 20)),
        name="swa",
    )(sink, pc, pc, pc, pc, pc, cos_tab, sin_tab, cos_tab, sin_tab, q_gain, k_gain, _swa_window_masks(t_lat))


NA_HEADS_PER_STEP = 4


def _na_slab_start(qb):
    r = qb * (QBLK // GRID_W)
    rows = SEQ // GRID_W
    return jnp.minimum(jnp.clip(r - NA_KH // 2, 0, rows - NA_KH), rows - NA_SLAB_ROWS)


def _na_bias_variant(qb, n_lat):
    return jnp.where(qb < 2, qb, jnp.where(qb < n_lat - 2, 2, jnp.minimum(qb, n_lat - 1) - (n_lat - 5)))


def _na_kernel(q_ref, kl_ref, vl_ref, kc_ref, vc_ref, bias_ref, qg_ref, kg_ref, o_ref,
               kn_ref, vb_ref, kcn_ref, vcb_ref, *, t_lat, n_lat):
    qb = pl.program_id(2)
    dh = HEAD_DIM
    scale = dh ** -0.5

    @pl.when(qb == 0)
    def _prep():
        k_gain = kg_ref[...]

        def body(r, carry):
            r0 = pl.multiple_of(r * KV_PREP_ROWS, KV_PREP_ROWS)
            rows = pl.ds(r0, KV_PREP_ROWS)
            for h in range(NA_HEADS_PER_STEP):
                cols = slice(h * dh, (h + 1) * dh)
                kn_ref[rows, cols] = _rms(kl_ref[rows, cols], k_gain).astype(BF16)
            vb_ref[rows, :] = vl_ref[rows, :].astype(BF16)
            return carry

        lax.fori_loop(0, t_lat // KV_PREP_ROWS, body, 0)
        for h in range(NA_HEADS_PER_STEP):
            cols = slice(h * dh, (h + 1) * dh)
            kcn_ref[:, cols] = _rms(kc_ref[:, cols], k_gain).astype(BF16)
        vcb_ref[...] = vc_ref[...].astype(BF16)

    def attend(latent):
        q_gain = qg_ref[...]
        heads = range(NA_HEADS_PER_STEP)
        cols = [slice(h * dh, (h + 1) * dh) for h in heads]
        span = NA_SLAB_ROWS * GRID_W
        start = pl.multiple_of(_na_slab_start(qb) * GRID_W, GRID_W)
        qn = [_rms(q_ref[:, cols[h]], q_gain).astype(BF16) for h in heads]
        s_ctx = [lax.dot_general(qn[h], kcn_ref[:, cols[h]], _NT, preferred_element_type=F32) * scale
                 for h in heads]
        m = [jnp.max(s, axis=-1, keepdims=True) for s in s_ctx]
        if latent:
            s_loc = [lax.dot_general(qn[h], kn_ref[pl.ds(start, span), cols[h]], _NT,
                                     preferred_element_type=F32) for h in heads]
            s_loc = [s_loc[h] * scale + bias_ref[h] for h in heads]
            m = [jnp.maximum(m[h], jnp.max(s_loc[h], axis=-1, keepdims=True)) for h in heads]
            p_loc = [jnp.exp(s_loc[h] - m[h]) for h in heads]
        p_ctx = [jnp.exp(s_ctx[h] - m[h]) for h in heads]
        o = [jnp.dot(p_ctx[h].astype(BF16), vcb_ref[:, cols[h]], preferred_element_type=F32) for h in heads]
        den = [jnp.sum(p_ctx[h], axis=-1, keepdims=True) for h in heads]
        if latent:
            o = [o[h] + jnp.dot(p_loc[h].astype(BF16), vb_ref[pl.ds(start, span), cols[h]],
                                preferred_element_type=F32) for h in heads]
            den = [den[h] + jnp.sum(p_loc[h], axis=-1, keepdims=True) for h in heads]
        for h in heads:
            o_ref[:, cols[h]] = (o[h] / den[h]).astype(o_ref.dtype)

    pl.when(qb < n_lat)(functools.partial(attend, True))
    pl.when(qb >= n_lat)(functools.partial(attend, False))


def _na(pc, bias_tab, q_gain, k_gain, *, batch, with_ctx_out):
    t_lat, t_ctx, dh = SEQ, CTX_LEN, HEAD_DIM
    hb = NA_HEADS_PER_STEP
    w = hb * dh
    n_lat, n_ctx = t_lat // QBLK, t_ctx // QBLK
    n_q = n_lat + (n_ctx if with_ctx_out else 0)
    rows_out = batch * (t_lat + (t_ctx if with_ctx_out else 0))
    ctx0 = batch * t_lat // t_ctx
    span = NA_SLAB_ROWS * GRID_W
    qmap = functools.partial(_q_row_block, batch=batch, n_lat=n_lat, n_ctx=n_ctx)
    blocks = (_nbytes((QBLK, w), F32) + 2 * _nbytes((t_lat + t_ctx, w), F32) + _nbytes((hb, QBLK, span), F32)
              + _nbytes((QBLK, w), BF16))
    scratch = 2 * _nbytes((t_lat + t_ctx, w), BF16)
    return pl.pallas_call(
        functools.partial(_na_kernel, t_lat=t_lat, n_lat=n_lat),
        grid=(batch, NA_HEADS // hb, n_q),
        in_specs=[
            pl.BlockSpec((QBLK, w), lambda b, hg, qb: (qmap(b, qb), C_NQ // w + hg)),
            pl.BlockSpec((t_lat, w), lambda b, hg, qb: (b, C_NK // w + hg)),
            pl.BlockSpec((t_lat, w), lambda b, hg, qb: (b, C_NV // w + hg)),
            pl.BlockSpec((t_ctx, w), lambda b, hg, qb: (ctx0 + b, C_NK // w + hg)),
            pl.BlockSpec((t_ctx, w), lambda b, hg, qb: (ctx0 + b, C_NV // w + hg)),
            pl.BlockSpec((hb, None, QBLK, span), lambda b, hg, qb: (hg, _na_bias_variant(qb, n_lat), 0, 0)),
            pl.BlockSpec((1, dh), lambda b, hg, qb: (0, 0)),
            pl.BlockSpec((1, dh), lambda b, hg, qb: (0, 0)),
        ],
        out_specs=pl.BlockSpec((QBLK, w), lambda b, hg, qb: (qmap(b, qb), hg)),
        out_shape=jax.ShapeDtypeStruct((rows_out, NA_HEADS * dh), BF16),
        scratch_shapes=[pltpu.VMEM((t_lat, w), BF16), pltpu.VMEM((t_lat, w), BF16),
                        pltpu.VMEM((t_ctx, w), BF16), pltpu.VMEM((t_ctx, w), BF16)],
        compiler_params=pltpu.CompilerParams(
            dimension_semantics=("arbitrary", "arbitrary", "arbitrary"),
            vmem_limit_bytes=_vmem_limit(blocks, scratch, 12 << 20)),
        name="natten",
    )(pc, pc, pc, pc, pc, bias_tab, q_gain, k_gain)


def _rope_tables():
    quarter = HEAD_DIM // 4
    pos = jnp.arange(SEQ)
    rows = (pos // GRID_W).astype(F32)
    cols = (pos % GRID_W).astype(F32)
    inv = ROPE_BASE ** (-jnp.arange(quarter, dtype=F32) / quarter)
    ang_r = rows[:, None] * inv[None, :]
    ang_c = cols[:, None] * inv[None, :]
    cos = jnp.concatenate([jnp.cos(ang_r), jnp.cos(ang_r), jnp.cos(ang_c), jnp.cos(ang_c)], axis=-1)
    sin = jnp.concatenate([-jnp.sin(ang_r), jnp.sin(ang_r), -jnp.sin(ang_c), jnp.sin(ang_c)], axis=-1)
    cos = jnp.concatenate([cos, jnp.ones((CTX_LEN, HEAD_DIM), F32)], axis=0)
    sin = jnp.concatenate([sin, jnp.zeros((CTX_LEN, HEAD_DIM), F32)], axis=0)
    return cos, sin


def _na_bias_tables(rpb):
    n_layers, n_heads = rpb.shape[:2]
    rows = SEQ // GRID_W
    n_lat = SEQ // QBLK
    rq_per = QBLK // GRID_W
    reps = np.array([0, 1, 2, n_lat - 2, n_lat - 1])
    r = reps * rq_per
    start = np.minimum(np.clip(r - NA_KH // 2, 0, rows - NA_KH), rows - NA_SLAB_ROWS)
    rq = r[:, None] + np.arange(rq_per)[None, :]
    kr = start[:, None] + np.arange(NA_SLAB_ROWS)[None, :]
    r0 = np.clip(rq - NA_KH // 2, 0, rows - NA_KH)
    valid_r = (kr[:, None, :] >= r0[:, :, None]) & (kr[:, None, :] < r0[:, :, None] + NA_KH)
    dr = np.clip(kr[:, None, :] - rq[:, :, None] + NA_KH - 1, 0, 2 * NA_KH - 2)
    qc = np.arange(GRID_W)
    kc = np.arange(GRID_W)
    cs = np.clip(qc - NA_KW // 2, 0, GRID_W - NA_KW)
    valid_c = (kc[None, :] >= cs[:, None]) & (kc[None, :] < cs[:, None] + NA_KW)
    dc = np.clip(kc[None, :] - qc[:, None] + NA_KW - 1, 0, 2 * NA_KW - 2)
    pick_r = np.eye(2 * NA_KH - 1, dtype=np.float32)[dr.reshape(-1)]
    pick_c = np.eye(2 * NA_KW - 1, dtype=np.float32)[dc.reshape(-1)]
    t = jnp.einsum("nr,lhrc->lhnc", pick_r, rpb.astype(F32), precision=HIGHEST)
    t = jnp.einsum("lhnc,xc->lhnx", t, pick_c, precision=HIGHEST)
    t = t.reshape(n_layers, n_heads, len(reps), rq_per, NA_SLAB_ROWS, GRID_W, GRID_W)
    t = t.transpose(0, 1, 2, 3, 5, 4, 6)
    valid = valid_r[:, :, None, :, None] & valid_c[None, None, :, None, :]
    t = jnp.where(valid[None, None], t, NEG_INF)
    return t.reshape(n_layers, n_heads, len(reps), QBLK, NA_SLAB_ROWS * GRID_W)


SUBLANES = 8
CAST_K = 2048
CAST_CHUNK = 256


def _cast_cols_kernel(main_ref, next_ref, o_ref, *, shift):
    bn = o_ref.shape[1]
    for c in range(o_ref.shape[0] // CAST_CHUNK):
        cols = slice(c * CAST_CHUNK, (c + 1) * CAST_CHUNK)
        if shift == 0:
            x = main_ref[:, cols]
        else:
            x = jnp.concatenate([main_ref[shift:bn, cols], next_ref[:, cols]], axis=0)
        o_ref[cols, :] = x.T.astype(o_ref.dtype)


def _cast_cols(w_t, col0, n, *, bn):
    n_layers, _, k = w_t.shape
    shift = col0 % bn
    base = col0 - shift
    nxt = shift if shift else SUBLANES
    assert n % bn == 0 and k % CAST_K == 0 and nxt % SUBLANES == 0 and bn % nxt == 0 and base % nxt == 0
    blocks = _nbytes((bn + nxt, CAST_K), F32) + _nbytes((CAST_K, bn), BF16)
    return pl.pallas_call(
        functools.partial(_cast_cols_kernel, shift=shift),
        grid=(n_layers, k // CAST_K, n // bn),
        in_specs=[
            pl.BlockSpec((None, bn, CAST_K), lambda l, i, j: (l, base // bn + j, i)),
            pl.BlockSpec((None, nxt, CAST_K), lambda l, i, j: (l, (base + (j + 1) * bn) // nxt, i)),
        ],
        out_specs=pl.BlockSpec((None, CAST_K, bn), lambda l, i, j: (l, i, j)),
        out_shape=jax.ShapeDtypeStruct((n_layers, k, n), BF16),
        compiler_params=pltpu.CompilerParams(
            dimension_semantics=("arbitrary", "arbitrary", "arbitrary"),
            vmem_limit_bytes=_vmem_limit(blocks, 0, 4 << 20)),
        name="cast_cols",
    )(w_t, w_t)


def _pad_gate_up(gate_up):
    r = gate_up.shape[1]
    out = jnp.zeros((2, 128, gate_up.shape[2]), F32)
    out = out.at[0, 0:r].set(gate_up[0])
    out = out.at[1, r:2 * r].set(gate_up[1])
    return out


BM = 1024
BM_SMALL = 256
BM_NORM = 512
BN = 512
BN_FF = 512
BN_A = 640
BN_C = 768
BN_MERGE = 512


def kernel(x, c, ctx, c_ctx, ada_down, ada_up, ada_bias, norm_gain, ffn_w_in, ffn_w_out, w_in,
           gla_gate_up, gla_gate_bias, gla_norm, swa_q_norm, swa_k_norm, swa_sink, na_q_norm,
           na_k_norm, na_rpb, w_branch, w_out):
    batch, seq, d = x.shape
    depth = ada_down.shape[0]
    assert (seq, d, ctx.shape[1]) == (SEQ, D_MODEL, CTX_LEN) and batch + 1 <= MOD_ROWS
    assert w_in.shape[2] == C_START + C_QKV_COLS + GATE_COLS
    lat_rows = batch * seq
    all_rows = lat_rows + batch * ctx.shape[1]

    v8 = jnp.concatenate([c, c_ctx[None], jnp.zeros((MOD_ROWS - batch - 1, d), F32)], axis=0)
    mods = _adaln(v8, ada_down, ada_up, ada_bias)
    mods = mods.reshape(depth, N_MOD, MOD_ROWS, 1, d)
    cos_tab, sin_tab = _rope_tables()
    na_bias = _na_bias_tables(na_rpb)

    ffn_in_f = ffn_w_in.reshape(depth * 2, d, -1)
    ffn_out_f = ffn_w_out.reshape(depth * 2, -1, d)
    w_branch_f = w_branch.reshape(depth, N_BRANCH * BRANCH_W, d)
    ffn_in_a = ffn_in_f[0:1].astype(BF16)
    ffn_out_a = ffn_out_f[0:1].astype(BF16)
    w_branch_bf = w_branch_f[0:1].astype(BF16)
    w_out_bf = w_out[0:1].astype(BF16)
    w_in_t = jnp.swapaxes(w_in, 1, 2)
    w_a_bf = _cast_cols(w_in_t, 0, A_COLS, bn=BN_A)
    w_c_bf = _cast_cols(w_in_t, C_START, C_QKV_COLS + GATE_COLS, bn=BN_C)

    h = jnp.concatenate([x.reshape(lat_rows, d), ctx.reshape(-1, d)], axis=0)
    bm = BM if all_rows % BM == 0 and lat_rows % BM == 0 else BM_SMALL
    mm = dict(seq=seq, bm=bm)
    norm = dict(seq=seq, bm=BM_NORM if all_rows % BM_NORM == 0 and lat_rows % BM_NORM == 0 else BM_SMALL)

    for l in range(depth):
        last = l == depth - 1
        rows_out = lat_rows if last else all_rows
        gain = norm_gain[l].reshape(3, 1, d)
        m = mods[l]

        def next_layer(src, idx):
            return None if last else (src, idx)

        xn = _modulate(h, all_rows, gain[0], m[0], m[1], **norm)
        g1, ffn_in_b = _swiglu(xn, ffn_in_a, 0, all_rows, bm=bm, bn=BN_FF, side=(ffn_in_f, 2 * l + 1))
        h, ffn_out_b = _resid_matmul(g1, ffn_out_a, 0, h, all_rows, m[2], weight=MACARON_W, bn=BN,
                                     side=(ffn_out_f, 2 * l + 1), **mm)

        xn = _modulate(h, all_rows, gain[1], m[3], m[4], **norm)
        pa = _matmul(xn, w_a_bf, l, all_rows, 0, A_COLS, bm=bm, bn=BN_A, out_dtype=F32)
        pc = _matmul(xn, w_c_bf, l, all_rows, 0, C_QKV_COLS, bm=bm, bn=BN_C, out_dtype=F32)
        gates = _matmul(xn, w_c_bf, l, rows_out, C_QKV_COLS, GATE_COLS, bm=bm, bn=BN_C,
                        out_dtype=BF16, sigmoid=True)
        a_lat, a_ctx = _gla(pa, _pad_gate_up(gla_gate_up[l]), gla_gate_bias[l].reshape(2, 1, -1),
                            gla_norm[l].reshape(1, -1), batch=batch)
        o_a = a_lat if last else jnp.concatenate([a_lat, a_ctx], axis=0)
        o_b = _swa(pc, swa_sink[l], cos_tab, sin_tab, swa_q_norm[l].reshape(1, -1), swa_k_norm[l].reshape(1, -1),
                   batch=batch, with_ctx_out=not last)
        o_c = _na(pc, na_bias[l], na_q_norm[l].reshape(1, -1), na_k_norm[l].reshape(1, -1),
                  batch=batch, with_ctx_out=not last)
        y, w_branch_next = _merge(o_a, o_b, o_c, w_branch_bf.reshape(1, N_BRANCH, BRANCH_W, d), 0, gates, rows_out,
                                  bm=bm, bn=BN_MERGE, side=next_layer(w_branch_f, l + 1))
        h, w_out_next = _resid_matmul(y, w_out_bf, 0, h, rows_out, m[5], weight=1.0, bn=BN,
                                      side=next_layer(w_out, l + 1), **mm)

        xn = _modulate(h, rows_out, gain[2], m[6], m[7], **norm)
        g2, ffn_in_a = _swiglu(xn, ffn_in_b, 0, rows_out, bm=bm, bn=BN_FF, side=next_layer(ffn_in_f, 2 * l + 2))
        h, ffn_out_a = _resid_matmul(g2, ffn_out_b, 0, h, rows_out, m[8], weight=MACARON_W, bn=BN,
                                     side=next_layer(ffn_out_f, 2 * l + 2), **mm)
        w_branch_bf, w_out_bf = w_branch_next, w_out_next

    return h.reshape(batch, seq, d)
```

```python
import functools

import jax
import jax.numpy as jnp
import numpy as np
from jax import lax
from jax.experimental import pallas as pl
from jax.experimental.pallas import tpu as pltpu

F32 = jnp.float32
BF16 = jnp.bfloat16
HIGHEST = lax.Precision.HIGHEST

D_MODEL = 4096
SEQ = 2048
CTX_LEN = 256
GRID_W = 64
HEAD_DIM = 128
GLA_HEADS = 4
GLA_DK = 128
GLA_DV = 256
GLA_GATE_RANK = 16
GLA_GATE_NORM = 16.0
GLA_CHUNK = 64
SWA_HEADS = 8
SWA_KV_HEADS = 2
SWA_GROUP = SWA_HEADS // SWA_KV_HEADS
SWA_WINDOW = 128
NA_HEADS = 8
NA_KH = 8
NA_KW = 16
N_BRANCH = 3
BRANCH_W = 1024
D_FF = 4096
MACARON_W = 0.5
N_MOD = 9
ROPE_BASE = 10000.0
EPS = 1e-6
NEG_INF = -1e30
MOD_ROWS = 8

A_GQ, A_GK, A_GV, A_GO, A_GD = 0, 512, 1024, 2048, 3072
A_COLS = 3200
C_START = 3104
C_SQ, C_SK, C_SV = 0, 1024, 1280
C_NQ, C_NK, C_NV = 1536, 2560, 3584
C_QKV_COLS = 4608
GATE_COLS = N_BRANCH * D_MODEL

V7X_VMEM_BYTES = 64 * 1024 * 1024
V7X_VMEM_RESERVE = 6 * 1024 * 1024
QBLK = 128
NA_SLAB_ROWS = 10


def _vmem_limit(pipelined_bytes, scratch_bytes=0, temp_bytes=0):
    need = 2 * pipelined_bytes + scratch_bytes + temp_bytes + (4 << 20)
    return int(min(max(need, 16 << 20), V7X_VMEM_BYTES - V7X_VMEM_RESERVE))


def _nbytes(shape, dtype):
    return int(np.prod(shape)) * jnp.dtype(dtype).itemsize


def _adaln_kernel(v_ref, down_ref, up_ref, bias_ref, o_ref, t_ref):
    @pl.when(pl.program_id(1) == 0)
    def _():
        t_ref[...] = jnp.dot(jax.nn.silu(v_ref[...]), down_ref[...], preferred_element_type=F32,
                             precision=HIGHEST)

    o_ref[...] = jnp.dot(t_ref[...], up_ref[...], preferred_element_type=F32, precision=HIGHEST) + bias_ref[...]


def _adaln(v8, ada_down, ada_up, ada_bias):
    depth, d, rank = ada_down.shape
    blocks = (_nbytes((MOD_ROWS, d), F32) + _nbytes((d, rank), F32) + _nbytes((rank, d), F32)
              + _nbytes((1, d), F32) + _nbytes((MOD_ROWS, d), F32))
    return pl.pallas_call(
        _adaln_kernel,
        grid=(depth, N_MOD),
        in_specs=[
            pl.BlockSpec((MOD_ROWS, d), lambda l, j: (0, 0)),
            pl.BlockSpec((None, d, rank), lambda l, j: (l, 0, 0)),
            pl.BlockSpec((None, rank, d), lambda l, j: (l, 0, j)),
            pl.BlockSpec((None, 1, d), lambda l, j: (l, 0, j)),
        ],
        out_specs=pl.BlockSpec((None, None, MOD_ROWS, d), lambda l, j: (l, j, 0, 0)),
        out_shape=jax.ShapeDtypeStruct((depth, N_MOD, MOD_ROWS, d), F32),
        scratch_shapes=[pltpu.VMEM((MOD_ROWS, rank), F32)],
        compiler_params=pltpu.CompilerParams(
            dimension_semantics=("arbitrary", "arbitrary"), vmem_limit_bytes=_vmem_limit(blocks)),
        name="adaln",
    )(v8, ada_down, ada_up, ada_bias.reshape(depth, 1, -1))


def _mod_row_index(i, bm, seq):
    return jnp.minimum((i * bm) // seq, MOD_ROWS - 1)


def _mod_spec(bm, seq, d):
    return pl.BlockSpec((None, 1, d), lambda i, j: (_mod_row_index(i, bm, seq), 0, 0))


NORM_ROWS = 32


def _modulate_kernel(h_ref, g_ref, sh_ref, sc_ref, xn_ref, *, bm):
    gain = g_ref[...]
    scale1 = 1.0 + sc_ref[...]
    shift = sh_ref[...]

    def body(c, carry):
        r0 = pl.multiple_of(c * NORM_ROWS, NORM_ROWS)
        x = h_ref[pl.ds(r0, NORM_ROWS), :]
        y = x * lax.rsqrt(jnp.mean(x * x, axis=-1, keepdims=True) + EPS)
        xn_ref[pl.ds(r0, NORM_ROWS), :] = ((y * gain) * scale1 + shift).astype(BF16)
        return carry

    lax.fori_loop(0, bm // NORM_ROWS, body, 0)


def _modulate(h, rows, gain, shift, scale, *, seq, bm):
    d = h.shape[1]
    blocks = _nbytes((bm, d), F32) + _nbytes((bm, d), BF16) + 3 * _nbytes((1, d), F32)
    return pl.pallas_call(
        functools.partial(_modulate_kernel, bm=bm),
        grid=(rows // bm, 1),
        in_specs=[
            pl.BlockSpec((bm, d), lambda i, j: (i, 0)),
            pl.BlockSpec((1, d), lambda i, j: (0, 0)),
            _mod_spec(bm, seq, d),
            _mod_spec(bm, seq, d),
        ],
        out_specs=pl.BlockSpec((bm, d), lambda i, j: (i, 0)),
        out_shape=jax.ShapeDtypeStruct((rows, d), BF16),
        compiler_params=pltpu.CompilerParams(
            dimension_semantics=("arbitrary", "arbitrary"), vmem_limit_bytes=_vmem_limit(blocks)),
        name="modulate",
    )(h, gain, shift, scale)


SIDE_ROWS = 64


def _side_cast(side, n_steps, nj):
    src, sidx = side
    _, r, cdim = src.shape
    n_chunks = r // SIDE_ROWS
    assert r % SIDE_ROWS == 0 and n_chunks <= n_steps

    def chunk(i, j):
        return jnp.minimum(i * nj + j, n_chunks - 1)

    in_spec = pl.BlockSpec((None, SIDE_ROWS, cdim), lambda i, j: (sidx, chunk(i, j), 0))
    out_spec = pl.BlockSpec((None, SIDE_ROWS, cdim), lambda i, j: (0, chunk(i, j), 0))
    nbytes = _nbytes((SIDE_ROWS, cdim), F32) + _nbytes((SIDE_ROWS, cdim), BF16)
    return in_spec, out_spec, jax.ShapeDtypeStruct((1, r, cdim), BF16), nbytes


def _with_side(body, n_in, n_out):
    def kernel(*refs):
        if len(refs) == n_in + n_out:
            body(*refs)
        else:
            body(*refs[:n_in], *refs[n_in + 1:n_in + 1 + n_out])
            refs[-1][...] = refs[n_in][...].astype(refs[-1].dtype)
    return kernel


def _call_with_side(body, n_in, side, grid, in_specs, out_spec, out_shape, blocks, temp, name, args):
    out_specs, out_shapes = [out_spec], [out_shape]
    if side is not None:
        s_in, s_out, s_shape, s_bytes = _side_cast(side, grid[0] * grid[1], grid[1])
        in_specs, out_specs, out_shapes = in_specs + [s_in], out_specs + [s_out], out_shapes + [s_shape]
        blocks, args = blocks + s_bytes, args + (side[0],)
    res = pl.pallas_call(
        _with_side(body, n_in, 1),
        grid=grid, in_specs=in_specs, out_specs=out_specs, out_shape=out_shapes,
        compiler_params=pltpu.CompilerParams(
            dimension_semantics=("arbitrary", "arbitrary"), vmem_limit_bytes=_vmem_limit(blocks, 0, temp)),
        name=name,
    )(*args)
    return (res[0], res[1]) if side is not None else (res[0], None)


def _swiglu_kernel(x_ref, wa_ref, wb_ref, o_ref):
    x = x_ref[...]
    a = jnp.dot(x, wa_ref[...], preferred_element_type=F32)
    b = jnp.dot(x, wb_ref[...], preferred_element_type=F32)
    o_ref[...] = (jax.nn.silu(a) * b).astype(o_ref.dtype)


def _swiglu(x, w_in, widx, rows, *, bm, bn, side=None):
    d = x.shape[1]
    f = w_in.shape[2] // 2
    nb = f // bn
    blocks = _nbytes((bm, d), BF16) + 2 * _nbytes((d, bn), BF16) + _nbytes((bm, bn), BF16)
    in_specs = [
        pl.BlockSpec((bm, d), lambda i, j: (i, 0)),
        pl.BlockSpec((None, d, bn), lambda i, j: (widx, 0, j)),
        pl.BlockSpec((None, d, bn), lambda i, j: (widx, 0, j + nb)),
    ]
    return _call_with_side(_swiglu_kernel, 3, side, (rows // bm, nb), in_specs,
                           pl.BlockSpec((bm, bn), lambda i, j: (i, j)), jax.ShapeDtypeStruct((rows, f), BF16),
                           blocks, 4 * _nbytes((bm, bn), F32), "swiglu", (x, w_in, w_in))


SIDE_T_ROWS = 128
SIDE_T_CHUNK = 512


def _mm_kernel(*refs, sigmoid, shift):
    if len(refs) == 3:
        x_ref, w_ref, o_ref = refs
    else:
        x_ref, w_ref, main_ref, next_ref, o_ref, dst_ref = refs
    y = jnp.dot(x_ref[...], w_ref[...], preferred_element_type=F32)
    if sigmoid:
        y = jax.nn.sigmoid(y)
    o_ref[...] = y.astype(o_ref.dtype)
    if len(refs) == 6:
        for c in range(dst_ref.shape[0] // SIDE_T_CHUNK):
            cols = slice(c * SIDE_T_CHUNK, (c + 1) * SIDE_T_CHUNK)
            if shift == 0:
                blk = main_ref[:, cols]
            else:
                blk = jnp.concatenate([main_ref[shift:, cols], next_ref[:, cols]], axis=0)
            dst_ref[cols, :] = blk.T.astype(dst_ref.dtype)


def _matmul(x, w, widx, rows, col0, n, *, bm, bn, out_dtype, sigmoid=False, side_t=None):
    k = x.shape[1]
    jb0 = col0 // bn
    grid = (rows // bm, n // bn)
    blocks = _nbytes((bm, k), BF16) + _nbytes((k, bn), BF16) + _nbytes((bm, bn), out_dtype)
    in_specs = [
        pl.BlockSpec((bm, k), lambda i, j: (i, 0)),
        pl.BlockSpec((None, k, bn), lambda i, j: (widx, 0, jb0 + j)),
    ]
    out_specs = [pl.BlockSpec((bm, bn), lambda i, j: (i, j))]
    out_shapes = [jax.ShapeDtypeStruct((rows, n), out_dtype)]
    args = (x, w)
    shift = 0
    if side_t is not None:
        w_t, sidx, c0, cn = side_t
        kk = w_t.shape[2]
        shift = c0 % SIDE_T_ROWS
        nxt = shift if shift else SUBLANES
        n_chunks = cn // SIDE_T_ROWS
        base_blk = (c0 - shift) // SIDE_T_ROWS
        assert cn % SIDE_T_ROWS == 0 and n_chunks <= grid[0] * grid[1] and SIDE_T_ROWS % nxt == 0

        def chunk(i, j):
            return jnp.minimum(i * grid[1] + j, n_chunks - 1)

        in_specs += [
            pl.BlockSpec((None, SIDE_T_ROWS, kk), lambda i, j: (sidx, base_blk + chunk(i, j), 0)),
            pl.BlockSpec((None, nxt, kk),
                         lambda i, j: (sidx, (base_blk + chunk(i, j) + 1) * (SIDE_T_ROWS // nxt), 0)),
        ]
        out_specs.append(pl.BlockSpec((None, kk, SIDE_T_ROWS), lambda i, j: (0, 0, chunk(i, j))))
        out_shapes.append(jax.ShapeDtypeStruct((1, kk, cn), BF16))
        blocks += _nbytes((SIDE_T_ROWS + nxt, kk), F32) + _nbytes((kk, SIDE_T_ROWS), BF16)
        args = (x, w, w_t, w_t)
    res = pl.pallas_call(
        functools.partial(_mm_kernel, sigmoid=sigmoid, shift=shift),
        grid=grid, in_specs=in_specs, out_specs=out_specs, out_shape=out_shapes,
        compiler_params=pltpu.CompilerParams(
            dimension_semantics=("arbitrary", "arbitrary"),
            vmem_limit_bytes=_vmem_limit(blocks, 0, 2 * _nbytes((bm, bn), F32) + (2 << 20))),
        name="matmul_sigmoid" if sigmoid else "matmul",
    )(*args)
    return (res[0], res[1]) if side_t is not None else (res[0], None)


def _resid_mm_kernel(x_ref, w_ref, h_ref, gate_ref, o_ref, *, weight):
    y = jnp.dot(x_ref[...], w_ref[...], preferred_element_type=F32)
    if weight != 1.0:
        o_ref[...] = h_ref[...] + weight * gate_ref[...] * y
    else:
        o_ref[...] = h_ref[...] + gate_ref[...] * y


def _resid_matmul(x, w, widx, h, rows, gate, *, weight, seq, bm, bn, side=None):
    k = x.shape[1]
    n = w.shape[2]
    blocks = (_nbytes((bm, k), BF16) + _nbytes((k, bn), BF16) + 2 * _nbytes((bm, bn), F32)
              + _nbytes((1, bn), F32))
    in_specs = [
        pl.BlockSpec((bm, k), lambda i, j: (i, 0)),
        pl.BlockSpec((None, k, bn), lambda i, j: (widx, 0, j)),
        pl.BlockSpec((bm, bn), lambda i, j: (i, j)),
        pl.BlockSpec((None, 1, bn), lambda i, j: (_mod_row_index(i, bm, seq), 0, j)),
    ]
    return _call_with_side(functools.partial(_resid_mm_kernel, weight=weight), 4, side, (rows // bm, n // bn),
                           in_specs, pl.BlockSpec((bm, bn), lambda i, j: (i, j)),
                           jax.ShapeDtypeStruct((rows, n), F32), blocks, _nbytes((bm, bn), F32),
                           "resid_matmul", (x, w, h, gate))


def _merge_kernel(oa_ref, ob_ref, oc_ref, w_ref, ga_ref, gb_ref, gc_ref, y_ref):
    y = ga_ref[...].astype(F32) * jnp.dot(oa_ref[...], w_ref[0], preferred_element_type=F32)
    y = y + gb_ref[...].astype(F32) * jnp.dot(ob_ref[...], w_ref[1], preferred_element_type=F32)
    y = y + gc_ref[...].astype(F32) * jnp.dot(oc_ref[...], w_ref[2], preferred_element_type=F32)
    y_ref[...] = y.astype(y_ref.dtype)


def _merge(oa, ob, oc, w_branch, widx, gates, rows, *, bm, bn, side=None):
    kb = oa.shape[1]
    d = w_branch.shape[3]
    per_branch = d // bn
    blocks = (3 * _nbytes((bm, kb), BF16) + _nbytes((N_BRANCH, kb, bn), BF16) + 4 * _nbytes((bm, bn), BF16))
    o_spec = pl.BlockSpec((bm, kb), lambda i, j: (i, 0))

    def gate_spec(br):
        return pl.BlockSpec((bm, bn), lambda i, j: (i, br * per_branch + j))

    in_specs = [o_spec, o_spec, o_spec,
                pl.BlockSpec((None, N_BRANCH, kb, bn), lambda i, j: (widx, 0, 0, j)),
                gate_spec(0), gate_spec(1), gate_spec(2)]
    return _call_with_side(_merge_kernel, 7, side, (rows // bm, d // bn), in_specs,
                           pl.BlockSpec((bm, bn), lambda i, j: (i, j)), jax.ShapeDtypeStruct((rows, d), BF16),
                           blocks, 4 * _nbytes((bm, bn), F32), "merge",
                           (oa, ob, oc, w_branch, gates, gates, gates))


GLA_BLOCK = 256
GLA_UNROLL = 4
GLA_BLOCKS_PER_STEP = 2
_NT = (((1,), (1,)), ((), ()))
_TN = (((0,), (0,)), ((), ()))


def _split_bf16(x):
    hi = x.astype(BF16)
    lo = (x - hi.astype(F32)).astype(BF16)
    return hi, lo


def _sum_dot(op, parts):
    acc = jnp.dot(op, parts[0], preferred_element_type=F32)
    for p in parts[1:]:
        acc = acc + jnp.dot(op, p, preferred_element_type=F32)
    return acc


def _gla_kernel(ql_ref, kl_ref, vl_ref, ogl_ref, gdl_ref, qc_ref, kc_ref, vc_ref, ogc_ref, gdc_ref,
                up_ref, gb_ref, gn_ref, al_ref, ac_ref,
                qe_ref, edec_ref, of_ref, ob_ref, kv_ref, st_ref, *, t_lat, t_ctx):
    c = GLA_CHUNK
    br = GLA_BLOCK
    cpb = br // c
    row = lax.broadcasted_iota(jnp.int32, (br, br), 0)
    col = lax.broadcasted_iota(jnp.int32, (br, br), 1)
    same = (row // c) == (col // c)
    keep = (same & (col <= row), same & (col >= row))
    tri = (keep[0].astype(BF16), keep[1].astype(BF16))
    chunk_of_row = lax.broadcasted_iota(jnp.int32, (br, GLA_DK), 0) // c
    q_scale = GLA_DK ** -0.5
    gain = gn_ref[...]
    o_refs = (of_ref, ob_ref)

    st_ref[...] = jnp.zeros_like(st_ref)

    def segment(q_ref, k_ref, v_ref, og_ref, gd_ref, out_ref, t):
        nc = t // c

        per_step = min(GLA_BLOCKS_PER_STEP, t // br)

        def block_body(r, carry):
            blocks = [r * per_step + i for i in range(per_step)]
            rows = [pl.ds(pl.multiple_of(b * br, br), br) for b in blocks]
            chains = [(i, d) for i in range(per_step) for d in range(2)]
            gd = [gd_ref[rw, :] for rw in rows]
            q = [q_ref[rw, :] * q_scale for rw in rows]
            k = [k_ref[rw, :] for rw in rows]
            v = [v_ref[rw, :].astype(BF16) for rw in rows]
            z = [jnp.dot(gd[i], up_ref[d], preferred_element_type=F32) + gb_ref[d] for i, d in chains]
            parts = [_split_bf16(jax.nn.log_sigmoid(zz) * (1.0 / GLA_GATE_NORM)) for zz in z]
            cum = [_sum_dot(tri[d], p) for (i, d), p in zip(chains, parts)]
            cum_end = []
            for (i, d), cm in zip(chains, cum):
                ends = [cm[j * c + c - 1:j * c + c] if d == 0 else cm[j * c:j * c + 1] for j in range(cpb)]
                cum_end.append(jnp.concatenate([jnp.broadcast_to(e, (c, e.shape[1])) for e in ends], axis=0))
            qe = [(q[i] * jnp.exp(cm)).astype(BF16) for (i, d), cm in zip(chains, cum)]
            kinv = [(k[i] * jnp.exp(-cm)).astype(BF16) for (i, d), cm in zip(chains, cum)]
            kdec = [(k[i] * jnp.exp(ce - cm)).astype(BF16) for (i, d), cm, ce in zip(chains, cum, cum_end)]
            a = [lax.dot_general(qq, kk, _NT, preferred_element_type=F32) for qq, kk in zip(qe, kinv)]
            kdec_by_chunk = [jnp.concatenate(
                [jnp.where(chunk_of_row == j, kd, jnp.zeros_like(kd)) for j in range(cpb)], axis=1) for kd in kdec]
            kv = [lax.dot_general(v[i], kd, _TN, preferred_element_type=F32)
                  for (i, d), kd in zip(chains, kdec_by_chunk)]
            a = [jnp.where(keep[d], aa, 0.0).astype(BF16) for (i, d), aa in zip(chains, a)]
            o = [jnp.dot(aa, v[i], preferred_element_type=F32) for (i, d), aa in zip(chains, a)]
            for n, (i, d) in enumerate(chains):
                qe_ref[d, rows[i], :] = qe[n]
                edec_ref[d, rows[i], :] = jnp.exp(cum_end[n])
                o_refs[d][rows[i], :] = o[n]
                for j in range(cpb):
                    kv_ref[d, blocks[i] * cpb + j] = kv[n][:, j * GLA_DK:(j + 1) * GLA_DK]
            return carry

        lax.fori_loop(0, t // (br * per_step), block_body, 0)

        def scan_body(n, carry):
            for d in range(2):
                m = n if d == 0 else nc - 1 - n
                rows = pl.ds(pl.multiple_of(m * c, c), c)
                st = st_ref[d]
                o_refs[d][rows, :] += lax.dot_general(qe_ref[d, rows, :], st.astype(BF16), _NT,
                                                      preferred_element_type=F32)
                st_ref[d] = st * edec_ref[d, pl.ds(pl.multiple_of(m * c, c), 1), :] + kv_ref[d, m]
            return carry

        lax.fori_loop(0, nc, scan_body, 0, unroll=GLA_UNROLL)

        def out_body(n, carry):
            rows = pl.ds(pl.multiple_of(n * c, c), c)
            o = of_ref[rows, :] + ob_ref[rows, :]
            y = o * lax.rsqrt(jnp.mean(o * o, axis=-1, keepdims=True) + EPS)
            y = (y * gain) * jax.nn.silu(og_ref[rows, :])
            out_ref[rows, :] = y.astype(out_ref.dtype)
            return carry

        lax.fori_loop(0, nc, out_body, 0, unroll=GLA_UNROLL)

    segment(qc_ref, kc_ref, vc_ref, ogc_ref, gdc_ref, ac_ref, t_ctx)
    segment(ql_ref, kl_ref, vl_ref, ogl_ref, gdl_ref, al_ref, t_lat)


def _gla(pa, gate_up_pad, gate_bias, gla_norm, *, batch):
    t_lat, t_ctx = SEQ, CTX_LEN
    ctx0 = batch * t_lat // t_ctx
    dk, dv = GLA_DK, GLA_DV
    nc = t_lat // GLA_CHUNK

    def lat(width, col0):
        return pl.BlockSpec((t_lat, width), lambda b, h: (b, col0 // width + h))

    def ctx(width, col0):
        return pl.BlockSpec((t_ctx, width), lambda b, h: (ctx0 + b, col0 // width + h))

    lat_gd = pl.BlockSpec((t_lat, 128), lambda b, h: (b, A_GD // 128))
    ctx_gd = pl.BlockSpec((t_ctx, 128), lambda b, h: (ctx0 + b, A_GD // 128))
    blocks = ((t_lat + t_ctx) * (3 * dk + 2 * dv) * 4 + _nbytes((2, 128, dk), F32)
              + (t_lat + t_ctx) * dv * 2)
    scratch_shapes = [pltpu.VMEM((2, t_lat, dk), BF16), pltpu.VMEM((2, t_lat, dk), F32),
                      pltpu.VMEM((t_lat, dv), F32), pltpu.VMEM((t_lat, dv), F32),
                      pltpu.VMEM((2, nc, dv, dk), F32), pltpu.VMEM((2, dv, dk), F32)]
    scratch = (_nbytes((2, t_lat, dk), BF16) + _nbytes((2, t_lat, dk), F32) + 2 * _nbytes((t_lat, dv), F32)
               + _nbytes((2, nc + 1, dv, dk), F32))
    return pl.pallas_call(
        functools.partial(_gla_kernel, t_lat=t_lat, t_ctx=t_ctx),
        grid=(batch, GLA_HEADS),
        in_specs=[lat(dk, A_GQ), lat(dk, A_GK), lat(dv, A_GV), lat(dv, A_GO), lat_gd,
                  ctx(dk, A_GQ), ctx(dk, A_GK), ctx(dv, A_GV), ctx(dv, A_GO), ctx_gd,
                  pl.BlockSpec((2, 128, dk), lambda b, h: (0, 0, h)),
                  pl.BlockSpec((2, 1, dk), lambda b, h: (0, 0, h)),
                  pl.BlockSpec((1, dv), lambda b, h: (0, 0))],
        out_specs=[pl.BlockSpec((t_lat, dv), lambda b, h: (b, h)),
                   pl.BlockSpec((t_ctx, dv), lambda b, h: (b, h))],
        out_shape=[jax.ShapeDtypeStruct((batch * t_lat, GLA_HEADS * dv), BF16),
                   jax.ShapeDtypeStruct((batch * t_ctx, GLA_HEADS * dv), BF16)],
        scratch_shapes=scratch_shapes,
        compiler_params=pltpu.CompilerParams(
            dimension_semantics=("arbitrary", "arbitrary"),
            vmem_limit_bytes=_vmem_limit(blocks, scratch, 6 << 20)),
        name="gla",
    )(pa, pa, pa, pa, pa, pa, pa, pa, pa, pa, gate_up_pad, gate_bias, gla_norm)


def _rms(x, gain):
    return (x * lax.rsqrt(jnp.mean(x * x, axis=-1, keepdims=True) + EPS)) * gain


def _rope(x, cos, sin_signed):
    lane = lax.broadcasted_iota(jnp.int32, x.shape, 1)
    partner = jnp.where((lane % 64) < 32, pltpu.roll(x, 96, 1), pltpu.roll(x, 32, 1))
    return x * cos + partner * sin_signed


def _q_row_block(b, qb, *, batch, n_lat, n_ctx):
    return jnp.where(qb < n_lat, b * n_lat + qb, batch * n_lat + b * n_ctx + (qb - n_lat))


KV_PREP_ROWS = 256


def _swa_window_masks(t_lat):
    span = 3 * QBLK
    q_off = np.array([0, QBLK, 2 * QBLK])[:, None, None] + np.arange(QBLK)[None, :, None]
    k_off = np.arange(span)[None, None, :]
    return jnp.asarray(np.where(np.abs(k_off - q_off) <= SWA_WINDOW, 0.0, NEG_INF), F32)


def _swa_mask_variant(qb, n_lat):
    return jnp.where(qb == 0, 0, jnp.where(qb >= n_lat - 1, 2, 1))


def _swa_kernel(sink_ref, q_ref, kl_ref, vl_ref, kc_ref, vc_ref, cq_ref, sq_ref, ck_ref, sk_ref,
                qg_ref, kg_ref, mask_ref, o_ref, kn_ref, vb_ref, kcn_ref, vcb_ref, *, t_lat, n_lat):
    qb = pl.program_id(1)
    g = SWA_GROUP
    dh = HEAD_DIM
    scale = dh ** -0.5

    @pl.when(qb == 0)
    def _prep():
        k_gain = kg_ref[...]

        def body(r, carry):
            r0 = pl.multiple_of(r * KV_PREP_ROWS, KV_PREP_ROWS)
            rows = pl.ds(r0, KV_PREP_ROWS)
            cos, sin = ck_ref[rows, :], sk_ref[rows, :]
            for kv in range(SWA_KV_HEADS):
                cols = slice(kv * dh, (kv + 1) * dh)
                kn_ref[rows, cols] = _rope(_rms(kl_ref[rows, cols], k_gain), cos, sin).astype(BF16)
            vb_ref[rows, :] = vl_ref[rows, :].astype(BF16)
            return carry

        lax.fori_loop(0, t_lat // KV_PREP_ROWS, body, 0)
        for kv in range(SWA_KV_HEADS):
            cols = slice(kv * dh, (kv + 1) * dh)
            kcn_ref[:, cols] = _rms(kc_ref[:, cols], k_gain).astype(BF16)
        vcb_ref[...] = vc_ref[...].astype(BF16)

    def attend(latent):
        cos, sin = cq_ref[...], sq_ref[...]
        q_gain = qg_ref[...]
        span = 3 * QBLK
        start = pl.multiple_of(jnp.clip((qb - 1) * QBLK, 0, t_lat - span), QBLK)
        kvs = range(SWA_KV_HEADS)
        cols = [slice(kv * dh, (kv + 1) * dh) for kv in kvs]
        qs = []
        for kv in kvs:
            heads = [_rope(_rms(q_ref[:, (kv * g + i) * dh:(kv * g + i + 1) * dh], q_gain), cos, sin).astype(BF16)
                     for i in range(g)]
            qs.append(jnp.concatenate(heads, axis=0))
        s_ctx = [lax.dot_general(qs[kv], kcn_ref[:, cols[kv]], _NT, preferred_element_type=F32) for kv in kvs]
        if latent:
            s_loc = [lax.dot_general(qs[kv], kn_ref[pl.ds(start, span), cols[kv]], _NT,
                                     preferred_element_type=F32) for kv in kvs]
        p_loc, p_ctx, dens = ([[] for _ in kvs] for _ in range(3))
        for i in range(g):
            rows = slice(i * QBLK, (i + 1) * QBLK)
            for kv in kvs:
                sink = sink_ref[kv * g + i]
                sc = s_ctx[kv][rows] * scale
                m = jnp.maximum(jnp.max(sc, axis=-1, keepdims=True), sink)
                if latent:
                    sl = s_loc[kv][rows] * scale + mask_ref[...]
                    m = jnp.maximum(m, jnp.max(sl, axis=-1, keepdims=True))
                    pl_i = jnp.exp(sl - m)
                    p_loc[kv].append(pl_i.astype(BF16))
                pc_i = jnp.exp(sc - m)
                p_ctx[kv].append(pc_i.astype(BF16))
                den = jnp.sum(pc_i, axis=-1, keepdims=True) + jnp.exp(sink - m)
                dens[kv].append(den + jnp.sum(pl_i, axis=-1, keepdims=True) if latent else den)
        o = [jnp.dot(jnp.concatenate(p_ctx[kv], axis=0), vcb_ref[:, cols[kv]], preferred_element_type=F32)
             for kv in kvs]
        if latent:
            o = [o[kv] + jnp.dot(jnp.concatenate(p_loc[kv], axis=0), vb_ref[pl.ds(start, span), cols[kv]],
                                 preferred_element_type=F32) for kv in kvs]
        for kv in kvs:
            for i in range(g):
                o_i = o[kv][i * QBLK:(i + 1) * QBLK, :] / dens[kv][i]
                o_ref[:, (kv * g + i) * dh:(kv * g + i + 1) * dh] = o_i.astype(o_ref.dtype)

    pl.when(qb < n_lat)(functools.partial(attend, True))
    pl.when(qb >= n_lat)(functools.partial(attend, False))


def _swa(pc, sink, cos_tab, sin_tab, q_gain, k_gain, *, batch, with_ctx_out):
    t_lat, t_ctx, dh = SEQ, CTX_LEN, HEAD_DIM
    qw, kw = SWA_HEADS * dh, SWA_KV_HEADS * dh
    n_lat, n_ctx = t_lat // QBLK, t_ctx // QBLK
    n_q = n_lat + (n_ctx if with_ctx_out else 0)
    rows_out = batch * (t_lat + (t_ctx if with_ctx_out else 0))
    ctx0 = batch * t_lat // t_ctx
    qmap = functools.partial(_q_row_block, batch=batch, n_lat=n_lat, n_ctx=n_ctx)
    blocks = (_nbytes((QBLK, qw), F32) + 2 * _nbytes((t_lat + t_ctx, kw), F32) + 2 * _nbytes((QBLK, dh), F32)
              + 2 * _nbytes((t_lat, dh), F32) + _nbytes((QBLK, qw), BF16))
    scratch = 2 * _nbytes((t_lat + t_ctx, kw), BF16)
    return pl.pallas_call(
        functools.partial(_swa_kernel, t_lat=t_lat, n_lat=n_lat),
        grid=(batch, n_q),
        in_specs=[
            pl.BlockSpec(memory_space=pltpu.SMEM),
            pl.BlockSpec((QBLK, qw), lambda b, qb: (qmap(b, qb), C_SQ // qw)),
            pl.BlockSpec((t_lat, kw), lambda b, qb: (b, C_SK // kw)),
            pl.BlockSpec((t_lat, kw), lambda b, qb: (b, C_SV // kw)),
            pl.BlockSpec((t_ctx, kw), lambda b, qb: (ctx0 + b, C_SK // kw)),
            pl.BlockSpec((t_ctx, kw), lambda b, qb: (ctx0 + b, C_SV // kw)),
            pl.BlockSpec((QBLK, dh), lambda b, qb: (qb, 0)),
            pl.BlockSpec((QBLK, dh), lambda b, qb: (qb, 0)),
            pl.BlockSpec((t_lat, dh), lambda b, qb: (0, 0)),
            pl.BlockSpec((t_lat, dh), lambda b, qb: (0, 0)),
            pl.BlockSpec((1, dh), lambda b, qb: (0, 0)),
            pl.BlockSpec((1, dh), lambda b, qb: (0, 0)),
            pl.BlockSpec((None, QBLK, 3 * QBLK), lambda b, qb: (_swa_mask_variant(qb, n_lat), 0, 0)),
        ],
        out_specs=pl.BlockSpec((QBLK, qw), lambda b, qb: (qmap(b, qb), 0)),
        out_shape=jax.ShapeDtypeStruct((rows_out, qw), BF16),
        scratch_shapes=[pltpu.VMEM((t_lat, kw), BF16), pltpu.VMEM((t_lat, kw), BF16),
                        pltpu.VMEM((t_ctx, kw), BF16), pltpu.VMEM((t_ctx, kw), BF16)],
        compiler_params=pltpu.CompilerParams(
            dimension_semantics=("arbitrary", "arbitrary"),
            vmem_limit_bytes=_vmem_limit(blocks, scratch, 12 << 20)),
        name="swa",
    )(sink, pc, pc, pc, pc, pc, cos_tab, sin_tab, cos_tab, sin_tab, q_gain, k_gain, _swa_window_masks(t_lat))


NA_HEADS_PER_STEP = 4


def _na_slab_start(qb):
    r = qb * (QBLK // GRID_W)
    rows = SEQ // GRID_W
    return jnp.minimum(jnp.clip(r - NA_KH // 2, 0, rows - NA_KH), rows - NA_SLAB_ROWS)


def _na_bias_variant(qb, n_lat):
    return jnp.where(qb < 2, qb, jnp.where(qb < n_lat - 2, 2, jnp.minimum(qb, n_lat - 1) - (n_lat - 5)))


def _na_kernel(q_ref, kl_ref, vl_ref, kc_ref, vc_ref, bias_ref, qg_ref, kg_ref, o_ref,
               kn_ref, vb_ref, kcn_ref, vcb_ref, *, t_lat, n_lat):
    qb = pl.program_id(2)
    dh = HEAD_DIM
    scale = dh ** -0.5

    @pl.when(qb == 0)
    def _prep():
        k_gain = kg_ref[...]

        def body(r, carry):
            r0 = pl.multiple_of(r * KV_PREP_ROWS, KV_PREP_ROWS)
            rows = pl.ds(r0, KV_PREP_ROWS)
            for h in range(NA_HEADS_PER_STEP):
                cols = slice(h * dh, (h + 1) * dh)
                kn_ref[rows, cols] = _rms(kl_ref[rows, cols], k_gain).astype(BF16)
            vb_ref[rows, :] = vl_ref[rows, :].astype(BF16)
            return carry

        lax.fori_loop(0, t_lat // KV_PREP_ROWS, body, 0)
        for h in range(NA_HEADS_PER_STEP):
            cols = slice(h * dh, (h + 1) * dh)
            kcn_ref[:, cols] = _rms(kc_ref[:, cols], k_gain).astype(BF16)
        vcb_ref[...] = vc_ref[...].astype(BF16)

    def attend(latent):
        q_gain = qg_ref[...]
        heads = range(NA_HEADS_PER_STEP)
        cols = [slice(h * dh, (h + 1) * dh) for h in heads]
        span = NA_SLAB_ROWS * GRID_W
        start = pl.multiple_of(_na_slab_start(qb) * GRID_W, GRID_W)
        qn = [_rms(q_ref[:, cols[h]], q_gain).astype(BF16) for h in heads]
        s_ctx = [lax.dot_general(qn[h], kcn_ref[:, cols[h]], _NT, preferred_element_type=F32) * scale
                 for h in heads]
        m = [jnp.max(s, axis=-1, keepdims=True) for s in s_ctx]
        if latent:
            s_loc = [lax.dot_general(qn[h], kn_ref[pl.ds(start, span), cols[h]], _NT,
                                     preferred_element_type=F32) for h in heads]
            s_loc = [s_loc[h] * scale + bias_ref[h] for h in heads]
            m = [jnp.maximum(m[h], jnp.max(s_loc[h], axis=-1, keepdims=True)) for h in heads]
            p_loc = [jnp.exp(s_loc[h] - m[h]) for h in heads]
        p_ctx = [jnp.exp(s_ctx[h] - m[h]) for h in heads]
        o = [jnp.dot(p_ctx[h].astype(BF16), vcb_ref[:, cols[h]], preferred_element_type=F32) for h in heads]
        den = [jnp.sum(p_ctx[h], axis=-1, keepdims=True) for h in heads]
        if latent:
            o = [o[h] + jnp.dot(p_loc[h].astype(BF16), vb_ref[pl.ds(start, span), cols[h]],
                                preferred_element_type=F32) for h in heads]
            den = [den[h] + jnp.sum(p_loc[h], axis=-1, keepdims=True) for h in heads]
        for h in heads:
            o_ref[:, cols[h]] = (o[h] / den[h]).astype(o_ref.dtype)

    pl.when(qb < n_lat)(functools.partial(attend, True))
    pl.when(qb >= n_lat)(functools.partial(attend, False))


def _na(pc, bias_tab, q_gain, k_gain, *, batch, with_ctx_out):
    t_lat, t_ctx, dh = SEQ, CTX_LEN, HEAD_DIM
    hb = NA_HEADS_PER_STEP
    w = hb * dh
    n_lat, n_ctx = t_lat // QBLK, t_ctx // QBLK
    n_q = n_lat + (n_ctx if with_ctx_out else 0)
    rows_out = batch * (t_lat + (t_ctx if with_ctx_out else 0))
    ctx0 = batch * t_lat // t_ctx
    span = NA_SLAB_ROWS * GRID_W
    qmap = functools.partial(_q_row_block, batch=batch, n_lat=n_lat, n_ctx=n_ctx)
    blocks = (_nbytes((QBLK, w), F32) + 2 * _nbytes((t_lat + t_ctx, w), F32) + _nbytes((hb, QBLK, span), F32)
              + _nbytes((QBLK, w), BF16))
    scratch = 2 * _nbytes((t_lat + t_ctx, w), BF16)
    return pl.pallas_call(
        functools.partial(_na_kernel, t_lat=t_lat, n_lat=n_lat),
        grid=(batch, NA_HEADS // hb, n_q),
        in_specs=[
            pl.BlockSpec((QBLK, w), lambda b, hg, qb: (qmap(b, qb), C_NQ // w + hg)),
            pl.BlockSpec((t_lat, w), lambda b, hg, qb: (b, C_NK // w + hg)),
            pl.BlockSpec((t_lat, w), lambda b, hg, qb: (b, C_NV // w + hg)),
            pl.BlockSpec((t_ctx, w), lambda b, hg, qb: (ctx0 + b, C_NK // w + hg)),
            pl.BlockSpec((t_ctx, w), lambda b, hg, qb: (ctx0 + b, C_NV // w + hg)),
            pl.BlockSpec((hb, None, QBLK, span), lambda b, hg, qb: (hg, _na_bias_variant(qb, n_lat), 0, 0)),
            pl.BlockSpec((1, dh), lambda b, hg, qb: (0, 0)),
            pl.BlockSpec((1, dh), lambda b, hg, qb: (0, 0)),
        ],
        out_specs=pl.BlockSpec((QBLK, w), lambda b, hg, qb: (qmap(b, qb), hg)),
        out_shape=jax.ShapeDtypeStruct((rows_out, NA_HEADS * dh), BF16),
        scratch_shapes=[pltpu.VMEM((t_lat, w), BF16), pltpu.VMEM((t_lat, w), BF16),
                        pltpu.VMEM((t_ctx, w), BF16), pltpu.VMEM((t_ctx, w), BF16)],
        compiler_params=pltpu.CompilerParams(
            dimension_semantics=("arbitrary", "arbitrary", "arbitrary"),
            vmem_limit_bytes=_vmem_limit(blocks, scratch, 12 << 20)),
        name="natten",
    )(pc, pc, pc, pc, pc, bias_tab, q_gain, k_gain)


def _rope_tables():
    quarter = HEAD_DIM // 4
    pos = jnp.arange(SEQ)
    rows = (pos // GRID_W).astype(F32)
    cols = (pos % GRID_W).astype(F32)
    inv = ROPE_BASE ** (-jnp.arange(quarter, dtype=F32) / quarter)
    ang_r = rows[:, None] * inv[None, :]
    ang_c = cols[:, None] * inv[None, :]
    cos = jnp.concatenate([jnp.cos(ang_r), jnp.cos(ang_r), jnp.cos(ang_c), jnp.cos(ang_c)], axis=-1)
    sin = jnp.concatenate([-jnp.sin(ang_r), jnp.sin(ang_r), -jnp.sin(ang_c), jnp.sin(ang_c)], axis=-1)
    cos = jnp.concatenate([cos, jnp.ones((CTX_LEN, HEAD_DIM), F32)], axis=0)
    sin = jnp.concatenate([sin, jnp.zeros((CTX_LEN, HEAD_DIM), F32)], axis=0)
    return cos, sin


def _na_bias_tables(rpb):
    n_layers, n_heads = rpb.shape[:2]
    rows = SEQ // GRID_W
    n_lat = SEQ // QBLK
    rq_per = QBLK // GRID_W
    reps = np.array([0, 1, 2, n_lat - 2, n_lat - 1])
    r = reps * rq_per
    start = np.minimum(np.clip(r - NA_KH // 2, 0, rows - NA_KH), rows - NA_SLAB_ROWS)
    rq = r[:, None] + np.arange(rq_per)[None, :]
    kr = start[:, None] + np.arange(NA_SLAB_ROWS)[None, :]
    r0 = np.clip(rq - NA_KH // 2, 0, rows - NA_KH)
    valid_r = (kr[:, None, :] >= r0[:, :, None]) & (kr[:, None, :] < r0[:, :, None] + NA_KH)
    dr = np.clip(kr[:, None, :] - rq[:, :, None] + NA_KH - 1, 0, 2 * NA_KH - 2)
    qc = np.arange(GRID_W)
    kc = np.arange(GRID_W)
    cs = np.clip(qc - NA_KW // 2, 0, GRID_W - NA_KW)
    valid_c = (kc[None, :] >= cs[:, None]) & (kc[None, :] < cs[:, None] + NA_KW)
    dc = np.clip(kc[None, :] - qc[:, None] + NA_KW - 1, 0, 2 * NA_KW - 2)
    pick_r = np.eye(2 * NA_KH - 1, dtype=np.float32)[dr.reshape(-1)]
    pick_c = np.eye(2 * NA_KW - 1, dtype=np.float32)[dc.reshape(-1)]
    t = jnp.einsum("nr,lhrc->lhnc", pick_r, rpb.astype(F32), precision=HIGHEST)
    t = jnp.einsum("lhnc,xc->lhnx", t, pick_c, precision=HIGHEST)
    t = t.reshape(n_layers, n_heads, len(reps), rq_per, NA_SLAB_ROWS, GRID_W, GRID_W)
    t = t.transpose(0, 1, 2, 3, 5, 4, 6)
    valid = valid_r[:, :, None, :, None] & valid_c[None, None, :, None, :]
    t = jnp.where(valid[None, None], t, NEG_INF)
    return t.reshape(n_layers, n_heads, len(reps), QBLK, NA_SLAB_ROWS * GRID_W)


SUBLANES = 8
CAST_K = 2048
CAST_CHUNK = 256


def _cast_cols_kernel(main_ref, next_ref, o_ref, *, shift):
    bn = o_ref.shape[1]
    for c in range(o_ref.shape[0] // CAST_CHUNK):
        cols = slice(c * CAST_CHUNK, (c + 1) * CAST_CHUNK)
        if shift == 0:
            x = main_ref[:, cols]
        else:
            x = jnp.concatenate([main_ref[shift:bn, cols], next_ref[:, cols]], axis=0)
        o_ref[cols, :] = x.T.astype(o_ref.dtype)


def _cast_cols(w_t, n_layers, col0, n, *, bn):
    k = w_t.shape[2]
    shift = col0 % bn
    base = col0 - shift
    nxt = shift if shift else SUBLANES
    assert n % bn == 0 and k % CAST_K == 0 and nxt % SUBLANES == 0 and bn % nxt == 0 and base % nxt == 0
    blocks = _nbytes((bn + nxt, CAST_K), F32) + _nbytes((CAST_K, bn), BF16)
    return pl.pallas_call(
        functools.partial(_cast_cols_kernel, shift=shift),
        grid=(n_layers, k // CAST_K, n // bn),
        in_specs=[
            pl.BlockSpec((None, bn, CAST_K), lambda l, i, j: (l, base // bn + j, i)),
            pl.BlockSpec((None, nxt, CAST_K), lambda l, i, j: (l, (base + (j + 1) * bn) // nxt, i)),
        ],
        out_specs=pl.BlockSpec((None, CAST_K, bn), lambda l, i, j: (l, i, j)),
        out_shape=jax.ShapeDtypeStruct((n_layers, k, n), BF16),
        compiler_params=pltpu.CompilerParams(
            dimension_semantics=("arbitrary", "arbitrary", "arbitrary"),
            vmem_limit_bytes=_vmem_limit(blocks, 0, 4 << 20)),
        name="cast_cols",
    )(w_t, w_t)


def _pad_gate_up(gate_up):
    r = gate_up.shape[1]
    out = jnp.zeros((2, 128, gate_up.shape[2]), F32)
    out = out.at[0, 0:r].set(gate_up[0])
    out = out.at[1, r:2 * r].set(gate_up[1])
    return out


BM = 1024
BM_SMALL = 256
BM_NORM = 512
BN = 512
BN_FF = 512
BN_A = 640
BN_C = 768
BN_MERGE = 512


def kernel(x, c, ctx, c_ctx, ada_down, ada_up, ada_bias, norm_gain, ffn_w_in, ffn_w_out, w_in,
           gla_gate_up, gla_gate_bias, gla_norm, swa_q_norm, swa_k_norm, swa_sink, na_q_norm,
           na_k_norm, na_rpb, w_branch, w_out):
    batch, seq, d = x.shape
    depth = ada_down.shape[0]
    assert (seq, d, ctx.shape[1]) == (SEQ, D_MODEL, CTX_LEN) and batch + 1 <= MOD_ROWS
    assert w_in.shape[2] == C_START + C_QKV_COLS + GATE_COLS
    lat_rows = batch * seq
    all_rows = lat_rows + batch * ctx.shape[1]

    v8 = jnp.concatenate([c, c_ctx[None], jnp.zeros((MOD_ROWS - batch - 1, d), F32)], axis=0)
    mods = _adaln(v8, ada_down, ada_up, ada_bias)
    mods = mods.reshape(depth, N_MOD, MOD_ROWS, 1, d)
    cos_tab, sin_tab = _rope_tables()
    na_bias = _na_bias_tables(na_rpb)

    ffn_in_f = ffn_w_in.reshape(depth * 2, d, -1)
    ffn_out_f = ffn_w_out.reshape(depth * 2, -1, d)
    w_branch_f = w_branch.reshape(depth, N_BRANCH * BRANCH_W, d)
    ffn_in_a = ffn_in_f[0:1].astype(BF16)
    ffn_out_a = ffn_out_f[0:1].astype(BF16)
    w_branch_bf = w_branch_f[0:1].astype(BF16)
    w_out_bf = w_out[0:1].astype(BF16)
    w_in_t = jnp.swapaxes(w_in, 1, 2)
    w_a_bf = _cast_cols(w_in_t, 1, 0, A_COLS, bn=BN_A)
    w_c_bf = _cast_cols(w_in_t, 1, C_START, C_QKV_COLS + GATE_COLS, bn=BN_C)

    h = jnp.concatenate([x.reshape(lat_rows, d), ctx.reshape(-1, d)], axis=0)
    bm = BM if all_rows % BM == 0 and lat_rows % BM == 0 else BM_SMALL
    mm = dict(seq=seq, bm=bm)
    norm = dict(seq=seq, bm=BM_NORM if all_rows % BM_NORM == 0 and lat_rows % BM_NORM == 0 else BM_SMALL)

    for l in range(depth):
        last = l == depth - 1
        rows_out = lat_rows if last else all_rows
        gain = norm_gain[l].reshape(3, 1, d)
        m = mods[l]

        def next_layer(src, idx):
            return None if last else (src, idx)

        xn = _modulate(h, all_rows, gain[0], m[0], m[1], **norm)
        g1, ffn_in_b = _swiglu(xn, ffn_in_a, 0, all_rows, bm=bm, bn=BN_FF, side=(ffn_in_f, 2 * l + 1))
        h, ffn_out_b = _resid_matmul(g1, ffn_out_a, 0, h, all_rows, m[2], weight=MACARON_W, bn=BN,
                                     side=(ffn_out_f, 2 * l + 1), **mm)

        xn = _modulate(h, all_rows, gain[1], m[3], m[4], **norm)
        pa, w_a_next = _matmul(xn, w_a_bf, 0, all_rows, 0, A_COLS, bm=bm, bn=BN_A, out_dtype=F32,
                               side_t=None if last else (w_in_t, l + 1, 0, A_COLS))
        pc, _ = _matmul(xn, w_c_bf, 0, all_rows, 0, C_QKV_COLS, bm=bm, bn=BN_C, out_dtype=F32)
        gates, w_c_next = _matmul(xn, w_c_bf, 0, rows_out, C_QKV_COLS, GATE_COLS, bm=bm, bn=BN_C,
                                  out_dtype=BF16, sigmoid=True,
                                  side_t=None if last else (w_in_t, l + 1, C_START, C_QKV_COLS + GATE_COLS))
        a_lat, a_ctx = _gla(pa, _pad_gate_up(gla_gate_up[l]), gla_gate_bias[l].reshape(2, 1, -1),
                            gla_norm[l].reshape(1, -1), batch=batch)
        o_a = a_lat if last else jnp.concatenate([a_lat, a_ctx], axis=0)
        o_b = _swa(pc, swa_sink[l], cos_tab, sin_tab, swa_q_norm[l].reshape(1, -1), swa_k_norm[l].reshape(1, -1),
                   batch=batch, with_ctx_out=not last)
        o_c = _na(pc, na_bias[l], na_q_norm[l].reshape(1, -1), na_k_norm[l].reshape(1, -1),
                  batch=batch, with_ctx_out=not last)
        y, w_branch_next = _merge(o_a, o_b, o_c, w_branch_bf.reshape(1, N_BRANCH, BRANCH_W, d), 0, gates, rows_out,
                                  bm=bm, bn=BN_MERGE, side=next_layer(w_branch_f, l + 1))
        h, w_out_next = _resid_matmul(y, w_out_bf, 0, h, rows_out, m[5], weight=1.0, bn=BN,
                                      side=next_layer(w_out, l + 1), **mm)

        xn = _modulate(h, rows_out, gain[2], m[6], m[7], **norm)
        g2, ffn_in_a = _swiglu(xn, ffn_in_b, 0, rows_out, bm=bm, bn=BN_FF, side=next_layer(ffn_in_f, 2 * l + 2))
        h, ffn_out_a = _resid_matmul(g2, ffn_out_b, 0, h, rows_out, m[8], weight=MACARON_W, bn=BN,
                                     side=next_layer(ffn_out_f, 2 * l + 2), **mm)
        w_branch_bf, w_out_bf, w_a_bf, w_c_bf = w_branch_next, w_out_next, w_a_next, w_c_next

    return h.reshape(batch, seq, d)
```

```python
import functools

import jax
import jax.numpy as jnp
import numpy as np
from jax import lax
from jax.experimental import pallas as pl
from jax.experimental.pallas import tpu as pltpu

F32 = jnp.float32
BF16 = jnp.bfloat16
HIGHEST = lax.Precision.HIGHEST

D_MODEL = 4096
SEQ = 2048
CTX_LEN = 256
GRID_W = 64
HEAD_DIM = 128
GLA_HEADS = 4
GLA_DK = 128
GLA_DV = 256
GLA_GATE_RANK = 16
GLA_GATE_NORM = 16.0
GLA_CHUNK = 64
SWA_HEADS = 8
SWA_KV_HEADS = 2
SWA_GROUP = SWA_HEADS // SWA_KV_HEADS
SWA_WINDOW = 128
NA_HEADS = 8
NA_KH = 8
NA_KW = 16
N_BRANCH = 3
BRANCH_W = 1024
D_FF = 4096
MACARON_W = 0.5
N_MOD = 9
ROPE_BASE = 10000.0
EPS = 1e-6
NEG_INF = -1e30
MOD_ROWS = 8

A_GQ, A_GK, A_GV, A_GO, A_GD = 0, 512, 1024, 2048, 3072
A_COLS = 3200
C_START = 3104
C_SQ, C_SK, C_SV = 0, 1024, 1280
C_NQ, C_NK, C_NV = 1536, 2560, 3584
C_QKV_COLS = 4608
GATE_COLS = N_BRANCH * D_MODEL

V7X_VMEM_BYTES = 64 * 1024 * 1024
V7X_VMEM_RESERVE = 6 * 1024 * 1024
QBLK = 128
NA_SLAB_ROWS = 10


def _vmem_limit(pipelined_bytes, scratch_bytes=0, temp_bytes=0):
    need = 2 * pipelined_bytes + scratch_bytes + temp_bytes + (4 << 20)
    return int(min(max(need, 16 << 20), V7X_VMEM_BYTES - V7X_VMEM_RESERVE))


def _nbytes(shape, dtype):
    return int(np.prod(shape)) * jnp.dtype(dtype).itemsize


def _adaln_kernel(v_ref, down_ref, up_ref, bias_ref, o_ref, t_ref):
    @pl.when(pl.program_id(1) == 0)
    def _():
        t_ref[...] = jnp.dot(jax.nn.silu(v_ref[...]), down_ref[...], preferred_element_type=F32,
                             precision=HIGHEST)

    o_ref[...] = jnp.dot(t_ref[...], up_ref[...], preferred_element_type=F32, precision=HIGHEST) + bias_ref[...]


def _adaln(v8, ada_down, ada_up, ada_bias):
    depth, d, rank = ada_down.shape
    blocks = (_nbytes((MOD_ROWS, d), F32) + _nbytes((d, rank), F32) + _nbytes((rank, d), F32)
              + _nbytes((1, d), F32) + _nbytes((MOD_ROWS, d), F32))
    return pl.pallas_call(
        _adaln_kernel,
        grid=(depth, N_MOD),
        in_specs=[
            pl.BlockSpec((MOD_ROWS, d), lambda l, j: (0, 0)),
            pl.BlockSpec((None, d, rank), lambda l, j: (l, 0, 0)),
            pl.BlockSpec((None, rank, d), lambda l, j: (l, 0, j)),
            pl.BlockSpec((None, 1, d), lambda l, j: (l, 0, j)),
        ],
        out_specs=pl.BlockSpec((None, None, MOD_ROWS, d), lambda l, j: (l, j, 0, 0)),
        out_shape=jax.ShapeDtypeStruct((depth, N_MOD, MOD_ROWS, d), F32),
        scratch_shapes=[pltpu.VMEM((MOD_ROWS, rank), F32)],
        compiler_params=pltpu.CompilerParams(
            dimension_semantics=("arbitrary", "arbitrary"), vmem_limit_bytes=_vmem_limit(blocks)),
        name="adaln",
    )(v8, ada_down, ada_up, ada_bias.reshape(depth, 1, -1))


def _mod_row_index(i, bm, seq):
    return jnp.minimum((i * bm) // seq, MOD_ROWS - 1)


def _mod_spec(bm, seq, d):
    return pl.BlockSpec((None, 1, d), lambda i, j: (_mod_row_index(i, bm, seq), 0, 0))


NORM_ROWS = 32


def _modulate_kernel(h_ref, g_ref, sh_ref, sc_ref, xn_ref, mult_ref, *, bm):
    mult_ref[...] = g_ref[...] * (1.0 + sc_ref[...])

    def body(c, carry):
        r0 = pl.multiple_of(c * NORM_ROWS, NORM_ROWS)
        x = h_ref[pl.ds(r0, NORM_ROWS), :]
        r = lax.rsqrt(jnp.mean(x * x, axis=-1, keepdims=True) + EPS)
        xn_ref[pl.ds(r0, NORM_ROWS), :] = ((x * r) * mult_ref[...] + sh_ref[...]).astype(BF16)
        return carry

    lax.fori_loop(0, bm // NORM_ROWS, body, 0, unroll=2)


def _modulate(h, rows, gain, shift, scale, *, seq, bm):
    d = h.shape[1]
    blocks = _nbytes((bm, d), F32) + _nbytes((bm, d), BF16) + 3 * _nbytes((1, d), F32)
    return pl.pallas_call(
        functools.partial(_modulate_kernel, bm=bm),
        grid=(rows // bm, 1),
        in_specs=[
            pl.BlockSpec((bm, d), lambda i, j: (i, 0)),
            pl.BlockSpec((1, d), lambda i, j: (0, 0)),
            _mod_spec(bm, seq, d),
            _mod_spec(bm, seq, d),
        ],
        out_specs=pl.BlockSpec((bm, d), lambda i, j: (i, 0)),
        out_shape=jax.ShapeDtypeStruct((rows, d), BF16),
        scratch_shapes=[pltpu.VMEM((1, d), F32)],
        compiler_params=pltpu.CompilerParams(
            dimension_semantics=("arbitrary", "arbitrary"), vmem_limit_bytes=_vmem_limit(blocks)),
        name="modulate",
    )(h, gain, shift, scale)


SIDE_ROWS = 64


def _side_cast(side, n_steps, nj):
    src, sidx = side
    _, r, cdim = src.shape
    side_rows = SIDE_ROWS
    while r // side_rows > n_steps:
        side_rows *= 2
    n_chunks = r // side_rows
    assert r % side_rows == 0

    def chunk(i, j):
        return jnp.minimum(i * nj + j, n_chunks - 1)

    in_spec = pl.BlockSpec((None, side_rows, cdim), lambda i, j: (sidx, chunk(i, j), 0))
    out_spec = pl.BlockSpec((None, side_rows, cdim), lambda i, j: (0, chunk(i, j), 0))
    nbytes = _nbytes((side_rows, cdim), F32) + _nbytes((side_rows, cdim), BF16)
    return in_spec, out_spec, jax.ShapeDtypeStruct((1, r, cdim), BF16), nbytes


def _with_side(body, n_in, n_out):
    def kernel(*refs):
        if len(refs) == n_in + n_out:
            body(*refs)
        else:
            body(*refs[:n_in], *refs[n_in + 1:n_in + 1 + n_out])
            refs[-1][...] = refs[n_in][...].astype(refs[-1].dtype)
    return kernel


def _call_with_side(body, n_in, side, grid, in_specs, out_spec, out_shape, blocks, temp, name, args):
    out_specs, out_shapes = [out_spec], [out_shape]
    if side is not None:
        s_in, s_out, s_shape, s_bytes = _side_cast(side, grid[0] * grid[1], grid[1])
        in_specs, out_specs, out_shapes = in_specs + [s_in], out_specs + [s_out], out_shapes + [s_shape]
        blocks, args = blocks + s_bytes, args + (side[0],)
    res = pl.pallas_call(
        _with_side(body, n_in, 1),
        grid=grid, in_specs=in_specs, out_specs=out_specs, out_shape=out_shapes,
        compiler_params=pltpu.CompilerParams(
            dimension_semantics=("arbitrary", "arbitrary"), vmem_limit_bytes=_vmem_limit(blocks, 0, temp)),
        name=name,
    )(*args)
    return (res[0], res[1]) if side is not None else (res[0], None)


def _swiglu_kernel(x_ref, wa_ref, wb_ref, o_ref):
    x = x_ref[...]
    a = jnp.dot(x, wa_ref[...], preferred_element_type=F32)
    b = jnp.dot(x, wb_ref[...], preferred_element_type=F32)
    o_ref[...] = (jax.nn.silu(a) * b).astype(o_ref.dtype)


def _swiglu(x, w_in, widx, rows, *, bm, bn, side=None):
    d = x.shape[1]
    f = w_in.shape[2] // 2
    nb = f // bn
    blocks = _nbytes((bm, d), BF16) + 2 * _nbytes((d, bn), BF16) + _nbytes((bm, bn), BF16)
    in_specs = [
        pl.BlockSpec((bm, d), lambda i, j: (i, 0)),
        pl.BlockSpec((None, d, bn), lambda i, j: (widx, 0, j)),
        pl.BlockSpec((None, d, bn), lambda i, j: (widx, 0, j + nb)),
    ]
    return _call_with_side(_swiglu_kernel, 3, side, (rows // bm, nb), in_specs,
                           pl.BlockSpec((bm, bn), lambda i, j: (i, j)), jax.ShapeDtypeStruct((rows, f), BF16),
                           blocks, 4 * _nbytes((bm, bn), F32), "swiglu", (x, w_in, w_in))


SIDE_T_ROWS = 128
SIDE_T_CHUNK = 512


def _mm_kernel(*refs, sigmoid, shift):
    if len(refs) == 3:
        x_ref, w_ref, o_ref = refs
    else:
        x_ref, w_ref, main_ref, next_ref, o_ref, dst_ref = refs
    y = jnp.dot(x_ref[...], w_ref[...], preferred_element_type=F32)
    if sigmoid:
        y = jax.nn.sigmoid(y)
    o_ref[...] = y.astype(o_ref.dtype)
    if len(refs) == 6:
        for c in range(dst_ref.shape[0] // SIDE_T_CHUNK):
            cols = slice(c * SIDE_T_CHUNK, (c + 1) * SIDE_T_CHUNK)
            if shift == 0:
                blk = main_ref[:, cols]
            else:
                blk = jnp.concatenate([main_ref[shift:, cols], next_ref[:, cols]], axis=0)
            dst_ref[cols, :] = blk.T.astype(dst_ref.dtype)


def _matmul(x, w, widx, rows, col0, n, *, bm, bn, out_dtype, sigmoid=False, side_t=None):
    k = x.shape[1]
    jb0 = col0 // bn
    grid = (rows // bm, n // bn)
    blocks = _nbytes((bm, k), BF16) + _nbytes((k, bn), BF16) + _nbytes((bm, bn), out_dtype)
    in_specs = [
        pl.BlockSpec((bm, k), lambda i, j: (i, 0)),
        pl.BlockSpec((None, k, bn), lambda i, j: (widx, 0, jb0 + j)),
    ]
    out_specs = [pl.BlockSpec((bm, bn), lambda i, j: (i, j))]
    out_shapes = [jax.ShapeDtypeStruct((rows, n), out_dtype)]
    args = (x, w)
    shift = 0
    if side_t is not None:
        w_t, sidx, c0, cn = side_t
        kk = w_t.shape[2]
        shift = c0 % SIDE_T_ROWS
        nxt = shift if shift else SUBLANES
        n_chunks = cn // SIDE_T_ROWS
        base_blk = (c0 - shift) // SIDE_T_ROWS
        assert cn % SIDE_T_ROWS == 0 and n_chunks <= grid[0] * grid[1] and SIDE_T_ROWS % nxt == 0

        def chunk(i, j):
            return jnp.minimum(i * grid[1] + j, n_chunks - 1)

        in_specs += [
            pl.BlockSpec((None, SIDE_T_ROWS, kk), lambda i, j: (sidx, base_blk + chunk(i, j), 0)),
            pl.BlockSpec((None, nxt, kk),
                         lambda i, j: (sidx, (base_blk + chunk(i, j) + 1) * (SIDE_T_ROWS // nxt), 0)),
        ]
        out_specs.append(pl.BlockSpec((None, kk, SIDE_T_ROWS), lambda i, j: (0, 0, chunk(i, j))))
        out_shapes.append(jax.ShapeDtypeStruct((1, kk, cn), BF16))
        blocks += _nbytes((SIDE_T_ROWS + nxt, kk), F32) + _nbytes((kk, SIDE_T_ROWS), BF16)
        args = (x, w, w_t, w_t)
    res = pl.pallas_call(
        functools.partial(_mm_kernel, sigmoid=sigmoid, shift=shift),
        grid=grid, in_specs=in_specs, out_specs=out_specs, out_shape=out_shapes,
        compiler_params=pltpu.CompilerParams(
            dimension_semantics=("arbitrary", "arbitrary"),
            vmem_limit_bytes=_vmem_limit(blocks, 0, 2 * _nbytes((bm, bn), F32) + (2 << 20))),
        name="matmul_sigmoid" if sigmoid else "matmul",
    )(*args)
    return (res[0], res[1]) if side_t is not None else (res[0], None)


def _resid_mm_kernel(x_ref, w_ref, h_ref, gate_ref, o_ref, *, weight):
    y = jnp.dot(x_ref[...], w_ref[...], preferred_element_type=F32)
    if weight != 1.0:
        o_ref[...] = h_ref[...] + weight * gate_ref[...] * y
    else:
        o_ref[...] = h_ref[...] + gate_ref[...] * y


def _resid_matmul(x, w, widx, h, rows, gate, *, weight, seq, bm, bn, side=None):
    k = x.shape[1]
    n = w.shape[2]
    blocks = (_nbytes((bm, k), BF16) + _nbytes((k, bn), BF16) + 2 * _nbytes((bm, bn), F32)
              + _nbytes((1, bn), F32))
    in_specs = [
        pl.BlockSpec((bm, k), lambda i, j: (i, 0)),
        pl.BlockSpec((None, k, bn), lambda i, j: (widx, 0, j)),
        pl.BlockSpec((bm, bn), lambda i, j: (i, j)),
        pl.BlockSpec((None, 1, bn), lambda i, j: (_mod_row_index(i, bm, seq), 0, j)),
    ]
    return _call_with_side(functools.partial(_resid_mm_kernel, weight=weight), 4, side, (rows // bm, n // bn),
                           in_specs, pl.BlockSpec((bm, bn), lambda i, j: (i, j)),
                           jax.ShapeDtypeStruct((rows, n), F32), blocks, _nbytes((bm, bn), F32),
                           "resid_matmul", (x, w, h, gate))


def _merge_kernel(oa_ref, ob_ref, oc_ref, w_ref, ga_ref, gb_ref, gc_ref, y_ref):
    y = ga_ref[...].astype(F32) * jnp.dot(oa_ref[...], w_ref[0], preferred_element_type=F32)
    y = y + gb_ref[...].astype(F32) * jnp.dot(ob_ref[...], w_ref[1], preferred_element_type=F32)
    y = y + gc_ref[...].astype(F32) * jnp.dot(oc_ref[...], w_ref[2], preferred_element_type=F32)
    y_ref[...] = y.astype(y_ref.dtype)


def _merge(oa, ob, oc, w_branch, widx, gates, rows, *, bm, bn, side=None):
    kb = oa.shape[1]
    d = w_branch.shape[3]
    per_branch = d // bn
    blocks = (3 * _nbytes((bm, kb), BF16) + _nbytes((N_BRANCH, kb, bn), BF16) + 4 * _nbytes((bm, bn), BF16))
    o_spec = pl.BlockSpec((bm, kb), lambda i, j: (i, 0))

    def gate_spec(br):
        return pl.BlockSpec((bm, bn), lambda i, j: (i, br * per_branch + j))

    in_specs = [o_spec, o_spec, o_spec,
                pl.BlockSpec((None, N_BRANCH, kb, bn), lambda i, j: (widx, 0, 0, j)),
                gate_spec(0), gate_spec(1), gate_spec(2)]
    return _call_with_side(_merge_kernel, 7, side, (rows // bm, d // bn), in_specs,
                           pl.BlockSpec((bm, bn), lambda i, j: (i, j)), jax.ShapeDtypeStruct((rows, d), BF16),
                           blocks, 4 * _nbytes((bm, bn), F32), "merge",
                           (oa, ob, oc, w_branch, gates, gates, gates))


GLA_BLOCK = 256
GLA_UNROLL = 4
GLA_BLOCKS_PER_STEP = 2
_NT = (((1,), (1,)), ((), ()))
_TN = (((0,), (0,)), ((), ()))


def _split_bf16(x):
    hi = x.astype(BF16)
    lo = (x - hi.astype(F32)).astype(BF16)
    return hi, lo


def _sum_dot(op, parts):
    acc = jnp.dot(op, parts[0], preferred_element_type=F32)
    for p in parts[1:]:
        acc = acc + jnp.dot(op, p, preferred_element_type=F32)
    return acc


def _gla_kernel(ql_ref, kl_ref, vl_ref, ogl_ref, gdl_ref, qc_ref, kc_ref, vc_ref, ogc_ref, gdc_ref,
                up_ref, gb_ref, gn_ref, al_ref, ac_ref,
                qe_ref, edec_ref, of_ref, ob_ref, kv_ref, st_ref, *, t_lat, t_ctx):
    c = GLA_CHUNK
    br = GLA_BLOCK
    cpb = br // c
    row = lax.broadcasted_iota(jnp.int32, (br, br), 0)
    col = lax.broadcasted_iota(jnp.int32, (br, br), 1)
    same = (row // c) == (col // c)
    keep = (same & (col <= row), same & (col >= row))
    tri = (keep[0].astype(BF16), keep[1].astype(BF16))
    chunk_of_row = lax.broadcasted_iota(jnp.int32, (br, GLA_DK), 0) // c
    q_scale = GLA_DK ** -0.5
    gain = gn_ref[...]
    o_refs = (of_ref, ob_ref)

    st_ref[...] = jnp.zeros_like(st_ref)

    def segment(q_ref, k_ref, v_ref, og_ref, gd_ref, out_ref, t):
        nc = t // c

        per_step = min(GLA_BLOCKS_PER_STEP, t // br)

        def block_body(r, carry):
            blocks = [r * per_step + i for i in range(per_step)]
            rows = [pl.ds(pl.multiple_of(b * br, br), br) for b in blocks]
            chains = [(i, d) for i in range(per_step) for d in range(2)]
            gd = [gd_ref[rw, :] for rw in rows]
            q = [q_ref[rw, :] * q_scale for rw in rows]
            k = [k_ref[rw, :] for rw in rows]
            v = [v_ref[rw, :].astype(BF16) for rw in rows]
            z = [jnp.dot(gd[i], up_ref[d], preferred_element_type=F32) + gb_ref[d] for i, d in chains]
            parts = [_split_bf16(jax.nn.log_sigmoid(zz) * (1.0 / GLA_GATE_NORM)) for zz in z]
            cum = [_sum_dot(tri[d], p) for (i, d), p in zip(chains, parts)]
            cum_end = []
            for (i, d), cm in zip(chains, cum):
                ends = [cm[j * c + c - 1:j * c + c] if d == 0 else cm[j * c:j * c + 1] for j in range(cpb)]
                cum_end.append(jnp.concatenate([jnp.broadcast_to(e, (c, e.shape[1])) for e in ends], axis=0))
            qe = [(q[i] * jnp.exp(cm)).astype(BF16) for (i, d), cm in zip(chains, cum)]
            kinv = [(k[i] * jnp.exp(-cm)).astype(BF16) for (i, d), cm in zip(chains, cum)]
            kdec = [(k[i] * jnp.exp(ce - cm)).astype(BF16) for (i, d), cm, ce in zip(chains, cum, cum_end)]
            a = [lax.dot_general(qq, kk, _NT, preferred_element_type=F32) for qq, kk in zip(qe, kinv)]
            kdec_by_chunk = [jnp.concatenate(
                [jnp.where(chunk_of_row == j, kd, jnp.zeros_like(kd)) for j in range(cpb)], axis=1) for kd in kdec]
            kv = [lax.dot_general(v[i], kd, _TN, preferred_element_type=F32)
                  for (i, d), kd in zip(chains, kdec_by_chunk)]
            a = [jnp.where(keep[d], aa, 0.0).astype(BF16) for (i, d), aa in zip(chains, a)]
            o = [jnp.dot(aa, v[i], preferred_element_type=F32) for (i, d), aa in zip(chains, a)]
            for n, (i, d) in enumerate(chains):
                qe_ref[d, rows[i], :] = qe[n]
                edec_ref[d, rows[i], :] = jnp.exp(cum_end[n])
                o_refs[d][rows[i], :] = o[n]
                for j in range(cpb):
                    kv_ref[d, blocks[i] * cpb + j] = kv[n][:, j * GLA_DK:(j + 1) * GLA_DK]
            return carry

        lax.fori_loop(0, t // (br * per_step), block_body, 0)

        def scan_body(n, carry):
            for d in range(2):
                m = n if d == 0 else nc - 1 - n
                rows = pl.ds(pl.multiple_of(m * c, c), c)
                st = st_ref[d]
                o_refs[d][rows, :] += lax.dot_general(qe_ref[d, rows, :], st.astype(BF16), _NT,
                                                      preferred_element_type=F32)
                st_ref[d] = st * edec_ref[d, pl.ds(pl.multiple_of(m * c, c), 1), :] + kv_ref[d, m]
            return carry

        lax.fori_loop(0, nc, scan_body, 0, unroll=GLA_UNROLL)

        def out_body(n, carry):
            rows = pl.ds(pl.multiple_of(n * c, c), c)
            o = of_ref[rows, :] + ob_ref[rows, :]
            y = o * lax.rsqrt(jnp.mean(o * o, axis=-1, keepdims=True) + EPS)
            y = (y * gain) * jax.nn.silu(og_ref[rows, :])
            out_ref[rows, :] = y.astype(out_ref.dtype)
            return carry

        lax.fori_loop(0, nc, out_body, 0, unroll=GLA_UNROLL)

    segment(qc_ref, kc_ref, vc_ref, ogc_ref, gdc_ref, ac_ref, t_ctx)
    segment(ql_ref, kl_ref, vl_ref, ogl_ref, gdl_ref, al_ref, t_lat)


def _gla(pa, pg, gate_up_pad, gate_bias, gla_norm, *, batch):
    t_lat, t_ctx = SEQ, CTX_LEN
    ctx0 = batch * t_lat // t_ctx
    dk, dv = GLA_DK, GLA_DV
    nc = t_lat // GLA_CHUNK

    def lat(width, col0):
        return pl.BlockSpec((t_lat, width), lambda b, h: (b, col0 // width + h))

    def ctx(width, col0):
        return pl.BlockSpec((t_ctx, width), lambda b, h: (ctx0 + b, col0 // width + h))

    lat_gd = pl.BlockSpec((t_lat, 128), lambda b, h: (b, 0))
    ctx_gd = pl.BlockSpec((t_ctx, 128), lambda b, h: (ctx0 + b, 0))
    blocks = ((t_lat + t_ctx) * (3 * dk + 2 * dv) * 4 + _nbytes((2, 128, dk), F32)
              + (t_lat + t_ctx) * dv * 2)
    scratch_shapes = [pltpu.VMEM((2, t_lat, dk), BF16), pltpu.VMEM((2, t_lat, dk), F32),
                      pltpu.VMEM((t_lat, dv), F32), pltpu.VMEM((t_lat, dv), F32),
                      pltpu.VMEM((2, nc, dv, dk), F32), pltpu.VMEM((2, dv, dk), F32)]
    scratch = (_nbytes((2, t_lat, dk), BF16) + _nbytes((2, t_lat, dk), F32) + 2 * _nbytes((t_lat, dv), F32)
               + _nbytes((2, nc + 1, dv, dk), F32))
    return pl.pallas_call(
        functools.partial(_gla_kernel, t_lat=t_lat, t_ctx=t_ctx),
        grid=(batch, GLA_HEADS),
        in_specs=[lat(dk, A_GQ), lat(dk, A_GK), lat(dv, A_GV), lat(dv, A_GO), lat_gd,
                  ctx(dk, A_GQ), ctx(dk, A_GK), ctx(dv, A_GV), ctx(dv, A_GO), ctx_gd,
                  pl.BlockSpec((2, 128, dk), lambda b, h: (0, 0, h)),
                  pl.BlockSpec((2, 1, dk), lambda b, h: (0, 0, h)),
                  pl.BlockSpec((1, dv), lambda b, h: (0, 0))],
        out_specs=[pl.BlockSpec((t_lat, dv), lambda b, h: (b, h)),
                   pl.BlockSpec((t_ctx, dv), lambda b, h: (b, h))],
        out_shape=[jax.ShapeDtypeStruct((batch * t_lat, GLA_HEADS * dv), BF16),
                   jax.ShapeDtypeStruct((batch * t_ctx, GLA_HEADS * dv), BF16)],
        scratch_shapes=scratch_shapes,
        compiler_params=pltpu.CompilerParams(
            dimension_semantics=("arbitrary", "arbitrary"),
            vmem_limit_bytes=_vmem_limit(blocks, scratch, 6 << 20)),
        name="gla",
    )(pa, pa, pa, pa, pg, pa, pa, pa, pa, pg, gate_up_pad, gate_bias, gla_norm)


def _rms(x, gain):
    return (x * lax.rsqrt(jnp.mean(x * x, axis=-1, keepdims=True) + EPS)) * gain


def _rope(x, cos, sin_signed):
    lane = lax.broadcasted_iota(jnp.int32, x.shape, 1)
    partner = jnp.where((lane % 64) < 32, pltpu.roll(x, 96, 1), pltpu.roll(x, 32, 1))
    return x * cos + partner * sin_signed


def _q_row_block(b, qb, *, batch, n_lat, n_ctx):
    return jnp.where(qb < n_lat, b * n_lat + qb, batch * n_lat + b * n_ctx + (qb - n_lat))


KV_PREP_ROWS = 256


def _swa_window_masks(t_lat):
    span = 3 * QBLK
    q_off = np.array([0, QBLK, 2 * QBLK])[:, None, None] + np.arange(QBLK)[None, :, None]
    k_off = np.arange(span)[None, None, :]
    return jnp.asarray(np.where(np.abs(k_off - q_off) <= SWA_WINDOW, 0.0, NEG_INF), F32)


def _swa_mask_variant(qb, n_lat):
    return jnp.where(qb == 0, 0, jnp.where(qb >= n_lat - 1, 2, 1))


def _swa_kernel(sink_ref, q_ref, kl_ref, vl_ref, kc_ref, vc_ref, cq_ref, sq_ref, ck_ref, sk_ref,
                qg_ref, kg_ref, mask_ref, o_ref, kn_ref, vb_ref, kcn_ref, vcb_ref, *, t_lat, n_lat):
    qb = pl.program_id(1)
    g = SWA_GROUP
    dh = HEAD_DIM
    scale = dh ** -0.5

    @pl.when(qb == 0)
    def _prep():
        k_gain = kg_ref[...]

        def body(r, carry):
            r0 = pl.multiple_of(r * KV_PREP_ROWS, KV_PREP_ROWS)
            rows = pl.ds(r0, KV_PREP_ROWS)
            cos, sin = ck_ref[rows, :], sk_ref[rows, :]
            for kv in range(SWA_KV_HEADS):
                cols = slice(kv * dh, (kv + 1) * dh)
                kn_ref[rows, cols] = _rope(_rms(kl_ref[rows, cols], k_gain), cos, sin).astype(BF16)
            vb_ref[rows, :] = vl_ref[rows, :].astype(BF16)
            return carry

        lax.fori_loop(0, t_lat // KV_PREP_ROWS, body, 0)
        for kv in range(SWA_KV_HEADS):
            cols = slice(kv * dh, (kv + 1) * dh)
            kcn_ref[:, cols] = _rms(kc_ref[:, cols], k_gain).astype(BF16)
        vcb_ref[...] = vc_ref[...].astype(BF16)

    def attend(latent):
        cos, sin = cq_ref[...], sq_ref[...]
        q_gain = qg_ref[...]
        span = 3 * QBLK
        start = pl.multiple_of(jnp.clip((qb - 1) * QBLK, 0, t_lat - span), QBLK)
        kvs = range(SWA_KV_HEADS)
        cols = [slice(kv * dh, (kv + 1) * dh) for kv in kvs]
        qs = []
        for kv in kvs:
            heads = [_rope(_rms(q_ref[:, (kv * g + i) * dh:(kv * g + i + 1) * dh], q_gain), cos, sin).astype(BF16)
                     for i in range(g)]
            qs.append(jnp.concatenate(heads, axis=0))
        s_ctx = [lax.dot_general(qs[kv], kcn_ref[:, cols[kv]], _NT, preferred_element_type=F32) for kv in kvs]
        if latent:
            s_loc = [lax.dot_general(qs[kv], kn_ref[pl.ds(start, span), cols[kv]], _NT,
                                     preferred_element_type=F32) for kv in kvs]
        p_loc, p_ctx, dens = ([[] for _ in kvs] for _ in range(3))
        for i in range(g):
            rows = slice(i * QBLK, (i + 1) * QBLK)
            for kv in kvs:
                sink = sink_ref[kv * g + i]
                sc = s_ctx[kv][rows] * scale
                m = jnp.maximum(jnp.max(sc, axis=-1, keepdims=True), sink)
                if latent:
                    sl = s_loc[kv][rows] * scale + mask_ref[...]
                    m = jnp.maximum(m, jnp.max(sl, axis=-1, keepdims=True))
                    pl_i = jnp.exp(sl - m)
                    p_loc[kv].append(pl_i.astype(BF16))
                pc_i = jnp.exp(sc - m)
                p_ctx[kv].append(pc_i.astype(BF16))
                den = jnp.sum(pc_i, axis=-1, keepdims=True) + jnp.exp(sink - m)
                dens[kv].append(den + jnp.sum(pl_i, axis=-1, keepdims=True) if latent else den)
        o = [jnp.dot(jnp.concatenate(p_ctx[kv], axis=0), vcb_ref[:, cols[kv]], preferred_element_type=F32)
             for kv in kvs]
        if latent:
            o = [o[kv] + jnp.dot(jnp.concatenate(p_loc[kv], axis=0), vb_ref[pl.ds(start, span), cols[kv]],
                                 preferred_element_type=F32) for kv in kvs]
        for kv in kvs:
            for i in range(g):
                o_i = o[kv][i * QBLK:(i + 1) * QBLK, :] / dens[kv][i]
                o_ref[:, (kv * g + i) * dh:(kv * g + i + 1) * dh] = o_i.astype(o_ref.dtype)

    pl.when(qb < n_lat)(functools.partial(attend, True))
    pl.when(qb >= n_lat)(functools.partial(attend, False))


def _swa(pc, sink, cos_tab, sin_tab, q_gain, k_gain, *, batch, with_ctx_out):
    t_lat, t_ctx, dh = SEQ, CTX_LEN, HEAD_DIM
    qw, kw = SWA_HEADS * dh, SWA_KV_HEADS * dh
    n_lat, n_ctx = t_lat // QBLK, t_ctx // QBLK
    n_q = n_lat + (n_ctx if with_ctx_out else 0)
    rows_out = batch * (t_lat + (t_ctx if with_ctx_out else 0))
    ctx0 = batch * t_lat // t_ctx
    qmap = functools.partial(_q_row_block, batch=batch, n_lat=n_lat, n_ctx=n_ctx)
    blocks = (_nbytes((QBLK, qw), F32) + 2 * _nbytes((t_lat + t_ctx, kw), F32) + 2 * _nbytes((QBLK, dh), F32)
              + 2 * _nbytes((t_lat, dh), F32) + _nbytes((QBLK, qw), BF16))
    scratch = 2 * _nbytes((t_lat + t_ctx, kw), BF16)
    return pl.pallas_call(
        functools.partial(_swa_kernel, t_lat=t_lat, n_lat=n_lat),
        grid=(batch, n_q),
        in_specs=[
            pl.BlockSpec(memory_space=pltpu.SMEM),
            pl.BlockSpec((QBLK, qw), lambda b, qb: (qmap(b, qb), C_SQ // qw)),
            pl.BlockSpec((t_lat, kw), lambda b, qb: (b, C_SK // kw)),
            pl.BlockSpec((t_lat, kw), lambda b, qb: (b, C_SV // kw)),
            pl.BlockSpec((t_ctx, kw), lambda b, qb: (ctx0 + b, C_SK // kw)),
            pl.BlockSpec((t_ctx, kw), lambda b, qb: (ctx0 + b, C_SV // kw)),
            pl.BlockSpec((QBLK, dh), lambda b, qb: (qb, 0)),
            pl.BlockSpec((QBLK, dh), lambda b, qb: (qb, 0)),
            pl.BlockSpec((t_lat, dh), lambda b, qb: (0, 0)),
            pl.BlockSpec((t_lat, dh), lambda b, qb: (0, 0)),
            pl.BlockSpec((1, dh), lambda b, qb: (0, 0)),
            pl.BlockSpec((1, dh), lambda b, qb: (0, 0)),
            pl.BlockSpec((None, QBLK, 3 * QBLK), lambda b, qb: (_swa_mask_variant(qb, n_lat), 0, 0)),
        ],
        out_specs=pl.BlockSpec((QBLK, qw), lambda b, qb: (qmap(b, qb), 0)),
        out_shape=jax.ShapeDtypeStruct((rows_out, qw), BF16),
        scratch_shapes=[pltpu.VMEM((t_lat, kw), BF16), pltpu.VMEM((t_lat, kw), BF16),
                        pltpu.VMEM((t_ctx, kw), BF16), pltpu.VMEM((t_ctx, kw), BF16)],
        compiler_params=pltpu.CompilerParams(
            dimension_semantics=("arbitrary", "arbitrary"),
            vmem_limit_bytes=_vmem_limit(blocks, scratch, 12 << 20)),
        name="swa",
    )(sink, pc, pc, pc, pc, pc, cos_tab, sin_tab, cos_tab, sin_tab, q_gain, k_gain, _swa_window_masks(t_lat))


NA_HEADS_PER_STEP = 4


def _na_slab_start(qb):
    r = qb * (QBLK // GRID_W)
    rows = SEQ // GRID_W
    return jnp.minimum(jnp.clip(r - NA_KH // 2, 0, rows - NA_KH), rows - NA_SLAB_ROWS)


def _na_bias_variant(qb, n_lat):
    return jnp.where(qb < 2, qb, jnp.where(qb < n_lat - 2, 2, jnp.minimum(qb, n_lat - 1) - (n_lat - 5)))


def _na_kernel(q_ref, kl_ref, vl_ref, kc_ref, vc_ref, bias_ref, qg_ref, kg_ref, o_ref,
               kn_ref, vb_ref, kcn_ref, vcb_ref, *, t_lat, n_lat):
    qb = pl.program_id(2)
    dh = HEAD_DIM
    scale = dh ** -0.5

    @pl.when(qb == 0)
    def _prep():
        k_gain = kg_ref[...]

        def body(r, carry):
            r0 = pl.multiple_of(r * KV_PREP_ROWS, KV_PREP_ROWS)
            rows = pl.ds(r0, KV_PREP_ROWS)
            for h in range(NA_HEADS_PER_STEP):
                cols = slice(h * dh, (h + 1) * dh)
                kn_ref[rows, cols] = _rms(kl_ref[rows, cols], k_gain).astype(BF16)
            vb_ref[rows, :] = vl_ref[rows, :].astype(BF16)
            return carry

        lax.fori_loop(0, t_lat // KV_PREP_ROWS, body, 0)
        for h in range(NA_HEADS_PER_STEP):
            cols = slice(h * dh, (h + 1) * dh)
            kcn_ref[:, cols] = _rms(kc_ref[:, cols], k_gain).astype(BF16)
        vcb_ref[...] = vc_ref[...].astype(BF16)

    def attend(latent):
        q_gain = qg_ref[...]
        heads = range(NA_HEADS_PER_STEP)
        cols = [slice(h * dh, (h + 1) * dh) for h in heads]
        span = NA_SLAB_ROWS * GRID_W
        start = pl.multiple_of(_na_slab_start(qb) * GRID_W, GRID_W)
        qn = [_rms(q_ref[:, cols[h]], q_gain).astype(BF16) for h in heads]
        s_ctx = [lax.dot_general(qn[h], kcn_ref[:, cols[h]], _NT, preferred_element_type=F32) * scale
                 for h in heads]
        m = [jnp.max(s, axis=-1, keepdims=True) for s in s_ctx]
        if latent:
            s_loc = [lax.dot_general(qn[h], kn_ref[pl.ds(start, span), cols[h]], _NT,
                                     preferred_element_type=F32) for h in heads]
            s_loc = [s_loc[h] * scale + bias_ref[h] for h in heads]
            m = [jnp.maximum(m[h], jnp.max(s_loc[h], axis=-1, keepdims=True)) for h in heads]
            p_loc = [jnp.exp(s_loc[h] - m[h]) for h in heads]
        p_ctx = [jnp.exp(s_ctx[h] - m[h]) for h in heads]
        o = [jnp.dot(p_ctx[h].astype(BF16), vcb_ref[:, cols[h]], preferred_element_type=F32) for h in heads]
        den = [jnp.sum(p_ctx[h], axis=-1, keepdims=True) for h in heads]
        if latent:
            o = [o[h] + jnp.dot(p_loc[h].astype(BF16), vb_ref[pl.ds(start, span), cols[h]],
                                preferred_element_type=F32) for h in heads]
            den = [den[h] + jnp.sum(p_loc[h], axis=-1, keepdims=True) for h in heads]
        for h in heads:
            o_ref[:, cols[h]] = (o[h] / den[h]).astype(o_ref.dtype)

    pl.when(qb < n_lat)(functools.partial(attend, True))
    pl.when(qb >= n_lat)(functools.partial(attend, False))


def _na(pc, bias_tab, q_gain, k_gain, *, batch, with_ctx_out):
    t_lat, t_ctx, dh = SEQ, CTX_LEN, HEAD_DIM
    hb = NA_HEADS_PER_STEP
    w = hb * dh
    n_lat, n_ctx = t_lat // QBLK, t_ctx // QBLK
    n_q = n_lat + (n_ctx if with_ctx_out else 0)
    rows_out = batch * (t_lat + (t_ctx if with_ctx_out else 0))
    ctx0 = batch * t_lat // t_ctx
    span = NA_SLAB_ROWS * GRID_W
    qmap = functools.partial(_q_row_block, batch=batch, n_lat=n_lat, n_ctx=n_ctx)
    blocks = (_nbytes((QBLK, w), F32) + 2 * _nbytes((t_lat + t_ctx, w), F32) + _nbytes((hb, QBLK, span), F32)
              + _nbytes((QBLK, w), BF16))
    scratch = 2 * _nbytes((t_lat + t_ctx, w), BF16)
    return pl.pallas_call(
        functools.partial(_na_kernel, t_lat=t_lat, n_lat=n_lat),
        grid=(batch, NA_HEADS // hb, n_q),
        in_specs=[
            pl.BlockSpec((QBLK, w), lambda b, hg, qb: (qmap(b, qb), C_NQ // w + hg)),
            pl.BlockSpec((t_lat, w), lambda b, hg, qb: (b, C_NK // w + hg)),
            pl.BlockSpec((t_lat, w), lambda b, hg, qb: (b, C_NV // w + hg)),
            pl.BlockSpec((t_ctx, w), lambda b, hg, qb: (ctx0 + b, C_NK // w + hg)),
            pl.BlockSpec((t_ctx, w), lambda b, hg, qb: (ctx0 + b, C_NV // w + hg)),
            pl.BlockSpec((hb, None, QBLK, span), lambda b, hg, qb: (hg, _na_bias_variant(qb, n_lat), 0, 0)),
            pl.BlockSpec((1, dh), lambda b, hg, qb: (0, 0)),
            pl.BlockSpec((1, dh), lambda b, hg, qb: (0, 0)),
        ],
        out_specs=pl.BlockSpec((QBLK, w), lambda b, hg, qb: (qmap(b, qb), hg)),
        out_shape=jax.ShapeDtypeStruct((rows_out, NA_HEADS * dh), BF16),
        scratch_shapes=[pltpu.VMEM((t_lat, w), BF16), pltpu.VMEM((t_lat, w), BF16),
                        pltpu.VMEM((t_ctx, w), BF16), pltpu.VMEM((t_ctx, w), BF16)],
        compiler_params=pltpu.CompilerParams(
            dimension_semantics=("arbitrary", "arbitrary", "arbitrary"),
            vmem_limit_bytes=_vmem_limit(blocks, scratch, 12 << 20)),
        name="natten",
    )(pc, pc, pc, pc, pc, bias_tab, q_gain, k_gain)


def _rope_tables():
    quarter = HEAD_DIM // 4
    pos = jnp.arange(SEQ)
    rows = (pos // GRID_W).astype(F32)
    cols = (pos % GRID_W).astype(F32)
    inv = ROPE_BASE ** (-jnp.arange(quarter, dtype=F32) / quarter)
    ang_r = rows[:, None] * inv[None, :]
    ang_c = cols[:, None] * inv[None, :]
    cos = jnp.concatenate([jnp.cos(ang_r), jnp.cos(ang_r), jnp.cos(ang_c), jnp.cos(ang_c)], axis=-1)
    sin = jnp.concatenate([-jnp.sin(ang_r), jnp.sin(ang_r), -jnp.sin(ang_c), jnp.sin(ang_c)], axis=-1)
    cos = jnp.concatenate([cos, jnp.ones((CTX_LEN, HEAD_DIM), F32)], axis=0)
    sin = jnp.concatenate([sin, jnp.zeros((CTX_LEN, HEAD_DIM), F32)], axis=0)
    return cos, sin


def _na_bias_tables(rpb):
    n_layers, n_heads = rpb.shape[:2]
    rows = SEQ // GRID_W
    n_lat = SEQ // QBLK
    rq_per = QBLK // GRID_W
    reps = np.array([0, 1, 2, n_lat - 2, n_lat - 1])
    r = reps * rq_per
    start = np.minimum(np.clip(r - NA_KH // 2, 0, rows - NA_KH), rows - NA_SLAB_ROWS)
    rq = r[:, None] + np.arange(rq_per)[None, :]
    kr = start[:, None] + np.arange(NA_SLAB_ROWS)[None, :]
    r0 = np.clip(rq - NA_KH // 2, 0, rows - NA_KH)
    valid_r = (kr[:, None, :] >= r0[:, :, None]) & (kr[:, None, :] < r0[:, :, None] + NA_KH)
    dr = np.clip(kr[:, None, :] - rq[:, :, None] + NA_KH - 1, 0, 2 * NA_KH - 2)
    qc = np.arange(GRID_W)
    kc = np.arange(GRID_W)
    cs = np.clip(qc - NA_KW // 2, 0, GRID_W - NA_KW)
    valid_c = (kc[None, :] >= cs[:, None]) & (kc[None, :] < cs[:, None] + NA_KW)
    dc = np.clip(kc[None, :] - qc[:, None] + NA_KW - 1, 0, 2 * NA_KW - 2)
    pick_r = np.eye(2 * NA_KH - 1, dtype=np.float32)[dr.reshape(-1)]
    pick_c = np.eye(2 * NA_KW - 1, dtype=np.float32)[dc.reshape(-1)]
    t = jnp.einsum("nr,lhrc->lhnc", pick_r, rpb.astype(F32), precision=HIGHEST)
    t = jnp.einsum("lhnc,xc->lhnx", t, pick_c, precision=HIGHEST)
    t = t.reshape(n_layers, n_heads, len(reps), rq_per, NA_SLAB_ROWS, GRID_W, GRID_W)
    t = t.transpose(0, 1, 2, 3, 5, 4, 6)
    valid = valid_r[:, :, None, :, None] & valid_c[None, None, :, None, :]
    t = jnp.where(valid[None, None], t, NEG_INF)
    return t.reshape(n_layers, n_heads, len(reps), QBLK, NA_SLAB_ROWS * GRID_W)


SUBLANES = 8
CAST_K = 2048
CAST_CHUNK = 256


def _cast_cols_kernel(main_ref, next_ref, o_ref, *, shift):
    bn = o_ref.shape[1]
    for c in range(o_ref.shape[0] // CAST_CHUNK):
        cols = slice(c * CAST_CHUNK, (c + 1) * CAST_CHUNK)
        if shift == 0:
            x = main_ref[:, cols]
        else:
            x = jnp.concatenate([main_ref[shift:bn, cols], next_ref[:, cols]], axis=0)
        o_ref[cols, :] = x.T.astype(o_ref.dtype)


def _cast_cols(w_t, n_layers, col0, n, *, bn):
    k = w_t.shape[2]
    shift = col0 % bn
    base = col0 - shift
    nxt = shift if shift else SUBLANES
    assert n % bn == 0 and k % CAST_K == 0 and nxt % SUBLANES == 0 and bn % nxt == 0 and base % nxt == 0
    blocks = _nbytes((bn + nxt, CAST_K), F32) + _nbytes((CAST_K, bn), BF16)
    return pl.pallas_call(
        functools.partial(_cast_cols_kernel, shift=shift),
        grid=(n_layers, k // CAST_K, n // bn),
        in_specs=[
            pl.BlockSpec((None, bn, CAST_K), lambda l, i, j: (l, base // bn + j, i)),
            pl.BlockSpec((None, nxt, CAST_K), lambda l, i, j: (l, (base + (j + 1) * bn) // nxt, i)),
        ],
        out_specs=pl.BlockSpec((None, CAST_K, bn), lambda l, i, j: (l, i, j)),
        out_shape=jax.ShapeDtypeStruct((n_layers, k, n), BF16),
        compiler_params=pltpu.CompilerParams(
            dimension_semantics=("arbitrary", "arbitrary", "arbitrary"),
            vmem_limit_bytes=_vmem_limit(blocks, 0, 4 << 20)),
        name="cast_cols",
    )(w_t, w_t)


def _pad_gate_up(gate_up):
    r = gate_up.shape[1]
    out = jnp.zeros((2, 128, gate_up.shape[2]), F32)
    out = out.at[0, 0:r].set(gate_up[0])
    out = out.at[1, r:2 * r].set(gate_up[1])
    return out


BM = 1024
BM_SMALL = 256
BM_NORM = 512
BN = 512
BN_FF = 512
BN_A = 640
BN_QKVG = 1024
BN_C = 768
BN_MERGE = 512


def kernel(x, c, ctx, c_ctx, ada_down, ada_up, ada_bias, norm_gain, ffn_w_in, ffn_w_out, w_in,
           gla_gate_up, gla_gate_bias, gla_norm, swa_q_norm, swa_k_norm, swa_sink, na_q_norm,
           na_k_norm, na_rpb, w_branch, w_out):
    batch, seq, d = x.shape
    depth = ada_down.shape[0]
    assert (seq, d, ctx.shape[1]) == (SEQ, D_MODEL, CTX_LEN) and batch + 1 <= MOD_ROWS
    assert w_in.shape[2] == C_START + C_QKV_COLS + GATE_COLS
    lat_rows = batch * seq
    all_rows = lat_rows + batch * ctx.shape[1]

    v8 = jnp.concatenate([c, c_ctx[None], jnp.zeros((MOD_ROWS - batch - 1, d), F32)], axis=0)
    mods = _adaln(v8, ada_down, ada_up, ada_bias)
    mods = mods.reshape(depth, N_MOD, MOD_ROWS, 1, d)
    cos_tab, sin_tab = _rope_tables()
    na_bias = _na_bias_tables(na_rpb)

    ffn_in_f = ffn_w_in.reshape(depth * 2, d, -1)
    ffn_out_f = ffn_w_out.reshape(depth * 2, -1, d)
    w_branch_f = w_branch.reshape(depth, N_BRANCH * BRANCH_W, d)
    ffn_in_a = ffn_in_f[0:1].astype(BF16)
    ffn_out_a = ffn_out_f[0:1].astype(BF16)
    w_branch_bf = w_branch_f[0:1].astype(BF16)
    w_out_bf = w_out[0:1].astype(BF16)
    w_in_t = jnp.swapaxes(w_in, 1, 2)
    w_a_bf = _cast_cols(w_in_t, 1, 0, A_COLS, bn=BN_A)
    w_c_bf = _cast_cols(w_in_t, 1, C_START, C_QKV_COLS + GATE_COLS, bn=BN_C)

    h = jnp.concatenate([x.reshape(lat_rows, d), ctx.reshape(-1, d)], axis=0)
    bm = BM if all_rows % BM == 0 and lat_rows % BM == 0 else BM_SMALL
    mm = dict(seq=seq, bm=bm)
    norm = dict(seq=seq, bm=BM_NORM if all_rows % BM_NORM == 0 and lat_rows % BM_NORM == 0 else BM_SMALL)

    for l in range(depth):
        last = l == depth - 1
        rows_out = lat_rows if last else all_rows
        gain = norm_gain[l].reshape(3, 1, d)
        m = mods[l]

        def next_layer(src, idx):
            return None if last else (src, idx)

        xn = _modulate(h, all_rows, gain[0], m[0], m[1], **norm)
        g1, ffn_in_b = _swiglu(xn, ffn_in_a, 0, all_rows, bm=bm, bn=BN_FF, side=(ffn_in_f, 2 * l + 1))
        h, ffn_out_b = _resid_matmul(g1, ffn_out_a, 0, h, all_rows, m[2], weight=MACARON_W, bn=BN,
                                     side=(ffn_out_f, 2 * l + 1), **mm)

        xn = _modulate(h, all_rows, gain[1], m[3], m[4], **norm)
        pa, w_a_next = _matmul(xn, w_a_bf, 0, all_rows, 0, A_GD, bm=bm, bn=BN_QKVG, out_dtype=F32,
                               side_t=None if last else (w_in_t, l + 1, 0, A_COLS))
        pg, _ = _matmul(xn, w_a_bf, 0, all_rows, A_GD, A_COLS - A_GD, bm=bm, bn=A_COLS - A_GD, out_dtype=F32)
        pc, _ = _matmul(xn, w_c_bf, 0, all_rows, 0, C_QKV_COLS, bm=bm, bn=BN_C, out_dtype=F32)
        gates, w_c_next = _matmul(xn, w_c_bf, 0, rows_out, C_QKV_COLS, GATE_COLS, bm=bm, bn=BN_C,
                                  out_dtype=BF16, sigmoid=True,
                                  side_t=None if last else (w_in_t, l + 1, C_START, C_QKV_COLS + GATE_COLS))
        a_lat, a_ctx = _gla(pa, pg, _pad_gate_up(gla_gate_up[l]), gla_gate_bias[l].reshape(2, 1, -1),
                            gla_norm[l].reshape(1, -1), batch=batch)
        o_a = a_lat if last else jnp.concatenate([a_lat, a_ctx], axis=0)
        o_b = _swa(pc, swa_sink[l], cos_tab, sin_tab, swa_q_norm[l].reshape(1, -1), swa_k_norm[l].reshape(1, -1),
                   batch=batch, with_ctx_out=not last)
        o_c = _na(pc, na_bias[l], na_q_norm[l].reshape(1, -1), na_k_norm[l].reshape(1, -1),
                  batch=batch, with_ctx_out=not last)
        y, w_branch_next = _merge(o_a, o_b, o_c, w_branch_bf.reshape(1, N_BRANCH, BRANCH_W, d), 0, gates, rows_out,
                                  bm=bm, bn=BN_MERGE, side=next_layer(w_branch_f, l + 1))
        h, w_out_next = _resid_matmul(y, w_out_bf, 0, h, rows_out, m[5], weight=1.0, bn=BN,
                                      side=next_layer(w_out, l + 1), **mm)

        xn = _modulate(h, rows_out, gain[2], m[6], m[7], **norm)
        g2, ffn_in_a = _swiglu(xn, ffn_in_b, 0, rows_out, bm=bm, bn=BN_FF, side=next_layer(ffn_in_f, 2 * l + 2))
        h, ffn_out_a = _resid_matmul(g2, ffn_out_b, 0, h, rows_out, m[8], weight=MACARON_W, bn=BN,
                                     side=next_layer(ffn_out_f, 2 * l + 2), **mm)
        w_branch_bf, w_out_bf, w_a_bf, w_c_bf = w_branch_next, w_out_next, w_a_next, w_c_next

    return h.reshape(batch, seq, d)
```

```python
import functools

import jax
import jax.numpy as jnp
import numpy as np
from jax import lax
from jax.experimental import pallas as pl
from jax.experimental.pallas import tpu as pltpu

F32 = jnp.float32
BF16 = jnp.bfloat16
HIGHEST = lax.Precision.HIGHEST

D_MODEL = 4096
SEQ = 2048
CTX_LEN = 256
GRID_W = 64
HEAD_DIM = 128
GLA_HEADS = 4
GLA_DK = 128
GLA_DV = 256
GLA_GATE_RANK = 16
GLA_GATE_NORM = 16.0
GLA_CHUNK = 64
SWA_HEADS = 8
SWA_KV_HEADS = 2
SWA_GROUP = SWA_HEADS // SWA_KV_HEADS
SWA_WINDOW = 128
NA_HEADS = 8
NA_KH = 8
NA_KW = 16
N_BRANCH = 3
BRANCH_W = 1024
D_FF = 4096
MACARON_W = 0.5
N_MOD = 9
ROPE_BASE = 10000.0
EPS = 1e-6
NEG_INF = -1e30
MOD_ROWS = 8

A_GQ, A_GK, A_GV, A_GO, A_GD = 0, 512, 1024, 2048, 3072
A_COLS = 3200
C_START = 3104
C_SQ, C_SK, C_SV = 0, 1024, 1280
C_NQ, C_NK, C_NV = 1536, 2560, 3584
C_QKV_COLS = 4608
GATE_COLS = N_BRANCH * D_MODEL

V7X_VMEM_BYTES = 64 * 1024 * 1024
V7X_VMEM_RESERVE = 6 * 1024 * 1024
QBLK = 128
NA_SLAB_ROWS = 10


def _vmem_limit(pipelined_bytes, scratch_bytes=0, temp_bytes=0):
    need = 2 * pipelined_bytes + scratch_bytes + temp_bytes + (4 << 20)
    return int(min(max(need, 16 << 20), V7X_VMEM_BYTES - V7X_VMEM_RESERVE))


def _nbytes(shape, dtype):
    return int(np.prod(shape)) * jnp.dtype(dtype).itemsize


def _adaln_kernel(v_ref, down_ref, up_ref, bias_ref, o_ref, t_ref):
    @pl.when(pl.program_id(1) == 0)
    def _():
        t_ref[...] = jnp.dot(jax.nn.silu(v_ref[...]), down_ref[...], preferred_element_type=F32,
                             precision=HIGHEST)

    o_ref[...] = jnp.dot(t_ref[...], up_ref[...], preferred_element_type=F32, precision=HIGHEST) + bias_ref[...]


def _adaln(v8, ada_down, ada_up, ada_bias):
    depth, d, rank = ada_down.shape
    blocks = (_nbytes((MOD_ROWS, d), F32) + _nbytes((d, rank), F32) + _nbytes((rank, d), F32)
              + _nbytes((1, d), F32) + _nbytes((MOD_ROWS, d), F32))
    return pl.pallas_call(
        _adaln_kernel,
        grid=(depth, N_MOD),
        in_specs=[
            pl.BlockSpec((MOD_ROWS, d), lambda l, j: (0, 0)),
            pl.BlockSpec((None, d, rank), lambda l, j: (l, 0, 0)),
            pl.BlockSpec((None, rank, d), lambda l, j: (l, 0, j)),
            pl.BlockSpec((None, 1, d), lambda l, j: (l, 0, j)),
        ],
        out_specs=pl.BlockSpec((None, None, MOD_ROWS, d), lambda l, j: (l, j, 0, 0)),
        out_shape=jax.ShapeDtypeStruct((depth, N_MOD, MOD_ROWS, d), F32),
        scratch_shapes=[pltpu.VMEM((MOD_ROWS, rank), F32)],
        compiler_params=pltpu.CompilerParams(
            dimension_semantics=("arbitrary", "arbitrary"), vmem_limit_bytes=_vmem_limit(blocks)),
        name="adaln",
    )(v8, ada_down, ada_up, ada_bias.reshape(depth, 1, -1))


def _mod_row_index(i, bm, seq):
    return jnp.minimum((i * bm) // seq, MOD_ROWS - 1)


def _mod_spec(bm, seq, d):
    return pl.BlockSpec((None, 1, d), lambda i, j: (_mod_row_index(i, bm, seq), 0, 0))


NORM_ROWS = 32


def _row_specs(a, a_ctx, bm, bn, col=lambda j: 0):
    if a_ctx is None:
        return [pl.BlockSpec((bm, bn), lambda i, j: (i, col(j)))], None
    n_top = a.shape[0] // bm
    assert a.shape[0] % bm == 0 and a_ctx.shape[0] % bm == 0
    return [pl.BlockSpec((bm, bn), lambda i, j: (jnp.minimum(i, n_top - 1), col(j))),
            pl.BlockSpec((bm, bn), lambda i, j: (jnp.maximum(i - n_top, 0), col(j)))], n_top


def _rows_of(top_ref, ctx_ref, n_top, idx=(slice(None), slice(None))):
    if ctx_ref is None:
        return top_ref[idx]
    return jnp.where(pl.program_id(0) < n_top, top_ref[idx], ctx_ref[idx])


def _modulate_kernel(*refs, bm, n_top):
    if n_top is None:
        (h_ref, g_ref, sh_ref, sc_ref, xn_ref, mult_ref), hc_ref = refs, None
    else:
        h_ref, hc_ref, g_ref, sh_ref, sc_ref, xn_ref, mult_ref = refs
    mult_ref[...] = g_ref[...] * (1.0 + sc_ref[...])

    def body(c, carry):
        r0 = pl.multiple_of(c * NORM_ROWS, NORM_ROWS)
        x = _rows_of(h_ref, hc_ref, n_top, (pl.ds(r0, NORM_ROWS), slice(None)))
        r = lax.rsqrt(jnp.mean(x * x, axis=-1, keepdims=True) + EPS)
        xn_ref[pl.ds(r0, NORM_ROWS), :] = ((x * r) * mult_ref[...] + sh_ref[...]).astype(BF16)
        return carry

    lax.fori_loop(0, bm // NORM_ROWS, body, 0, unroll=2)


def _modulate(h, rows, gain, shift, scale, *, seq, bm, h_ctx=None):
    d = h.shape[1]
    blocks = 2 * _nbytes((bm, d), F32) + _nbytes((bm, d), BF16) + 3 * _nbytes((1, d), F32)
    h_specs, n_top = _row_specs(h, h_ctx, bm, d)
    return pl.pallas_call(
        functools.partial(_modulate_kernel, bm=bm, n_top=n_top),
        grid=(rows // bm, 1),
        in_specs=h_specs + [
            pl.BlockSpec((1, d), lambda i, j: (0, 0)),
            _mod_spec(bm, seq, d),
            _mod_spec(bm, seq, d),
        ],
        out_specs=pl.BlockSpec((bm, d), lambda i, j: (i, 0)),
        out_shape=jax.ShapeDtypeStruct((rows, d), BF16),
        scratch_shapes=[pltpu.VMEM((1, d), F32)],
        compiler_params=pltpu.CompilerParams(
            dimension_semantics=("arbitrary", "arbitrary"), vmem_limit_bytes=_vmem_limit(blocks)),
        name="modulate",
    )(*((h,) if h_ctx is None else (h, h_ctx)), gain, shift, scale)


SIDE_ROWS = 64


def _side_cast(side, n_steps, nj):
    src, sidx = side
    _, r, cdim = src.shape
    side_rows = SIDE_ROWS
    while r // side_rows > n_steps:
        side_rows *= 2
    n_chunks = r // side_rows
    assert r % side_rows == 0

    def chunk(i, j):
        return jnp.minimum(i * nj + j, n_chunks - 1)

    in_spec = pl.BlockSpec((None, side_rows, cdim), lambda i, j: (sidx, chunk(i, j), 0))
    out_spec = pl.BlockSpec((None, side_rows, cdim), lambda i, j: (0, chunk(i, j), 0))
    nbytes = _nbytes((side_rows, cdim), F32) + _nbytes((side_rows, cdim), BF16)
    return in_spec, out_spec, jax.ShapeDtypeStruct((1, r, cdim), BF16), nbytes


def _with_side(body, n_in, n_out):
    def kernel(*refs):
        if len(refs) == n_in + n_out:
            body(*refs)
        else:
            body(*refs[:n_in], *refs[n_in + 1:n_in + 1 + n_out])
            refs[-1][...] = refs[n_in][...].astype(refs[-1].dtype)
    return kernel


def _call_with_side(body, n_in, side, grid, in_specs, out_spec, out_shape, blocks, temp, name, args):
    out_specs, out_shapes = [out_spec], [out_shape]
    if side is not None:
        s_in, s_out, s_shape, s_bytes = _side_cast(side, grid[0] * grid[1], grid[1])
        in_specs, out_specs, out_shapes = in_specs + [s_in], out_specs + [s_out], out_shapes + [s_shape]
        blocks, args = blocks + s_bytes, args + (side[0],)
    res = pl.pallas_call(
        _with_side(body, n_in, 1),
        grid=grid, in_specs=in_specs, out_specs=out_specs, out_shape=out_shapes,
        compiler_params=pltpu.CompilerParams(
            dimension_semantics=("arbitrary", "arbitrary"), vmem_limit_bytes=_vmem_limit(blocks, 0, temp)),
        name=name,
    )(*args)
    return (res[0], res[1]) if side is not None else (res[0], None)


def _swiglu_kernel(x_ref, wa_ref, wb_ref, o_ref):
    x = x_ref[...]
    a = jnp.dot(x, wa_ref[...], preferred_element_type=F32)
    b = jnp.dot(x, wb_ref[...], preferred_element_type=F32)
    o_ref[...] = (jax.nn.silu(a) * b).astype(o_ref.dtype)


def _swiglu(x, w_in, widx, rows, *, bm, bn, side=None):
    d = x.shape[1]
    f = w_in.shape[2] // 2
    nb = f // bn
    blocks = _nbytes((bm, d), BF16) + 2 * _nbytes((d, bn), BF16) + _nbytes((bm, bn), BF16)
    in_specs = [
        pl.BlockSpec((bm, d), lambda i, j: (i, 0)),
        pl.BlockSpec((None, d, bn), lambda i, j: (widx, 0, j)),
        pl.BlockSpec((None, d, bn), lambda i, j: (widx, 0, j + nb)),
    ]
    return _call_with_side(_swiglu_kernel, 3, side, (rows // bm, nb), in_specs,
                           pl.BlockSpec((bm, bn), lambda i, j: (i, j)), jax.ShapeDtypeStruct((rows, f), BF16),
                           blocks, 4 * _nbytes((bm, bn), F32), "swiglu", (x, w_in, w_in))


SIDE_T_ROWS = 128
SIDE_T_CHUNK = 512


def _mm_kernel(*refs, sigmoid, shift):
    if len(refs) == 3:
        x_ref, w_ref, o_ref = refs
    else:
        x_ref, w_ref, main_ref, next_ref, o_ref, dst_ref = refs
    y = jnp.dot(x_ref[...], w_ref[...], preferred_element_type=F32)
    if sigmoid:
        y = jax.nn.sigmoid(y)
    o_ref[...] = y.astype(o_ref.dtype)
    if len(refs) == 6:
        for c in range(dst_ref.shape[0] // SIDE_T_CHUNK):
            cols = slice(c * SIDE_T_CHUNK, (c + 1) * SIDE_T_CHUNK)
            if shift == 0:
                blk = main_ref[:, cols]
            else:
                blk = jnp.concatenate([main_ref[shift:, cols], next_ref[:, cols]], axis=0)
            dst_ref[cols, :] = blk.T.astype(dst_ref.dtype)


def _matmul(x, w, widx, rows, col0, n, *, bm, bn, out_dtype, sigmoid=False, side_t=None):
    k = x.shape[1]
    jb0 = col0 // bn
    grid = (rows // bm, n // bn)
    blocks = _nbytes((bm, k), BF16) + _nbytes((k, bn), BF16) + _nbytes((bm, bn), out_dtype)
    in_specs = [
        pl.BlockSpec((bm, k), lambda i, j: (i, 0)),
        pl.BlockSpec((None, k, bn), lambda i, j: (widx, 0, jb0 + j)),
    ]
    out_specs = [pl.BlockSpec((bm, bn), lambda i, j: (i, j))]
    out_shapes = [jax.ShapeDtypeStruct((rows, n), out_dtype)]
    args = (x, w)
    shift = 0
    if side_t is not None:
        w_t, sidx, c0, cn = side_t
        kk = w_t.shape[2]
        shift = c0 % SIDE_T_ROWS
        nxt = shift if shift else SUBLANES
        n_chunks = cn // SIDE_T_ROWS
        base_blk = (c0 - shift) // SIDE_T_ROWS
        assert cn % SIDE_T_ROWS == 0 and n_chunks <= grid[0] * grid[1] and SIDE_T_ROWS % nxt == 0

        def chunk(i, j):
            return jnp.minimum(i * grid[1] + j, n_chunks - 1)

        in_specs += [
            pl.BlockSpec((None, SIDE_T_ROWS, kk), lambda i, j: (sidx, base_blk + chunk(i, j), 0)),
            pl.BlockSpec((None, nxt, kk),
                         lambda i, j: (sidx, (base_blk + chunk(i, j) + 1) * (SIDE_T_ROWS // nxt), 0)),
        ]
        out_specs.append(pl.BlockSpec((None, kk, SIDE_T_ROWS), lambda i, j: (0, 0, chunk(i, j))))
        out_shapes.append(jax.ShapeDtypeStruct((1, kk, cn), BF16))
        blocks += _nbytes((SIDE_T_ROWS + nxt, kk), F32) + _nbytes((kk, SIDE_T_ROWS), BF16)
        args = (x, w, w_t, w_t)
    res = pl.pallas_call(
        functools.partial(_mm_kernel, sigmoid=sigmoid, shift=shift),
        grid=grid, in_specs=in_specs, out_specs=out_specs, out_shape=out_shapes,
        compiler_params=pltpu.CompilerParams(
            dimension_semantics=("arbitrary", "arbitrary"),
            vmem_limit_bytes=_vmem_limit(blocks, 0, 2 * _nbytes((bm, bn), F32) + (2 << 20))),
        name="matmul_sigmoid" if sigmoid else "matmul",
    )(*args)
    return (res[0], res[1]) if side_t is not None else (res[0], None)


def _resid_mm_kernel(*refs, weight, n_top):
    if n_top is None:
        (x_ref, w_ref, h_ref, gate_ref, o_ref), hc_ref = refs, None
    else:
        x_ref, w_ref, h_ref, hc_ref, gate_ref, o_ref = refs
    y = jnp.dot(x_ref[...], w_ref[...], preferred_element_type=F32)
    h = _rows_of(h_ref, hc_ref, n_top)
    if weight != 1.0:
        o_ref[...] = h + weight * gate_ref[...] * y
    else:
        o_ref[...] = h + gate_ref[...] * y


def _resid_matmul(x, w, widx, h, rows, gate, *, weight, seq, bm, bn, side=None, h_ctx=None):
    k = x.shape[1]
    n = w.shape[2]
    blocks = (_nbytes((bm, k), BF16) + _nbytes((k, bn), BF16) + 3 * _nbytes((bm, bn), F32)
              + _nbytes((1, bn), F32))
    h_specs, n_top = _row_specs(h, h_ctx, bm, bn, lambda j: j)
    in_specs = [
        pl.BlockSpec((bm, k), lambda i, j: (i, 0)),
        pl.BlockSpec((None, k, bn), lambda i, j: (widx, 0, j)),
    ] + h_specs + [
        pl.BlockSpec((None, 1, bn), lambda i, j: (_mod_row_index(i, bm, seq), 0, j)),
    ]
    args = (x, w, h, gate) if h_ctx is None else (x, w, h, h_ctx, gate)
    return _call_with_side(functools.partial(_resid_mm_kernel, weight=weight, n_top=n_top), len(args), side,
                           (rows // bm, n // bn), in_specs, pl.BlockSpec((bm, bn), lambda i, j: (i, j)),
                           jax.ShapeDtypeStruct((rows, n), F32), blocks, _nbytes((bm, bn), F32),
                           "resid_matmul", args)


def _merge_kernel(*refs, n_top):
    if n_top is None:
        (oa_ref, ob_ref, oc_ref, w_ref, ga_ref, gb_ref, gc_ref, y_ref), oac_ref = refs, None
    else:
        oa_ref, oac_ref, ob_ref, oc_ref, w_ref, ga_ref, gb_ref, gc_ref, y_ref = refs
    oa = _rows_of(oa_ref, oac_ref, n_top)
    y = ga_ref[...].astype(F32) * jnp.dot(oa, w_ref[0], preferred_element_type=F32)
    y = y + gb_ref[...].astype(F32) * jnp.dot(ob_ref[...], w_ref[1], preferred_element_type=F32)
    y = y + gc_ref[...].astype(F32) * jnp.dot(oc_ref[...], w_ref[2], preferred_element_type=F32)
    y_ref[...] = y.astype(y_ref.dtype)


def _merge(oa, ob, oc, w_branch, widx, gates, rows, *, bm, bn, side=None, oa_ctx=None):
    kb = oa.shape[1]
    d = w_branch.shape[3]
    per_branch = d // bn
    blocks = (4 * _nbytes((bm, kb), BF16) + _nbytes((N_BRANCH, kb, bn), BF16) + 4 * _nbytes((bm, bn), BF16))
    o_spec = pl.BlockSpec((bm, kb), lambda i, j: (i, 0))
    oa_specs, n_top = _row_specs(oa, oa_ctx, bm, kb)

    def gate_spec(br):
        return pl.BlockSpec((bm, bn), lambda i, j: (i, br * per_branch + j))

    in_specs = oa_specs + [o_spec, o_spec,
                           pl.BlockSpec((None, N_BRANCH, kb, bn), lambda i, j: (widx, 0, 0, j)),
                           gate_spec(0), gate_spec(1), gate_spec(2)]
    args = ((oa,) if oa_ctx is None else (oa, oa_ctx)) + (ob, oc, w_branch, gates, gates, gates)
    return _call_with_side(functools.partial(_merge_kernel, n_top=n_top), len(args), side,
                           (rows // bm, d // bn), in_specs, pl.BlockSpec((bm, bn), lambda i, j: (i, j)),
                           jax.ShapeDtypeStruct((rows, d), BF16), blocks, 4 * _nbytes((bm, bn), F32), "merge",
                           args)


GLA_BLOCK = 256
GLA_UNROLL = 4
GLA_BLOCKS_PER_STEP = 2
_NT = (((1,), (1,)), ((), ()))
_TN = (((0,), (0,)), ((), ()))


def _split_bf16(x):
    hi = x.astype(BF16)
    lo = (x - hi.astype(F32)).astype(BF16)
    return hi, lo


def _sum_dot(op, parts):
    acc = jnp.dot(op, parts[0], preferred_element_type=F32)
    for p in parts[1:]:
        acc = acc + jnp.dot(op, p, preferred_element_type=F32)
    return acc


def _gla_kernel(ql_ref, kl_ref, vl_ref, ogl_ref, gdl_ref, qc_ref, kc_ref, vc_ref, ogc_ref, gdc_ref,
                up_ref, gb_ref, gn_ref, al_ref, ac_ref,
                qe_ref, edec_ref, of_ref, ob_ref, kv_ref, st_ref, *, t_lat, t_ctx):
    c = GLA_CHUNK
    br = GLA_BLOCK
    cpb = br // c
    row = lax.broadcasted_iota(jnp.int32, (br, br), 0)
    col = lax.broadcasted_iota(jnp.int32, (br, br), 1)
    same = (row // c) == (col // c)
    keep = (same & (col <= row), same & (col >= row))
    tri = (keep[0].astype(BF16), keep[1].astype(BF16))
    chunk_of_row = lax.broadcasted_iota(jnp.int32, (br, GLA_DK), 0) // c
    q_scale = GLA_DK ** -0.5
    gain = gn_ref[...]
    o_refs = (of_ref, ob_ref)

    st_ref[...] = jnp.zeros_like(st_ref)

    def segment(q_ref, k_ref, v_ref, og_ref, gd_ref, out_ref, t):
        nc = t // c

        per_step = min(GLA_BLOCKS_PER_STEP, t // br)

        def block_body(r, carry):
            blocks = [r * per_step + i for i in range(per_step)]
            rows = [pl.ds(pl.multiple_of(b * br, br), br) for b in blocks]
            chains = [(i, d) for i in range(per_step) for d in range(2)]
            gd = [gd_ref[rw, :] for rw in rows]
            q = [q_ref[rw, :] * q_scale for rw in rows]
            k = [k_ref[rw, :] for rw in rows]
            v = [v_ref[rw, :].astype(BF16) for rw in rows]
            z = [jnp.dot(gd[i], up_ref[d], preferred_element_type=F32) + gb_ref[d] for i, d in chains]
            parts = [_split_bf16(jax.nn.log_sigmoid(zz) * (1.0 / GLA_GATE_NORM)) for zz in z]
            cum = [_sum_dot(tri[d], p) for (i, d), p in zip(chains, parts)]
            cum_end = []
            for (i, d), cm in zip(chains, cum):
                ends = [cm[j * c + c - 1:j * c + c] if d == 0 else cm[j * c:j * c + 1] for j in range(cpb)]
                cum_end.append(jnp.concatenate([jnp.broadcast_to(e, (c, e.shape[1])) for e in ends], axis=0))
            qe = [(q[i] * jnp.exp(cm)).astype(BF16) for (i, d), cm in zip(chains, cum)]
            kinv = [(k[i] * jnp.exp(-cm)).astype(BF16) for (i, d), cm in zip(chains, cum)]
            kdec = [(k[i] * jnp.exp(ce - cm)).astype(BF16) for (i, d), cm, ce in zip(chains, cum, cum_end)]
            a = [lax.dot_general(qq, kk, _NT, preferred_element_type=F32) for qq, kk in zip(qe, kinv)]
            kdec_by_chunk = [jnp.concatenate(
                [jnp.where(chunk_of_row == j, kd, jnp.zeros_like(kd)) for j in range(cpb)], axis=1) for kd in kdec]
            kv = [lax.dot_general(v[i], kd, _TN, preferred_element_type=F32)
                  for (i, d), kd in zip(chains, kdec_by_chunk)]
            a = [jnp.where(keep[d], aa, 0.0).astype(BF16) for (i, d), aa in zip(chains, a)]
            o = [jnp.dot(aa, v[i], preferred_element_type=F32) for (i, d), aa in zip(chains, a)]
            for n, (i, d) in enumerate(chains):
                qe_ref[d, rows[i], :] = qe[n]
                edec_ref[d, rows[i], :] = jnp.exp(cum_end[n])
                o_refs[d][rows[i], :] = o[n]
                for j in range(cpb):
                    kv_ref[d, blocks[i] * cpb + j] = kv[n][:, j * GLA_DK:(j + 1) * GLA_DK]
            return carry

        lax.fori_loop(0, t // (br * per_step), block_body, 0)

        def scan_body(n, carry):
            ms = (n, nc - 1 - n)
            rows = [pl.ds(pl.multiple_of(m * c, c), c) for m in ms]
            st = [st_ref[d] for d in range(2)]
            inter = [lax.dot_general(qe_ref[d, rows[d], :], st[d].astype(BF16), _NT, preferred_element_type=F32)
                     for d in range(2)]
            for d in range(2):
                st_ref[d] = st[d] * edec_ref[d, pl.ds(pl.multiple_of(ms[d] * c, c), 1), :] + kv_ref[d, ms[d]]
            for d in range(2):
                o_refs[d][rows[d], :] += inter[d]
            return carry

        lax.fori_loop(0, nc, scan_body, 0, unroll=GLA_UNROLL)

        def out_body(n, carry):
            rows = pl.ds(pl.multiple_of(n * c, c), c)
            o = of_ref[rows, :] + ob_ref[rows, :]
            y = o * lax.rsqrt(jnp.mean(o * o, axis=-1, keepdims=True) + EPS)
            y = (y * gain) * jax.nn.silu(og_ref[rows, :])
            out_ref[rows, :] = y.astype(out_ref.dtype)
            return carry

        lax.fori_loop(0, nc, out_body, 0, unroll=GLA_UNROLL)

    segment(qc_ref, kc_ref, vc_ref, ogc_ref, gdc_ref, ac_ref, t_ctx)
    segment(ql_ref, kl_ref, vl_ref, ogl_ref, gdl_ref, al_ref, t_lat)


def _gla(pa, pg, gate_up_pad, gate_bias, gla_norm, *, batch):
    t_lat, t_ctx = SEQ, CTX_LEN
    ctx0 = batch * t_lat // t_ctx
    dk, dv = GLA_DK, GLA_DV
    nc = t_lat // GLA_CHUNK

    def lat(width, col0):
        return pl.BlockSpec((t_lat, width), lambda b, h: (b, col0 // width + h))

    def ctx(width, col0):
        return pl.BlockSpec((t_ctx, width), lambda b, h: (ctx0 + b, col0 // width + h))

    lat_gd = pl.BlockSpec((t_lat, 128), lambda b, h: (b, 0))
    ctx_gd = pl.BlockSpec((t_ctx, 128), lambda b, h: (ctx0 + b, 0))
    blocks = ((t_lat + t_ctx) * (3 * dk + 2 * dv) * 4 + _nbytes((2, 128, dk), F32)
              + (t_lat + t_ctx) * dv * 2)
    scratch_shapes = [pltpu.VMEM((2, t_lat, dk), BF16), pltpu.VMEM((2, t_lat, dk), F32),
                      pltpu.VMEM((t_lat, dv), F32), pltpu.VMEM((t_lat, dv), F32),
                      pltpu.VMEM((2, nc, dv, dk), F32), pltpu.VMEM((2, dv, dk), F32)]
    scratch = (_nbytes((2, t_lat, dk), BF16) + _nbytes((2, t_lat, dk), F32) + 2 * _nbytes((t_lat, dv), F32)
               + _nbytes((2, nc + 1, dv, dk), F32))
    return pl.pallas_call(
        functools.partial(_gla_kernel, t_lat=t_lat, t_ctx=t_ctx),
        grid=(batch, GLA_HEADS),
        in_specs=[lat(dk, A_GQ), lat(dk, A_GK), lat(dv, A_GV), lat(dv, A_GO), lat_gd,
                  ctx(dk, A_GQ), ctx(dk, A_GK), ctx(dv, A_GV), ctx(dv, A_GO), ctx_gd,
                  pl.BlockSpec((2, 128, dk), lambda b, h: (0, 0, h)),
                  pl.BlockSpec((2, 1, dk), lambda b, h: (0, 0, h)),
                  pl.BlockSpec((1, dv), lambda b, h: (0, 0))],
        out_specs=[pl.BlockSpec((t_lat, dv), lambda b, h: (b, h)),
                   pl.BlockSpec((t_ctx, dv), lambda b, h: (b, h))],
        out_shape=[jax.ShapeDtypeStruct((batch * t_lat, GLA_HEADS * dv), BF16),
                   jax.ShapeDtypeStruct((batch * t_ctx, GLA_HEADS * dv), BF16)],
        scratch_shapes=scratch_shapes,
        compiler_params=pltpu.CompilerParams(
            dimension_semantics=("arbitrary", "arbitrary"),
            vmem_limit_bytes=_vmem_limit(blocks, scratch, 6 << 20)),
        name="gla",
    )(pa, pa, pa, pa, pg, pa, pa, pa, pa, pg, gate_up_pad, gate_bias, gla_norm)


def _rms(x, gain):
    return (x * lax.rsqrt(jnp.mean(x * x, axis=-1, keepdims=True) + EPS)) * gain


def _rope(x, cos, sin_signed):
    lane = lax.broadcasted_iota(jnp.int32, x.shape, 1)
    partner = jnp.where((lane % 64) < 32, pltpu.roll(x, 96, 1), pltpu.roll(x, 32, 1))
    return x * cos + partner * sin_signed


def _q_row_block(b, qb, *, batch, n_lat, n_ctx):
    return jnp.where(qb < n_lat, b * n_lat + qb, batch * n_lat + b * n_ctx + (qb - n_lat))


KV_PREP_ROWS = 256


def _swa_window_masks(t_lat):
    span = 3 * QBLK
    q_off = np.array([0, QBLK, 2 * QBLK])[:, None, None] + np.arange(QBLK)[None, :, None]
    k_off = np.arange(span)[None, None, :]
    return jnp.asarray(np.where(np.abs(k_off - q_off) <= SWA_WINDOW, 0.0, NEG_INF), F32)


def _swa_mask_variant(qb, n_lat):
    return jnp.where(qb == 0, 0, jnp.where(qb >= n_lat - 1, 2, 1))


def _swa_kernel(sink_ref, q_ref, kl_ref, vl_ref, kc_ref, vc_ref, cq_ref, sq_ref, ck_ref, sk_ref,
                qg_ref, kg_ref, mask_ref, o_ref, kn_ref, vb_ref, kcn_ref, vcb_ref, *, t_lat, n_lat):
    qb = pl.program_id(1)
    g = SWA_GROUP
    dh = HEAD_DIM
    scale = dh ** -0.5

    @pl.when(qb == 0)
    def _prep():
        k_gain = kg_ref[...]

        def body(r, carry):
            r0 = pl.multiple_of(r * KV_PREP_ROWS, KV_PREP_ROWS)
            rows = pl.ds(r0, KV_PREP_ROWS)
            cos, sin = ck_ref[rows, :], sk_ref[rows, :]
            for kv in range(SWA_KV_HEADS):
                cols = slice(kv * dh, (kv + 1) * dh)
                kn_ref[rows, cols] = _rope(_rms(kl_ref[rows, cols], k_gain), cos, sin).astype(BF16)
            vb_ref[rows, :] = vl_ref[rows, :].astype(BF16)
            return carry

        lax.fori_loop(0, t_lat // KV_PREP_ROWS, body, 0)
        for kv in range(SWA_KV_HEADS):
            cols = slice(kv * dh, (kv + 1) * dh)
            kcn_ref[:, cols] = _rms(kc_ref[:, cols], k_gain).astype(BF16)
        vcb_ref[...] = vc_ref[...].astype(BF16)

    def attend(latent):
        cos, sin = cq_ref[...], sq_ref[...]
        q_gain = qg_ref[...]
        span = 3 * QBLK
        start = pl.multiple_of(jnp.clip((qb - 1) * QBLK, 0, t_lat - span), QBLK)
        kvs = range(SWA_KV_HEADS)
        cols = [slice(kv * dh, (kv + 1) * dh) for kv in kvs]
        qs = []
        for kv in kvs:
            heads = [_rope(_rms(q_ref[:, (kv * g + i) * dh:(kv * g + i + 1) * dh], q_gain), cos, sin).astype(BF16)
                     for i in range(g)]
            qs.append(jnp.concatenate(heads, axis=0))
        s_ctx = [lax.dot_general(qs[kv], kcn_ref[:, cols[kv]], _NT, preferred_element_type=F32) for kv in kvs]
        if latent:
            s_loc = [lax.dot_general(qs[kv], kn_ref[pl.ds(start, span), cols[kv]], _NT,
                                     preferred_element_type=F32) for kv in kvs]
        p_loc, p_ctx, dens = ([[] for _ in kvs] for _ in range(3))
        for i in range(g):
            rows = slice(i * QBLK, (i + 1) * QBLK)
            for kv in kvs:
                sink = sink_ref[kv * g + i]
                sc = s_ctx[kv][rows] * scale
                m = jnp.maximum(jnp.max(sc, axis=-1, keepdims=True), sink)
                if latent:
                    sl = s_loc[kv][rows] * scale + mask_ref[...]
                    m = jnp.maximum(m, jnp.max(sl, axis=-1, keepdims=True))
                    pl_i = jnp.exp(sl - m)
                    p_loc[kv].append(pl_i.astype(BF16))
                pc_i = jnp.exp(sc - m)
                p_ctx[kv].append(pc_i.astype(BF16))
                den = jnp.sum(pc_i, axis=-1, keepdims=True) + jnp.exp(sink - m)
                dens[kv].append(den + jnp.sum(pl_i, axis=-1, keepdims=True) if latent else den)
        o = [jnp.dot(jnp.concatenate(p_ctx[kv], axis=0), vcb_ref[:, cols[kv]], preferred_element_type=F32)
             for kv in kvs]
        if latent:
            o = [o[kv] + jnp.dot(jnp.concatenate(p_loc[kv], axis=0), vb_ref[pl.ds(start, span), cols[kv]],
                                 preferred_element_type=F32) for kv in kvs]
        for kv in kvs:
            for i in range(g):
                o_i = o[kv][i * QBLK:(i + 1) * QBLK, :] / dens[kv][i]
                o_ref[:, (kv * g + i) * dh:(kv * g + i + 1) * dh] = o_i.astype(o_ref.dtype)

    pl.when(qb < n_lat)(functools.partial(attend, True))
    pl.when(qb >= n_lat)(functools.partial(attend, False))


def _swa(pc, sink, cos_tab, sin_tab, q_gain, k_gain, *, batch, with_ctx_out):
    t_lat, t_ctx, dh = SEQ, CTX_LEN, HEAD_DIM
    qw, kw = SWA_HEADS * dh, SWA_KV_HEADS * dh
    n_lat, n_ctx = t_lat // QBLK, t_ctx // QBLK
    n_q = n_lat + (n_ctx if with_ctx_out else 0)
    rows_out = batch * (t_lat + (t_ctx if with_ctx_out else 0))
    ctx0 = batch * t_lat // t_ctx
    qmap = functools.partial(_q_row_block, batch=batch, n_lat=n_lat, n_ctx=n_ctx)
    blocks = (_nbytes((QBLK, qw), F32) + 2 * _nbytes((t_lat + t_ctx, kw), F32) + 2 * _nbytes((QBLK, dh), F32)
              + 2 * _nbytes((t_lat, dh), F32) + _nbytes((QBLK, qw), BF16))
    scratch = 2 * _nbytes((t_lat + t_ctx, kw), BF16)
    return pl.pallas_call(
        functools.partial(_swa_kernel, t_lat=t_lat, n_lat=n_lat),
        grid=(batch, n_q),
        in_specs=[
            pl.BlockSpec(memory_space=pltpu.SMEM),
            pl.BlockSpec((QBLK, qw), lambda b, qb: (qmap(b, qb), C_SQ // qw)),
            pl.BlockSpec((t_lat, kw), lambda b, qb: (b, C_SK // kw)),
            pl.BlockSpec((t_lat, kw), lambda b, qb: (b, C_SV // kw)),
            pl.BlockSpec((t_ctx, kw), lambda b, qb: (ctx0 + b, C_SK // kw)),
            pl.BlockSpec((t_ctx, kw), lambda b, qb: (ctx0 + b, C_SV // kw)),
            pl.BlockSpec((QBLK, dh), lambda b, qb: (qb, 0)),
            pl.BlockSpec((QBLK, dh), lambda b, qb: (qb, 0)),
            pl.BlockSpec((t_lat, dh), lambda b, qb: (0, 0)),
            pl.BlockSpec((t_lat, dh), lambda b, qb: (0, 0)),
            pl.BlockSpec((1, dh), lambda b, qb: (0, 0)),
            pl.BlockSpec((1, dh), lambda b, qb: (0, 0)),
            pl.BlockSpec((None, QBLK, 3 * QBLK), lambda b, qb: (_swa_mask_variant(qb, n_lat), 0, 0)),
        ],
        out_specs=pl.BlockSpec((QBLK, qw), lambda b, qb: (qmap(b, qb), 0)),
        out_shape=jax.ShapeDtypeStruct((rows_out, qw), BF16),
        scratch_shapes=[pltpu.VMEM((t_lat, kw), BF16), pltpu.VMEM((t_lat, kw), BF16),
                        pltpu.VMEM((t_ctx, kw), BF16), pltpu.VMEM((t_ctx, kw), BF16)],
        compiler_params=pltpu.CompilerParams(
            dimension_semantics=("arbitrary", "arbitrary"),
            vmem_limit_bytes=_vmem_limit(blocks, scratch, 12 << 20)),
        name="swa",
    )(sink, pc, pc, pc, pc, pc, cos_tab, sin_tab, cos_tab, sin_tab, q_gain, k_gain, _swa_window_masks(t_lat))


NA_HEADS_PER_STEP = 4


def _na_slab_start(qb):
    r = qb * (QBLK // GRID_W)
    rows = SEQ // GRID_W
    return jnp.minimum(jnp.clip(r - NA_KH // 2, 0, rows - NA_KH), rows - NA_SLAB_ROWS)


def _na_bias_variant(qb, n_lat):
    return jnp.where(qb < 2, qb, jnp.where(qb < n_lat - 2, 2, jnp.minimum(qb, n_lat - 1) - (n_lat - 5)))


def _na_kernel(q_ref, kl_ref, vl_ref, kc_ref, vc_ref, bias_ref, qg_ref, kg_ref, o_ref,
               kn_ref, vb_ref, kcn_ref, vcb_ref, *, t_lat, n_lat):
    qb = pl.program_id(2)
    dh = HEAD_DIM
    scale = dh ** -0.5

    @pl.when(qb == 0)
    def _prep():
        k_gain = kg_ref[...]

        def body(r, carry):
            r0 = pl.multiple_of(r * KV_PREP_ROWS, KV_PREP_ROWS)
            rows = pl.ds(r0, KV_PREP_ROWS)
            for h in range(NA_HEADS_PER_STEP):
                cols = slice(h * dh, (h + 1) * dh)
                kn_ref[rows, cols] = _rms(kl_ref[rows, cols], k_gain).astype(BF16)
            vb_ref[rows, :] = vl_ref[rows, :].astype(BF16)
            return carry

        lax.fori_loop(0, t_lat // KV_PREP_ROWS, body, 0)
        for h in range(NA_HEADS_PER_STEP):
            cols = slice(h * dh, (h + 1) * dh)
            kcn_ref[:, cols] = _rms(kc_ref[:, cols], k_gain).astype(BF16)
        vcb_ref[...] = vc_ref[...].astype(BF16)

    def attend(latent):
        q_gain = qg_ref[...]
        heads = range(NA_HEADS_PER_STEP)
        cols = [slice(h * dh, (h + 1) * dh) for h in heads]
        span = NA_SLAB_ROWS * GRID_W
        start = pl.multiple_of(_na_slab_start(qb) * GRID_W, GRID_W)
        qn = [_rms(q_ref[:, cols[h]], q_gain).astype(BF16) for h in heads]
        s_ctx = [lax.dot_general(qn[h], kcn_ref[:, cols[h]], _NT, preferred_element_type=F32) * scale
                 for h in heads]
        m = [jnp.max(s, axis=-1, keepdims=True) for s in s_ctx]
        if latent:
            s_loc = [lax.dot_general(qn[h], kn_ref[pl.ds(start, span), cols[h]], _NT,
                                     preferred_element_type=F32) for h in heads]
            s_loc = [s_loc[h] * scale + bias_ref[h] for h in heads]
            m = [jnp.maximum(m[h], jnp.max(s_loc[h], axis=-1, keepdims=True)) for h in heads]
            p_loc = [jnp.exp(s_loc[h] - m[h]) for h in heads]
        p_ctx = [jnp.exp(s_ctx[h] - m[h]) for h in heads]
        o = [jnp.dot(p_ctx[h].astype(BF16), vcb_ref[:, cols[h]], preferred_element_type=F32) for h in heads]
        den = [jnp.sum(p_ctx[h], axis=-1, keepdims=True) for h in heads]
        if latent:
            o = [o[h] + jnp.dot(p_loc[h].astype(BF16), vb_ref[pl.ds(start, span), cols[h]],
                                preferred_element_type=F32) for h in heads]
            den = [den[h] + jnp.sum(p_loc[h], axis=-1, keepdims=True) for h in heads]
        for h in heads:
            o_ref[:, cols[h]] = (o[h] / den[h]).astype(o_ref.dtype)

    pl.when(qb < n_lat)(functools.partial(attend, True))
    pl.when(qb >= n_lat)(functools.partial(attend, False))


def _na(pc, bias_tab, q_gain, k_gain, *, batch, with_ctx_out):
    t_lat, t_ctx, dh = SEQ, CTX_LEN, HEAD_DIM
    hb = NA_HEADS_PER_STEP
    w = hb * dh
    n_lat, n_ctx = t_lat // QBLK, t_ctx // QBLK
    n_q = n_lat + (n_ctx if with_ctx_out else 0)
    rows_out = batch * (t_lat + (t_ctx if with_ctx_out else 0))
    ctx0 = batch * t_lat // t_ctx
    span = NA_SLAB_ROWS * GRID_W
    qmap = functools.partial(_q_row_block, batch=batch, n_lat=n_lat, n_ctx=n_ctx)
    blocks = (_nbytes((QBLK, w), F32) + 2 * _nbytes((t_lat + t_ctx, w), F32) + _nbytes((hb, QBLK, span), F32)
              + _nbytes((QBLK, w), BF16))
    scratch = 2 * _nbytes((t_lat + t_ctx, w), BF16)
    return pl.pallas_call(
        functools.partial(_na_kernel, t_lat=t_lat, n_lat=n_lat),
        grid=(batch, NA_HEADS // hb, n_q),
        in_specs=[
            pl.BlockSpec((QBLK, w), lambda b, hg, qb: (qmap(b, qb), C_NQ // w + hg)),
            pl.BlockSpec((t_lat, w), lambda b, hg, qb: (b, C_NK // w + hg)),
            pl.BlockSpec((t_lat, w), lambda b, hg, qb: (b, C_NV // w + hg)),
            pl.BlockSpec((t_ctx, w), lambda b, hg, qb: (ctx0 + b, C_NK // w + hg)),
            pl.BlockSpec((t_ctx, w), lambda b, hg, qb: (ctx0 + b, C_NV // w + hg)),
            pl.BlockSpec((hb, None, QBLK, span), lambda b, hg, qb: (hg, _na_bias_variant(qb, n_lat), 0, 0)),
            pl.BlockSpec((1, dh), lambda b, hg, qb: (0, 0)),
            pl.BlockSpec((1, dh), lambda b, hg, qb: (0, 0)),
        ],
        out_specs=pl.BlockSpec((QBLK, w), lambda b, hg, qb: (qmap(b, qb), hg)),
        out_shape=jax.ShapeDtypeStruct((rows_out, NA_HEADS * dh), BF16),
        scratch_shapes=[pltpu.VMEM((t_lat, w), BF16), pltpu.VMEM((t_lat, w), BF16),
                        pltpu.VMEM((t_ctx, w), BF16), pltpu.VMEM((t_ctx, w), BF16)],
        compiler_params=pltpu.CompilerParams(
            dimension_semantics=("arbitrary", "arbitrary", "arbitrary"),
            vmem_limit_bytes=_vmem_limit(blocks, scratch, 12 << 20)),
        name="natten",
    )(pc, pc, pc, pc, pc, bias_tab, q_gain, k_gain)


def _rope_tables():
    quarter = HEAD_DIM // 4
    pos = jnp.arange(SEQ)
    rows = (pos // GRID_W).astype(F32)
    cols = (pos % GRID_W).astype(F32)
    inv = ROPE_BASE ** (-jnp.arange(quarter, dtype=F32) / quarter)
    ang_r = rows[:, None] * inv[None, :]
    ang_c = cols[:, None] * inv[None, :]
    cos = jnp.concatenate([jnp.cos(ang_r), jnp.cos(ang_r), jnp.cos(ang_c), jnp.cos(ang_c)], axis=-1)
    sin = jnp.concatenate([-jnp.sin(ang_r), jnp.sin(ang_r), -jnp.sin(ang_c), jnp.sin(ang_c)], axis=-1)
    cos = jnp.concatenate([cos, jnp.ones((CTX_LEN, HEAD_DIM), F32)], axis=0)
    sin = jnp.concatenate([sin, jnp.zeros((CTX_LEN, HEAD_DIM), F32)], axis=0)
    return cos, sin


def _na_bias_tables(rpb):
    n_layers, n_heads = rpb.shape[:2]
    rows = SEQ // GRID_W
    n_lat = SEQ // QBLK
    rq_per = QBLK // GRID_W
    reps = np.array([0, 1, 2, n_lat - 2, n_lat - 1])
    r = reps * rq_per
    start = np.minimum(np.clip(r - NA_KH // 2, 0, rows - NA_KH), rows - NA_SLAB_ROWS)
    rq = r[:, None] + np.arange(rq_per)[None, :]
    kr = start[:, None] + np.arange(NA_SLAB_ROWS)[None, :]
    r0 = np.clip(rq - NA_KH // 2, 0, rows - NA_KH)
    valid_r = (kr[:, None, :] >= r0[:, :, None]) & (kr[:, None, :] < r0[:, :, None] + NA_KH)
    dr = np.clip(kr[:, None, :] - rq[:, :, None] + NA_KH - 1, 0, 2 * NA_KH - 2)
    qc = np.arange(GRID_W)
    kc = np.arange(GRID_W)
    cs = np.clip(qc - NA_KW // 2, 0, GRID_W - NA_KW)
    valid_c = (kc[None, :] >= cs[:, None]) & (kc[None, :] < cs[:, None] + NA_KW)
    dc = np.clip(kc[None, :] - qc[:, None] + NA_KW - 1, 0, 2 * NA_KW - 2)
    pick_r = np.eye(2 * NA_KH - 1, dtype=np.float32)[dr.reshape(-1)]
    pick_c = np.eye(2 * NA_KW - 1, dtype=np.float32)[dc.reshape(-1)]
    t = jnp.einsum("nr,lhrc->lhnc", pick_r, rpb.astype(F32), precision=HIGHEST)
    t = jnp.einsum("lhnc,xc->lhnx", t, pick_c, precision=HIGHEST)
    t = t.reshape(n_layers, n_heads, len(reps), rq_per, NA_SLAB_ROWS, GRID_W, GRID_W)
    t = t.transpose(0, 1, 2, 3, 5, 4, 6)
    valid = valid_r[:, :, None, :, None] & valid_c[None, None, :, None, :]
    t = jnp.where(valid[None, None], t, NEG_INF)
    return t.reshape(n_layers, n_heads, len(reps), QBLK, NA_SLAB_ROWS * GRID_W)


SUBLANES = 8
CAST_K = 2048
CAST_CHUNK = 256


def _cast_cols_kernel(main_ref, next_ref, o_ref, *, shift):
    bn = o_ref.shape[1]
    for c in range(o_ref.shape[0] // CAST_CHUNK):
        cols = slice(c * CAST_CHUNK, (c + 1) * CAST_CHUNK)
        if shift == 0:
            x = main_ref[:, cols]
        else:
            x = jnp.concatenate([main_ref[shift:bn, cols], next_ref[:, cols]], axis=0)
        o_ref[cols, :] = x.T.astype(o_ref.dtype)


def _cast_cols(w_t, n_layers, col0, n, *, bn):
    k = w_t.shape[2]
    shift = col0 % bn
    base = col0 - shift
    nxt = shift if shift else SUBLANES
    assert n % bn == 0 and k % CAST_K == 0 and nxt % SUBLANES == 0 and bn % nxt == 0 and base % nxt == 0
    blocks = _nbytes((bn + nxt, CAST_K), F32) + _nbytes((CAST_K, bn), BF16)
    return pl.pallas_call(
        functools.partial(_cast_cols_kernel, shift=shift),
        grid=(n_layers, k // CAST_K, n // bn),
        in_specs=[
            pl.BlockSpec((None, bn, CAST_K), lambda l, i, j: (l, base // bn + j, i)),
            pl.BlockSpec((None, nxt, CAST_K), lambda l, i, j: (l, (base + (j + 1) * bn) // nxt, i)),
        ],
        out_specs=pl.BlockSpec((None, CAST_K, bn), lambda l, i, j: (l, i, j)),
        out_shape=jax.ShapeDtypeStruct((n_layers, k, n), BF16),
        compiler_params=pltpu.CompilerParams(
            dimension_semantics=("arbitrary", "arbitrary", "arbitrary"),
            vmem_limit_bytes=_vmem_limit(blocks, 0, 4 << 20)),
        name="cast_cols",
    )(w_t, w_t)


def _pad_gate_up(gate_up):
    r = gate_up.shape[1]
    out = jnp.zeros((2, 128, gate_up.shape[2]), F32)
    out = out.at[0, 0:r].set(gate_up[0])
    out = out.at[1, r:2 * r].set(gate_up[1])
    return out


BM = 1024
BM_SMALL = 256
BM_NORM = 512
BN = 512
BN_FF = 512
BN_A = 640
BN_QKVG = 1024
BN_C = 768
BN_MERGE = 512


def kernel(x, c, ctx, c_ctx, ada_down, ada_up, ada_bias, norm_gain, ffn_w_in, ffn_w_out, w_in,
           gla_gate_up, gla_gate_bias, gla_norm, swa_q_norm, swa_k_norm, swa_sink, na_q_norm,
           na_k_norm, na_rpb, w_branch, w_out):
    batch, seq, d = x.shape
    depth = ada_down.shape[0]
    assert (seq, d, ctx.shape[1]) == (SEQ, D_MODEL, CTX_LEN) and batch + 1 <= MOD_ROWS
    assert w_in.shape[2] == C_START + C_QKV_COLS + GATE_COLS
    lat_rows = batch * seq
    all_rows = lat_rows + batch * ctx.shape[1]

    v8 = jnp.concatenate([c, c_ctx[None], jnp.zeros((MOD_ROWS - batch - 1, d), F32)], axis=0)
    mods = _adaln(v8, ada_down, ada_up, ada_bias)
    mods = mods.reshape(depth, N_MOD, MOD_ROWS, 1, d)
    cos_tab, sin_tab = _rope_tables()
    na_bias = _na_bias_tables(na_rpb)

    ffn_in_f = ffn_w_in.reshape(depth * 2, d, -1)
    ffn_out_f = ffn_w_out.reshape(depth * 2, -1, d)
    w_branch_f = w_branch.reshape(depth, N_BRANCH * BRANCH_W, d)
    ffn_in_a = ffn_in_f[0:1].astype(BF16)
    ffn_out_a = ffn_out_f[0:1].astype(BF16)
    w_branch_bf = w_branch_f[0:1].astype(BF16)
    w_out_bf = w_out[0:1].astype(BF16)
    w_in_t = jnp.swapaxes(w_in, 1, 2)
    w_a_bf = _cast_cols(w_in_t, 1, 0, A_COLS, bn=BN_A)
    w_c_bf = _cast_cols(w_in_t, 1, C_START, C_QKV_COLS + GATE_COLS, bn=BN_C)

    h, h_ctx = x.reshape(lat_rows, d), ctx.reshape(-1, d)
    bm = BM if all_rows % BM == 0 and lat_rows % BM == 0 else BM_SMALL
    mm = dict(seq=seq, bm=bm)
    norm = dict(seq=seq, bm=BM_NORM if all_rows % BM_NORM == 0 and lat_rows % BM_NORM == 0 else BM_SMALL)

    for l in range(depth):
        last = l == depth - 1
        rows_out = lat_rows if last else all_rows
        gain = norm_gain[l].reshape(3, 1, d)
        m = mods[l]

        def next_layer(src, idx):
            return None if last else (src, idx)

        xn = _modulate(h, all_rows, gain[0], m[0], m[1], h_ctx=h_ctx, **norm)
        g1, ffn_in_b = _swiglu(xn, ffn_in_a, 0, all_rows, bm=bm, bn=BN_FF, side=(ffn_in_f, 2 * l + 1))
        h, ffn_out_b = _resid_matmul(g1, ffn_out_a, 0, h, all_rows, m[2], weight=MACARON_W, bn=BN,
                                     side=(ffn_out_f, 2 * l + 1), h_ctx=h_ctx, **mm)
        h_ctx = None

        xn = _modulate(h, all_rows, gain[1], m[3], m[4], **norm)
        pa, w_a_next = _matmul(xn, w_a_bf, 0, all_rows, 0, A_GD, bm=bm, bn=BN_QKVG, out_dtype=F32,
                               side_t=None if last else (w_in_t, l + 1, 0, A_COLS))
        pg, _ = _matmul(xn, w_a_bf, 0, all_rows, A_GD, A_COLS - A_GD, bm=bm, bn=A_COLS - A_GD, out_dtype=F32)
        pc, _ = _matmul(xn, w_c_bf, 0, all_rows, 0, C_QKV_COLS, bm=bm, bn=BN_C, out_dtype=F32)
        gates, w_c_next = _matmul(xn, w_c_bf, 0, rows_out, C_QKV_COLS, GATE_COLS, bm=bm, bn=BN_C,
                                  out_dtype=BF16, sigmoid=True,
                                  side_t=None if last else (w_in_t, l + 1, C_START, C_QKV_COLS + GATE_COLS))
        a_lat, a_ctx = _gla(pa, pg, _pad_gate_up(gla_gate_up[l]), gla_gate_bias[l].reshape(2, 1, -1),
                            gla_norm[l].reshape(1, -1), batch=batch)
        o_b = _swa(pc, swa_sink[l], cos_tab, sin_tab, swa_q_norm[l].reshape(1, -1), swa_k_norm[l].reshape(1, -1),
                   batch=batch, with_ctx_out=not last)
        o_c = _na(pc, na_bias[l], na_q_norm[l].reshape(1, -1), na_k_norm[l].reshape(1, -1),
                  batch=batch, with_ctx_out=not last)
        y, w_branch_next = _merge(a_lat, o_b, o_c, w_branch_bf.reshape(1, N_BRANCH, BRANCH_W, d), 0, gates, rows_out,
                                  bm=bm, bn=BN_MERGE, side=next_layer(w_branch_f, l + 1),
                                  oa_ctx=None if last else a_ctx)
        h, w_out_next = _resid_matmul(y, w_out_bf, 0, h, rows_out, m[5], weight=1.0, bn=BN,
                                      side=next_layer(w_out, l + 1), **mm)

        xn = _modulate(h, rows_out, gain[2], m[6], m[7], **norm)
        g2, ffn_in_a = _swiglu(xn, ffn_in_b, 0, rows_out, bm=bm, bn=BN_FF, side=next_layer(ffn_in_f, 2 * l + 2))
        h, ffn_out_a = _resid_matmul(g2, ffn_out_b, 0, h, rows_out, m[8], weight=MACARON_W, bn=BN,
                                     side=next_layer(ffn_out_f, 2 * l + 2), **mm)
        w_branch_bf, w_out_bf, w_a_bf, w_c_bf = w_branch_next, w_out_next, w_a_next, w_c_next

    return h.reshape(batch, seq, d)
```

```python
import functools

import jax
import jax.numpy as jnp
import numpy as np
from jax import lax
from jax.experimental import pallas as pl
from jax.experimental.pallas import tpu as pltpu

F32 = jnp.float32
BF16 = jnp.bfloat16
HIGHEST = lax.Precision.HIGHEST

D_MODEL = 4096
SEQ = 2048
CTX_LEN = 256
GRID_W = 64
HEAD_DIM = 128
GLA_HEADS = 4
GLA_DK = 128
GLA_DV = 256
GLA_GATE_RANK = 16
GLA_GATE_NORM = 16.0
GLA_CHUNK = 64
SWA_HEADS = 8
SWA_KV_HEADS = 2
SWA_GROUP = SWA_HEADS // SWA_KV_HEADS
SWA_WINDOW = 128
NA_HEADS = 8
NA_KH = 8
NA_KW = 16
N_BRANCH = 3
BRANCH_W = 1024
D_FF = 4096
MACARON_W = 0.5
N_MOD = 9
ROPE_BASE = 10000.0
EPS = 1e-6
NEG_INF = -1e30
MOD_ROWS = 8

A_GQ, A_GK, A_GV, A_GO, A_GD = 0, 512, 1024, 2048, 3072
A_COLS = 3200
C_START = 3104
C_SQ, C_SK, C_SV = 0, 1024, 1280
C_NQ, C_NK, C_NV = 1536, 2560, 3584
C_QKV_COLS = 4608
GATE_COLS = N_BRANCH * D_MODEL

V7X_VMEM_BYTES = 64 * 1024 * 1024
V7X_VMEM_RESERVE = 6 * 1024 * 1024
QBLK = 128
NA_SLAB_ROWS = 10


def _vmem_limit(pipelined_bytes, scratch_bytes=0, temp_bytes=0):
    need = 2 * pipelined_bytes + scratch_bytes + temp_bytes + (4 << 20)
    return int(min(max(need, 16 << 20), V7X_VMEM_BYTES - V7X_VMEM_RESERVE))


def _nbytes(shape, dtype):
    return int(np.prod(shape)) * jnp.dtype(dtype).itemsize


def _adaln_kernel(v_ref, down_ref, up_ref, bias_ref, o_ref, t_ref):
    @pl.when(pl.program_id(1) == 0)
    def _():
        t_ref[...] = jnp.dot(jax.nn.silu(v_ref[...]), down_ref[...], preferred_element_type=F32,
                             precision=HIGHEST)

    o_ref[...] = jnp.dot(t_ref[...], up_ref[...], preferred_element_type=F32, precision=HIGHEST) + bias_ref[...]


def _adaln(v8, ada_down, ada_up, ada_bias):
    depth, d, rank = ada_down.shape
    blocks = (_nbytes((MOD_ROWS, d), F32) + _nbytes((d, rank), F32) + _nbytes((rank, d), F32)
              + _nbytes((1, d), F32) + _nbytes((MOD_ROWS, d), F32))
    return pl.pallas_call(
        _adaln_kernel,
        grid=(depth, N_MOD),
        in_specs=[
            pl.BlockSpec((MOD_ROWS, d), lambda l, j: (0, 0)),
            pl.BlockSpec((None, d, rank), lambda l, j: (l, 0, 0)),
            pl.BlockSpec((None, rank, d), lambda l, j: (l, 0, j)),
            pl.BlockSpec((None, 1, d), lambda l, j: (l, 0, j)),
        ],
        out_specs=pl.BlockSpec((None, None, MOD_ROWS, d), lambda l, j: (l, j, 0, 0)),
        out_shape=jax.ShapeDtypeStruct((depth, N_MOD, MOD_ROWS, d), F32),
        scratch_shapes=[pltpu.VMEM((MOD_ROWS, rank), F32)],
        compiler_params=pltpu.CompilerParams(
            dimension_semantics=("arbitrary", "arbitrary"), vmem_limit_bytes=_vmem_limit(blocks)),
        name="adaln",
    )(v8, ada_down, ada_up, ada_bias.reshape(depth, 1, -1))


def _mod_row_index(i, bm, seq):
    return jnp.minimum((i * bm) // seq, MOD_ROWS - 1)


def _mod_spec(bm, seq, d):
    return pl.BlockSpec((None, 1, d), lambda i, j: (_mod_row_index(i, bm, seq), 0, 0))


NORM_ROWS = 32


def _row_specs(a, a_ctx, bm, bn, col=lambda j: 0):
    if a_ctx is None:
        return [pl.BlockSpec((bm, bn), lambda i, j: (i, col(j)))], None
    n_top = a.shape[0] // bm
    assert a.shape[0] % bm == 0 and a_ctx.shape[0] % bm == 0
    return [pl.BlockSpec((bm, bn), lambda i, j: (jnp.minimum(i, n_top - 1), col(j))),
            pl.BlockSpec((bm, bn), lambda i, j: (jnp.maximum(i - n_top, 0), col(j)))], n_top


def _rows_of(top_ref, ctx_ref, n_top, idx=(slice(None), slice(None))):
    if ctx_ref is None:
        return top_ref[idx]
    return jnp.where(pl.program_id(0) < n_top, top_ref[idx], ctx_ref[idx])


def _modulate_kernel(*refs, bm, n_top):
    if n_top is None:
        (h_ref, g_ref, sh_ref, sc_ref, xn_ref, mult_ref), hc_ref = refs, None
    else:
        h_ref, hc_ref, g_ref, sh_ref, sc_ref, xn_ref, mult_ref = refs
    mult_ref[...] = g_ref[...] * (1.0 + sc_ref[...])

    def body(c, carry):
        r0 = pl.multiple_of(c * NORM_ROWS, NORM_ROWS)
        x = _rows_of(h_ref, hc_ref, n_top, (pl.ds(r0, NORM_ROWS), slice(None)))
        r = lax.rsqrt(jnp.mean(x * x, axis=-1, keepdims=True) + EPS)
        xn_ref[pl.ds(r0, NORM_ROWS), :] = ((x * r) * mult_ref[...] + sh_ref[...]).astype(BF16)
        return carry

    lax.fori_loop(0, bm // NORM_ROWS, body, 0, unroll=2)


def _modulate(h, rows, gain, shift, scale, *, seq, bm, h_ctx=None):
    d = h.shape[1]
    blocks = 2 * _nbytes((bm, d), F32) + _nbytes((bm, d), BF16) + 3 * _nbytes((1, d), F32)
    h_specs, n_top = _row_specs(h, h_ctx, bm, d)
    return pl.pallas_call(
        functools.partial(_modulate_kernel, bm=bm, n_top=n_top),
        grid=(rows // bm, 1),
        in_specs=h_specs + [
            pl.BlockSpec((1, d), lambda i, j: (0, 0)),
            _mod_spec(bm, seq, d),
            _mod_spec(bm, seq, d),
        ],
        out_specs=pl.BlockSpec((bm, d), lambda i, j: (i, 0)),
        out_shape=jax.ShapeDtypeStruct((rows, d), BF16),
        scratch_shapes=[pltpu.VMEM((1, d), F32)],
        compiler_params=pltpu.CompilerParams(
            dimension_semantics=("arbitrary", "arbitrary"), vmem_limit_bytes=_vmem_limit(blocks)),
        name="modulate",
    )(*((h,) if h_ctx is None else (h, h_ctx)), gain, shift, scale)


SIDE_ROWS = 64


def _side_cast(side, n_steps, nj):
    src, sidx = side
    _, r, cdim = src.shape
    side_rows = SIDE_ROWS
    while r // side_rows > n_steps:
        side_rows *= 2
    n_chunks = r // side_rows
    assert r % side_rows == 0

    def chunk(i, j):
        return jnp.minimum(i * nj + j, n_chunks - 1)

    in_spec = pl.BlockSpec((None, side_rows, cdim), lambda i, j: (sidx, chunk(i, j), 0))
    out_spec = pl.BlockSpec((None, side_rows, cdim), lambda i, j: (0, chunk(i, j), 0))
    nbytes = _nbytes((side_rows, cdim), F32) + _nbytes((side_rows, cdim), BF16)
    return in_spec, out_spec, jax.ShapeDtypeStruct((1, r, cdim), BF16), nbytes


def _with_side(body, n_in, n_out):
    def kernel(*refs):
        if len(refs) == n_in + n_out:
            body(*refs)
        else:
            body(*refs[:n_in], *refs[n_in + 1:n_in + 1 + n_out])
            refs[-1][...] = refs[n_in][...].astype(refs[-1].dtype)
    return kernel


def _call_with_side(body, n_in, side, grid, in_specs, out_spec, out_shape, blocks, temp, name, args):
    out_specs, out_shapes = [out_spec], [out_shape]
    if side is not None:
        s_in, s_out, s_shape, s_bytes = _side_cast(side, grid[0] * grid[1], grid[1])
        in_specs, out_specs, out_shapes = in_specs + [s_in], out_specs + [s_out], out_shapes + [s_shape]
        blocks, args = blocks + s_bytes, args + (side[0],)
    res = pl.pallas_call(
        _with_side(body, n_in, 1),
        grid=grid, in_specs=in_specs, out_specs=out_specs, out_shape=out_shapes,
        compiler_params=pltpu.CompilerParams(
            dimension_semantics=("arbitrary", "arbitrary"), vmem_limit_bytes=_vmem_limit(blocks, 0, temp)),
        name=name,
    )(*args)
    return (res[0], res[1]) if side is not None else (res[0], None)


def _swiglu_kernel(x_ref, wa_ref, wb_ref, o_ref):
    x = x_ref[...]
    a = jnp.dot(x, wa_ref[...], preferred_element_type=F32)
    b = jnp.dot(x, wb_ref[...], preferred_element_type=F32)
    o_ref[...] = (jax.nn.silu(a) * b).astype(o_ref.dtype)


def _swiglu(x, w_in, widx, rows, *, bm, bn, side=None):
    d = x.shape[1]
    f = w_in.shape[2] // 2
    nb = f // bn
    blocks = _nbytes((bm, d), BF16) + 2 * _nbytes((d, bn), BF16) + _nbytes((bm, bn), BF16)
    in_specs = [
        pl.BlockSpec((bm, d), lambda i, j: (i, 0)),
        pl.BlockSpec((None, d, bn), lambda i, j: (widx, 0, j)),
        pl.BlockSpec((None, d, bn), lambda i, j: (widx, 0, j + nb)),
    ]
    return _call_with_side(_swiglu_kernel, 3, side, (rows // bm, nb), in_specs,
                           pl.BlockSpec((bm, bn), lambda i, j: (i, j)), jax.ShapeDtypeStruct((rows, f), BF16),
                           blocks, 4 * _nbytes((bm, bn), F32), "swiglu", (x, w_in, w_in))


SIDE_T_ROWS = 128
SIDE_T_CHUNK = 512


def _mm_kernel(*refs, sigmoid, shift):
    if len(refs) == 3:
        x_ref, w_ref, o_ref = refs
    else:
        x_ref, w_ref, main_ref, next_ref, o_ref, dst_ref = refs
    y = jnp.dot(x_ref[...], w_ref[...], preferred_element_type=F32)
    if sigmoid:
        y = jax.nn.sigmoid(y)
    o_ref[...] = y.astype(o_ref.dtype)
    if len(refs) == 6:
        for c in range(dst_ref.shape[0] // SIDE_T_CHUNK):
            cols = slice(c * SIDE_T_CHUNK, (c + 1) * SIDE_T_CHUNK)
            if shift == 0:
                blk = main_ref[:, cols]
            else:
                blk = jnp.concatenate([main_ref[shift:, cols], next_ref[:, cols]], axis=0)
            dst_ref[cols, :] = blk.T.astype(dst_ref.dtype)


def _matmul(x, w, widx, rows, col0, n, *, bm, bn, out_dtype, sigmoid=False, side_t=None):
    k = x.shape[1]
    jb0 = col0 // bn
    grid = (rows // bm, n // bn)
    blocks = _nbytes((bm, k), BF16) + _nbytes((k, bn), BF16) + _nbytes((bm, bn), out_dtype)
    in_specs = [
        pl.BlockSpec((bm, k), lambda i, j: (i, 0)),
        pl.BlockSpec((None, k, bn), lambda i, j: (widx, 0, jb0 + j)),
    ]
    out_specs = [pl.BlockSpec((bm, bn), lambda i, j: (i, j))]
    out_shapes = [jax.ShapeDtypeStruct((rows, n), out_dtype)]
    args = (x, w)
    shift = 0
    if side_t is not None:
        w_t, sidx, c0, cn = side_t
        kk = w_t.shape[2]
        shift = c0 % SIDE_T_ROWS
        nxt = shift if shift else SUBLANES
        n_chunks = cn // SIDE_T_ROWS
        base_blk = (c0 - shift) // SIDE_T_ROWS
        assert cn % SIDE_T_ROWS == 0 and n_chunks <= grid[0] * grid[1] and SIDE_T_ROWS % nxt == 0

        def chunk(i, j):
            return jnp.minimum(i * grid[1] + j, n_chunks - 1)

        in_specs += [
            pl.BlockSpec((None, SIDE_T_ROWS, kk), lambda i, j: (sidx, base_blk + chunk(i, j), 0)),
            pl.BlockSpec((None, nxt, kk),
                         lambda i, j: (sidx, (base_blk + chunk(i, j) + 1) * (SIDE_T_ROWS // nxt), 0)),
        ]
        out_specs.append(pl.BlockSpec((None, kk, SIDE_T_ROWS), lambda i, j: (0, 0, chunk(i, j))))
        out_shapes.append(jax.ShapeDtypeStruct((1, kk, cn), BF16))
        blocks += _nbytes((SIDE_T_ROWS + nxt, kk), F32) + _nbytes((kk, SIDE_T_ROWS), BF16)
        args = (x, w, w_t, w_t)
    res = pl.pallas_call(
        functools.partial(_mm_kernel, sigmoid=sigmoid, shift=shift),
        grid=grid, in_specs=in_specs, out_specs=out_specs, out_shape=out_shapes,
        compiler_params=pltpu.CompilerParams(
            dimension_semantics=("arbitrary", "arbitrary"),
            vmem_limit_bytes=_vmem_limit(blocks, 0, 2 * _nbytes((bm, bn), F32) + (2 << 20))),
        name="matmul_sigmoid" if sigmoid else "matmul",
    )(*args)
    return (res[0], res[1]) if side_t is not None else (res[0], None)


def _resid_mm_kernel(*refs, weight, n_top):
    if n_top is None:
        (x_ref, w_ref, h_ref, gate_ref, o_ref), hc_ref = refs, None
    else:
        x_ref, w_ref, h_ref, hc_ref, gate_ref, o_ref = refs
    y = jnp.dot(x_ref[...], w_ref[...], preferred_element_type=F32)
    h = _rows_of(h_ref, hc_ref, n_top)
    if weight != 1.0:
        o_ref[...] = h + weight * gate_ref[...] * y
    else:
        o_ref[...] = h + gate_ref[...] * y


def _resid_matmul(x, w, widx, h, rows, gate, *, weight, seq, bm, bn, side=None, h_ctx=None):
    k = x.shape[1]
    n = w.shape[2]
    blocks = (_nbytes((bm, k), BF16) + _nbytes((k, bn), BF16) + 3 * _nbytes((bm, bn), F32)
              + _nbytes((1, bn), F32))
    h_specs, n_top = _row_specs(h, h_ctx, bm, bn, lambda j: j)
    in_specs = [
        pl.BlockSpec((bm, k), lambda i, j: (i, 0)),
        pl.BlockSpec((None, k, bn), lambda i, j: (widx, 0, j)),
    ] + h_specs + [
        pl.BlockSpec((None, 1, bn), lambda i, j: (_mod_row_index(i, bm, seq), 0, j)),
    ]
    args = (x, w, h, gate) if h_ctx is None else (x, w, h, h_ctx, gate)
    return _call_with_side(functools.partial(_resid_mm_kernel, weight=weight, n_top=n_top), len(args), side,
                           (rows // bm, n // bn), in_specs, pl.BlockSpec((bm, bn), lambda i, j: (i, j)),
                           jax.ShapeDtypeStruct((rows, n), F32), blocks, _nbytes((bm, bn), F32),
                           "resid_matmul", args)


def _merge_kernel(*refs, n_top):
    if n_top is None:
        (oa_ref, ob_ref, oc_ref, w_ref, ga_ref, gb_ref, gc_ref, y_ref), oac_ref = refs, None
    else:
        oa_ref, oac_ref, ob_ref, oc_ref, w_ref, ga_ref, gb_ref, gc_ref, y_ref = refs
    oa = _rows_of(oa_ref, oac_ref, n_top)
    y = ga_ref[...].astype(F32) * jnp.dot(oa, w_ref[0], preferred_element_type=F32)
    y = y + gb_ref[...].astype(F32) * jnp.dot(ob_ref[...], w_ref[1], preferred_element_type=F32)
    y = y + gc_ref[...].astype(F32) * jnp.dot(oc_ref[...], w_ref[2], preferred_element_type=F32)
    y_ref[...] = y.astype(y_ref.dtype)


def _merge(oa, ob, oc, w_branch, widx, gates, rows, *, bm, bn, side=None, oa_ctx=None):
    kb = oa.shape[1]
    d = w_branch.shape[3]
    per_branch = d // bn
    blocks = (4 * _nbytes((bm, kb), BF16) + _nbytes((N_BRANCH, kb, bn), BF16) + 4 * _nbytes((bm, bn), BF16))
    o_spec = pl.BlockSpec((bm, kb), lambda i, j: (i, 0))
    oa_specs, n_top = _row_specs(oa, oa_ctx, bm, kb)

    def gate_spec(br):
        return pl.BlockSpec((bm, bn), lambda i, j: (i, br * per_branch + j))

    in_specs = oa_specs + [o_spec, o_spec,
                           pl.BlockSpec((None, N_BRANCH, kb, bn), lambda i, j: (widx, 0, 0, j)),
                           gate_spec(0), gate_spec(1), gate_spec(2)]
    args = ((oa,) if oa_ctx is None else (oa, oa_ctx)) + (ob, oc, w_branch, gates, gates, gates)
    return _call_with_side(functools.partial(_merge_kernel, n_top=n_top), len(args), side,
                           (rows // bm, d // bn), in_specs, pl.BlockSpec((bm, bn), lambda i, j: (i, j)),
                           jax.ShapeDtypeStruct((rows, d), BF16), blocks, 4 * _nbytes((bm, bn), F32), "merge",
                           args)


GLA_BLOCK = 256
GLA_UNROLL = 4
GLA_BLOCKS_PER_STEP = 2
_NT = (((1,), (1,)), ((), ()))
_TN = (((0,), (0,)), ((), ()))


def _split_bf16(x):
    hi = x.astype(BF16)
    lo = (x - hi.astype(F32)).astype(BF16)
    return hi, lo


def _sum_dot(op, parts):
    acc = jnp.dot(op, parts[0], preferred_element_type=F32)
    for p in parts[1:]:
        acc = acc + jnp.dot(op, p, preferred_element_type=F32)
    return acc


def _gla_kernel(ql_ref, kl_ref, vl_ref, ogl_ref, gdl_ref, qc_ref, kc_ref, vc_ref, ogc_ref, gdc_ref,
                up_ref, gb_ref, gn_ref, al_ref, ac_ref,
                qe_ref, edec_ref, of_ref, ob_ref, kv_ref, st_ref, *, t_lat, t_ctx):
    c = GLA_CHUNK
    br = GLA_BLOCK
    cpb = br // c
    row = lax.broadcasted_iota(jnp.int32, (br, br), 0)
    col = lax.broadcasted_iota(jnp.int32, (br, br), 1)
    same = (row // c) == (col // c)
    keep = (same & (col <= row), same & (col >= row))
    tri = (keep[0].astype(BF16), keep[1].astype(BF16))
    chunk_of_row = lax.broadcasted_iota(jnp.int32, (br, GLA_DK), 0) // c
    q_scale = GLA_DK ** -0.5
    gain = gn_ref[...]
    o_refs = (of_ref, ob_ref)

    st_ref[...] = jnp.zeros_like(st_ref)

    def segment(q_ref, k_ref, v_ref, og_ref, gd_ref, out_ref, t):
        nc = t // c

        per_step = min(GLA_BLOCKS_PER_STEP, t // br)

        def block_body(r, carry):
            blocks = [r * per_step + i for i in range(per_step)]
            rows = [pl.ds(pl.multiple_of(b * br, br), br) for b in blocks]
            chains = [(i, d) for i in range(per_step) for d in range(2)]
            gd = [gd_ref[rw, :].astype(BF16) for rw in rows]
            up = [up_ref[d].astype(BF16) for d in range(2)]
            q = [q_ref[rw, :] * q_scale for rw in rows]
            k = [k_ref[rw, :] for rw in rows]
            v = [v_ref[rw, :].astype(BF16) for rw in rows]
            z = [jnp.dot(gd[i], up[d], preferred_element_type=F32) + gb_ref[d] for i, d in chains]
            parts = [_split_bf16(jax.nn.log_sigmoid(zz) * (1.0 / GLA_GATE_NORM)) for zz in z]
            cum = [_sum_dot(tri[d], p) for (i, d), p in zip(chains, parts)]
            cum_end = []
            for (i, d), cm in zip(chains, cum):
                ends = [cm[j * c + c - 1:j * c + c] if d == 0 else cm[j * c:j * c + 1] for j in range(cpb)]
                cum_end.append(jnp.concatenate([jnp.broadcast_to(e, (c, e.shape[1])) for e in ends], axis=0))
            qe = [(q[i] * jnp.exp(cm)).astype(BF16) for (i, d), cm in zip(chains, cum)]
            kinv = [(k[i] * jnp.exp(-cm)).astype(BF16) for (i, d), cm in zip(chains, cum)]
            kdec = [(k[i] * jnp.exp(ce - cm)).astype(BF16) for (i, d), cm, ce in zip(chains, cum, cum_end)]
            a = [lax.dot_general(qq, kk, _NT, preferred_element_type=F32) for qq, kk in zip(qe, kinv)]
            kdec_by_chunk = [jnp.concatenate(
                [jnp.where(chunk_of_row == j, kd, jnp.zeros_like(kd)) for j in range(cpb)], axis=1) for kd in kdec]
            kv = [lax.dot_general(v[i], kd, _TN, preferred_element_type=F32)
                  for (i, d), kd in zip(chains, kdec_by_chunk)]
            a = [jnp.where(keep[d], aa, 0.0).astype(BF16) for (i, d), aa in zip(chains, a)]
            o = [jnp.dot(aa, v[i], preferred_element_type=F32) for (i, d), aa in zip(chains, a)]
            for n, (i, d) in enumerate(chains):
                qe_ref[d, rows[i], :] = qe[n]
                edec_ref[d, rows[i], :] = jnp.exp(cum_end[n])
                o_refs[d][rows[i], :] = o[n]
                for j in range(cpb):
                    kv_ref[d, blocks[i] * cpb + j] = kv[n][:, j * GLA_DK:(j + 1) * GLA_DK]
            return carry

        lax.fori_loop(0, t // (br * per_step), block_body, 0)

        def scan_body(n, carry):
            ms = (n, nc - 1 - n)
            rows = [pl.ds(pl.multiple_of(m * c, c), c) for m in ms]
            st = [st_ref[d] for d in range(2)]
            inter = [lax.dot_general(qe_ref[d, rows[d], :], st[d].astype(BF16), _NT, preferred_element_type=F32)
                     for d in range(2)]
            for d in range(2):
                st_ref[d] = st[d] * edec_ref[d, pl.ds(pl.multiple_of(ms[d] * c, c), 1), :] + kv_ref[d, ms[d]]
            for d in range(2):
                o_refs[d][rows[d], :] += inter[d]
            return carry

        lax.fori_loop(0, nc, scan_body, 0, unroll=GLA_UNROLL)

        def out_body(n, carry):
            rows = pl.ds(pl.multiple_of(n * c, c), c)
            o = of_ref[rows, :] + ob_ref[rows, :]
            y = o * lax.rsqrt(jnp.mean(o * o, axis=-1, keepdims=True) + EPS)
            y = (y * gain) * jax.nn.silu(og_ref[rows, :])
            out_ref[rows, :] = y.astype(out_ref.dtype)
            return carry

        lax.fori_loop(0, nc, out_body, 0, unroll=GLA_UNROLL)

    segment(qc_ref, kc_ref, vc_ref, ogc_ref, gdc_ref, ac_ref, t_ctx)
    segment(ql_ref, kl_ref, vl_ref, ogl_ref, gdl_ref, al_ref, t_lat)


def _gla(pa, pg, gate_up_pad, gate_bias, gla_norm, *, batch):
    t_lat, t_ctx = SEQ, CTX_LEN
    ctx0 = batch * t_lat // t_ctx
    dk, dv = GLA_DK, GLA_DV
    nc = t_lat // GLA_CHUNK

    def lat(width, col0):
        return pl.BlockSpec((t_lat, width), lambda b, h: (b, col0 // width + h))

    def ctx(width, col0):
        return pl.BlockSpec((t_ctx, width), lambda b, h: (ctx0 + b, col0 // width + h))

    lat_gd = pl.BlockSpec((t_lat, 128), lambda b, h: (b, 0))
    ctx_gd = pl.BlockSpec((t_ctx, 128), lambda b, h: (ctx0 + b, 0))
    blocks = ((t_lat + t_ctx) * (3 * dk + 2 * dv) * 4 + _nbytes((2, 128, dk), F32)
              + (t_lat + t_ctx) * dv * 2)
    scratch_shapes = [pltpu.VMEM((2, t_lat, dk), BF16), pltpu.VMEM((2, t_lat, dk), F32),
                      pltpu.VMEM((t_lat, dv), F32), pltpu.VMEM((t_lat, dv), F32),
                      pltpu.VMEM((2, nc, dv, dk), F32), pltpu.VMEM((2, dv, dk), F32)]
    scratch = (_nbytes((2, t_lat, dk), BF16) + _nbytes((2, t_lat, dk), F32) + 2 * _nbytes((t_lat, dv), F32)
               + _nbytes((2, nc + 1, dv, dk), F32))
    return pl.pallas_call(
        functools.partial(_gla_kernel, t_lat=t_lat, t_ctx=t_ctx),
        grid=(batch, GLA_HEADS),
        in_specs=[lat(dk, A_GQ), lat(dk, A_GK), lat(dv, A_GV), lat(dv, A_GO), lat_gd,
                  ctx(dk, A_GQ), ctx(dk, A_GK), ctx(dv, A_GV), ctx(dv, A_GO), ctx_gd,
                  pl.BlockSpec((2, 128, dk), lambda b, h: (0, 0, h)),
                  pl.BlockSpec((2, 1, dk), lambda b, h: (0, 0, h)),
                  pl.BlockSpec((1, dv), lambda b, h: (0, 0))],
        out_specs=[pl.BlockSpec((t_lat, dv), lambda b, h: (b, h)),
                   pl.BlockSpec((t_ctx, dv), lambda b, h: (b, h))],
        out_shape=[jax.ShapeDtypeStruct((batch * t_lat, GLA_HEADS * dv), BF16),
                   jax.ShapeDtypeStruct((batch * t_ctx, GLA_HEADS * dv), BF16)],
        scratch_shapes=scratch_shapes,
        compiler_params=pltpu.CompilerParams(
            dimension_semantics=("arbitrary", "arbitrary"),
            vmem_limit_bytes=_vmem_limit(blocks, scratch, 6 << 20)),
        name="gla",
    )(pa, pa, pa, pa, pg, pa, pa, pa, pa, pg, gate_up_pad, gate_bias, gla_norm)


def _rms(x, gain):
    return (x * lax.rsqrt(jnp.mean(x * x, axis=-1, keepdims=True) + EPS)) * gain


def _rope(x, cos, sin_signed):
    lane = lax.broadcasted_iota(jnp.int32, x.shape, 1)
    partner = jnp.where((lane % 64) < 32, pltpu.roll(x, 96, 1), pltpu.roll(x, 32, 1))
    return x * cos + partner * sin_signed


def _q_row_block(b, qb, *, batch, n_lat, n_ctx):
    return jnp.where(qb < n_lat, b * n_lat + qb, batch * n_lat + b * n_ctx + (qb - n_lat))


KV_PREP_ROWS = 256


def _swa_window_masks(t_lat):
    span = 3 * QBLK
    q_off = np.array([0, QBLK, 2 * QBLK])[:, None, None] + np.arange(QBLK)[None, :, None]
    k_off = np.arange(span)[None, None, :]
    return jnp.asarray(np.where(np.abs(k_off - q_off) <= SWA_WINDOW, 0.0, NEG_INF), F32)


def _swa_mask_variant(qb, n_lat):
    return jnp.where(qb == 0, 0, jnp.where(qb >= n_lat - 1, 2, 1))


def _swa_kernel(sink_ref, q_ref, kl_ref, vl_ref, kc_ref, vc_ref, cq_ref, sq_ref, ck_ref, sk_ref,
                qg_ref, kg_ref, mask_ref, o_ref, kn_ref, vb_ref, kcn_ref, vcb_ref, *, t_lat, n_lat):
    qb = pl.program_id(1)
    g = SWA_GROUP
    dh = HEAD_DIM
    scale = dh ** -0.5

    @pl.when(qb == 0)
    def _prep():
        k_gain = kg_ref[...]

        def body(r, carry):
            r0 = pl.multiple_of(r * KV_PREP_ROWS, KV_PREP_ROWS)
            rows = pl.ds(r0, KV_PREP_ROWS)
            cos, sin = ck_ref[rows, :], sk_ref[rows, :]
            for kv in range(SWA_KV_HEADS):
                cols = slice(kv * dh, (kv + 1) * dh)
                kn_ref[rows, cols] = _rope(_rms(kl_ref[rows, cols], k_gain), cos, sin).astype(BF16)
            vb_ref[rows, :] = vl_ref[rows, :].astype(BF16)
            return carry

        lax.fori_loop(0, t_lat // KV_PREP_ROWS, body, 0)
        for kv in range(SWA_KV_HEADS):
            cols = slice(kv * dh, (kv + 1) * dh)
            kcn_ref[:, cols] = _rms(kc_ref[:, cols], k_gain).astype(BF16)
        vcb_ref[...] = vc_ref[...].astype(BF16)

    def attend(latent):
        cos, sin = cq_ref[...], sq_ref[...]
        q_gain = qg_ref[...]
        span = 3 * QBLK
        start = pl.multiple_of(jnp.clip((qb - 1) * QBLK, 0, t_lat - span), QBLK)
        kvs = range(SWA_KV_HEADS)
        cols = [slice(kv * dh, (kv + 1) * dh) for kv in kvs]
        qs = []
        for kv in kvs:
            heads = [_rope(_rms(q_ref[:, (kv * g + i) * dh:(kv * g + i + 1) * dh], q_gain), cos, sin).astype(BF16)
                     for i in range(g)]
            qs.append(jnp.concatenate(heads, axis=0))
        s_ctx = [lax.dot_general(qs[kv], kcn_ref[:, cols[kv]], _NT, preferred_element_type=F32) for kv in kvs]
        if latent:
            s_loc = [lax.dot_general(qs[kv], kn_ref[pl.ds(start, span), cols[kv]], _NT,
                                     preferred_element_type=F32) for kv in kvs]
        p_loc, p_ctx, dens = ([[] for _ in kvs] for _ in range(3))
        for i in range(g):
            rows = slice(i * QBLK, (i + 1) * QBLK)
            for kv in kvs:
                sink = sink_ref[kv * g + i]
                sc = s_ctx[kv][rows] * scale
                m = jnp.maximum(jnp.max(sc, axis=-1, keepdims=True), sink)
                if latent:
                    sl = s_loc[kv][rows] * scale + mask_ref[...]
                    m = jnp.maximum(m, jnp.max(sl, axis=-1, keepdims=True))
                    pl_i = jnp.exp(sl - m)
                    p_loc[kv].append(pl_i.astype(BF16))
                pc_i = jnp.exp(sc - m)
                p_ctx[kv].append(pc_i.astype(BF16))
                den = jnp.sum(pc_i, axis=-1, keepdims=True) + jnp.exp(sink - m)
                dens[kv].append(den + jnp.sum(pl_i, axis=-1, keepdims=True) if latent else den)
        o = [jnp.dot(jnp.concatenate(p_ctx[kv], axis=0), vcb_ref[:, cols[kv]], preferred_element_type=F32)
             for kv in kvs]
        if latent:
            o = [o[kv] + jnp.dot(jnp.concatenate(p_loc[kv], axis=0), vb_ref[pl.ds(start, span), cols[kv]],
                                 preferred_element_type=F32) for kv in kvs]
        for kv in kvs:
            for i in range(g):
                o_i = o[kv][i * QBLK:(i + 1) * QBLK, :] / dens[kv][i]
                o_ref[:, (kv * g + i) * dh:(kv * g + i + 1) * dh] = o_i.astype(o_ref.dtype)

    pl.when(qb < n_lat)(functools.partial(attend, True))
    pl.when(qb >= n_lat)(functools.partial(attend, False))


def _swa(pc, sink, cos_tab, sin_tab, q_gain, k_gain, *, batch, with_ctx_out):
    t_lat, t_ctx, dh = SEQ, CTX_LEN, HEAD_DIM
    qw, kw = SWA_HEADS * dh, SWA_KV_HEADS * dh
    n_lat, n_ctx = t_lat // QBLK, t_ctx // QBLK
    n_q = n_lat + (n_ctx if with_ctx_out else 0)
    rows_out = batch * (t_lat + (t_ctx if with_ctx_out else 0))
    ctx0 = batch * t_lat // t_ctx
    qmap = functools.partial(_q_row_block, batch=batch, n_lat=n_lat, n_ctx=n_ctx)
    blocks = (_nbytes((QBLK, qw), F32) + 2 * _nbytes((t_lat + t_ctx, kw), F32) + 2 * _nbytes((QBLK, dh), F32)
              + 2 * _nbytes((t_lat, dh), F32) + _nbytes((QBLK, qw), BF16))
    scratch = 2 * _nbytes((t_lat + t_ctx, kw), BF16)
    return pl.pallas_call(
        functools.partial(_swa_kernel, t_lat=t_lat, n_lat=n_lat),
        grid=(batch, n_q),
        in_specs=[
            pl.BlockSpec(memory_space=pltpu.SMEM),
            pl.BlockSpec((QBLK, qw), lambda b, qb: (qmap(b, qb), C_SQ // qw)),
            pl.BlockSpec((t_lat, kw), lambda b, qb: (b, C_SK // kw)),
            pl.BlockSpec((t_lat, kw), lambda b, qb: (b, C_SV // kw)),
            pl.BlockSpec((t_ctx, kw), lambda b, qb: (ctx0 + b, C_SK // kw)),
            pl.BlockSpec((t_ctx, kw), lambda b, qb: (ctx0 + b, C_SV // kw)),
            pl.BlockSpec((QBLK, dh), lambda b, qb: (qb, 0)),
            pl.BlockSpec((QBLK, dh), lambda b, qb: (qb, 0)),
            pl.BlockSpec((t_lat, dh), lambda b, qb: (0, 0)),
            pl.BlockSpec((t_lat, dh), lambda b, qb: (0, 0)),
            pl.BlockSpec((1, dh), lambda b, qb: (0, 0)),
            pl.BlockSpec((1, dh), lambda b, qb: (0, 0)),
            pl.BlockSpec((None, QBLK, 3 * QBLK), lambda b, qb: (_swa_mask_variant(qb, n_lat), 0, 0)),
        ],
        out_specs=pl.BlockSpec((QBLK, qw), lambda b, qb: (qmap(b, qb), 0)),
        out_shape=jax.ShapeDtypeStruct((rows_out, qw), BF16),
        scratch_shapes=[pltpu.VMEM((t_lat, kw), BF16), pltpu.VMEM((t_lat, kw), BF16),
                        pltpu.VMEM((t_ctx, kw), BF16), pltpu.VMEM((t_ctx, kw), BF16)],
        compiler_params=pltpu.CompilerParams(
            dimension_semantics=("arbitrary", "arbitrary"),
            vmem_limit_bytes=_vmem_limit(blocks, scratch, 12 << 20)),
        name="swa",
    )(sink, pc, pc, pc, pc, pc, cos_tab, sin_tab, cos_tab, sin_tab, q_gain, k_gain, _swa_window_masks(t_lat))


NA_HEADS_PER_STEP = 4


def _na_slab_start(qb):
    r = qb * (QBLK // GRID_W)
    rows = SEQ // GRID_W
    return jnp.minimum(jnp.clip(r - NA_KH // 2, 0, rows - NA_KH), rows - NA_SLAB_ROWS)


def _na_bias_variant(qb, n_lat):
    return jnp.where(qb < 2, qb, jnp.where(qb < n_lat - 2, 2, jnp.minimum(qb, n_lat - 1) - (n_lat - 5)))


def _na_kernel(q_ref, kl_ref, vl_ref, kc_ref, vc_ref, bias_ref, qg_ref, kg_ref, o_ref,
               kn_ref, vb_ref, kcn_ref, vcb_ref, *, t_lat, n_lat):
    qb = pl.program_id(2)
    dh = HEAD_DIM
    scale = dh ** -0.5

    @pl.when(qb == 0)
    def _prep():
        k_gain = kg_ref[...]

        def body(r, carry):
            r0 = pl.multiple_of(r * KV_PREP_ROWS, KV_PREP_ROWS)
            rows = pl.ds(r0, KV_PREP_ROWS)
            for h in range(NA_HEADS_PER_STEP):
                cols = slice(h * dh, (h + 1) * dh)
                kn_ref[rows, cols] = _rms(kl_ref[rows, cols], k_gain).astype(BF16)
            vb_ref[rows, :] = vl_ref[rows, :].astype(BF16)
            return carry

        lax.fori_loop(0, t_lat // KV_PREP_ROWS, body, 0)
        for h in range(NA_HEADS_PER_STEP):
            cols = slice(h * dh, (h + 1) * dh)
            kcn_ref[:, cols] = _rms(kc_ref[:, cols], k_gain).astype(BF16)
        vcb_ref[...] = vc_ref[...].astype(BF16)

    def attend(latent):
        q_gain = qg_ref[...]
        heads = range(NA_HEADS_PER_STEP)
        cols = [slice(h * dh, (h + 1) * dh) for h in heads]
        span = NA_SLAB_ROWS * GRID_W
        start = pl.multiple_of(_na_slab_start(qb) * GRID_W, GRID_W)
        qn = [_rms(q_ref[:, cols[h]], q_gain).astype(BF16) for h in heads]
        s_ctx = [lax.dot_general(qn[h], kcn_ref[:, cols[h]], _NT, preferred_element_type=F32) * scale
                 for h in heads]
        m = [jnp.max(s, axis=-1, keepdims=True) for s in s_ctx]
        if latent:
            s_loc = [lax.dot_general(qn[h], kn_ref[pl.ds(start, span), cols[h]], _NT,
                                     preferred_element_type=F32) for h in heads]
            s_loc = [s_loc[h] * scale + bias_ref[h] for h in heads]
            m = [jnp.maximum(m[h], jnp.max(s_loc[h], axis=-1, keepdims=True)) for h in heads]
            p_loc = [jnp.exp(s_loc[h] - m[h]) for h in heads]
        p_ctx = [jnp.exp(s_ctx[h] - m[h]) for h in heads]
        o = [jnp.dot(p_ctx[h].astype(BF16), vcb_ref[:, cols[h]], preferred_element_type=F32) for h in heads]
        den = [jnp.sum(p_ctx[h], axis=-1, keepdims=True) for h in heads]
        if latent:
            o = [o[h] + jnp.dot(p_loc[h].astype(BF16), vb_ref[pl.ds(start, span), cols[h]],
                                preferred_element_type=F32) for h in heads]
            den = [den[h] + jnp.sum(p_loc[h], axis=-1, keepdims=True) for h in heads]
        for h in heads:
            o_ref[:, cols[h]] = (o[h] / den[h]).astype(o_ref.dtype)

    pl.when(qb < n_lat)(functools.partial(attend, True))
    pl.when(qb >= n_lat)(functools.partial(attend, False))


def _na(pc, bias_tab, q_gain, k_gain, *, batch, with_ctx_out):
    t_lat, t_ctx, dh = SEQ, CTX_LEN, HEAD_DIM
    hb = NA_HEADS_PER_STEP
    w = hb * dh
    n_lat, n_ctx = t_lat // QBLK, t_ctx // QBLK
    n_q = n_lat + (n_ctx if with_ctx_out else 0)
    rows_out = batch * (t_lat + (t_ctx if with_ctx_out else 0))
    ctx0 = batch * t_lat // t_ctx
    span = NA_SLAB_ROWS * GRID_W
    qmap = functools.partial(_q_row_block, batch=batch, n_lat=n_lat, n_ctx=n_ctx)
    blocks = (_nbytes((QBLK, w), F32) + 2 * _nbytes((t_lat + t_ctx, w), F32) + _nbytes((hb, QBLK, span), F32)
              + _nbytes((QBLK, w), BF16))
    scratch = 2 * _nbytes((t_lat + t_ctx, w), BF16)
    return pl.pallas_call(
        functools.partial(_na_kernel, t_lat=t_lat, n_lat=n_lat),
        grid=(batch, NA_HEADS // hb, n_q),
        in_specs=[
            pl.BlockSpec((QBLK, w), lambda b, hg, qb: (qmap(b, qb), C_NQ // w + hg)),
            pl.BlockSpec((t_lat, w), lambda b, hg, qb: (b, C_NK // w + hg)),
            pl.BlockSpec((t_lat, w), lambda b, hg, qb: (b, C_NV // w + hg)),
            pl.BlockSpec((t_ctx, w), lambda b, hg, qb: (ctx0 + b, C_NK // w + hg)),
            pl.BlockSpec((t_ctx, w), lambda b, hg, qb: (ctx0 + b, C_NV // w + hg)),
            pl.BlockSpec((hb, None, QBLK, span), lambda b, hg, qb: (hg, _na_bias_variant(qb, n_lat), 0, 0)),
            pl.BlockSpec((1, dh), lambda b, hg, qb: (0, 0)),
            pl.BlockSpec((1, dh), lambda b, hg, qb: (0, 0)),
        ],
        out_specs=pl.BlockSpec((QBLK, w), lambda b, hg, qb: (qmap(b, qb), hg)),
        out_shape=jax.ShapeDtypeStruct((rows_out, NA_HEADS * dh), BF16),
        scratch_shapes=[pltpu.VMEM((t_lat, w), BF16), pltpu.VMEM((t_lat, w), BF16),
                        pltpu.VMEM((t_ctx, w), BF16), pltpu.VMEM((t_ctx, w), BF16)],
        compiler_params=pltpu.CompilerParams(
            dimension_semantics=("arbitrary", "arbitrary", "arbitrary"),
            vmem_limit_bytes=_vmem_limit(blocks, scratch, 12 << 20)),
        name="natten",
    )(pc, pc, pc, pc, pc, bias_tab, q_gain, k_gain)


def _rope_tables():
    quarter = HEAD_DIM // 4
    pos = jnp.arange(SEQ)
    rows = (pos // GRID_W).astype(F32)
    cols = (pos % GRID_W).astype(F32)
    inv = ROPE_BASE ** (-jnp.arange(quarter, dtype=F32) / quarter)
    ang_r = rows[:, None] * inv[None, :]
    ang_c = cols[:, None] * inv[None, :]
    cos = jnp.concatenate([jnp.cos(ang_r), jnp.cos(ang_r), jnp.cos(ang_c), jnp.cos(ang_c)], axis=-1)
    sin = jnp.concatenate([-jnp.sin(ang_r), jnp.sin(ang_r), -jnp.sin(ang_c), jnp.sin(ang_c)], axis=-1)
    cos = jnp.concatenate([cos, jnp.ones((CTX_LEN, HEAD_DIM), F32)], axis=0)
    sin = jnp.concatenate([sin, jnp.zeros((CTX_LEN, HEAD_DIM), F32)], axis=0)
    return cos, sin


def _na_bias_tables(rpb):
    n_layers, n_heads = rpb.shape[:2]
    rows = SEQ // GRID_W
    n_lat = SEQ // QBLK
    rq_per = QBLK // GRID_W
    reps = np.array([0, 1, 2, n_lat - 2, n_lat - 1])
    r = reps * rq_per
    start = np.minimum(np.clip(r - NA_KH // 2, 0, rows - NA_KH), rows - NA_SLAB_ROWS)
    rq = r[:, None] + np.arange(rq_per)[None, :]
    kr = start[:, None] + np.arange(NA_SLAB_ROWS)[None, :]
    r0 = np.clip(rq - NA_KH // 2, 0, rows - NA_KH)
    valid_r = (kr[:, None, :] >= r0[:, :, None]) & (kr[:, None, :] < r0[:, :, None] + NA_KH)
    dr = np.clip(kr[:, None, :] - rq[:, :, None] + NA_KH - 1, 0, 2 * NA_KH - 2)
    qc = np.arange(GRID_W)
    kc = np.arange(GRID_W)
    cs = np.clip(qc - NA_KW // 2, 0, GRID_W - NA_KW)
    valid_c = (kc[None, :] >= cs[:, None]) & (kc[None, :] < cs[:, None] + NA_KW)
    dc = np.clip(kc[None, :] - qc[:, None] + NA_KW - 1, 0, 2 * NA_KW - 2)
    pick_r = np.eye(2 * NA_KH - 1, dtype=np.float32)[dr.reshape(-1)]
    pick_c = np.eye(2 * NA_KW - 1, dtype=np.float32)[dc.reshape(-1)]
    t = jnp.einsum("nr,lhrc->lhnc", pick_r, rpb.astype(F32), precision=HIGHEST)
    t = jnp.einsum("lhnc,xc->lhnx", t, pick_c, precision=HIGHEST)
    t = t.reshape(n_layers, n_heads, len(reps), rq_per, NA_SLAB_ROWS, GRID_W, GRID_W)
    t = t.transpose(0, 1, 2, 3, 5, 4, 6)
    valid = valid_r[:, :, None, :, None] & valid_c[None, None, :, None, :]
    t = jnp.where(valid[None, None], t, NEG_INF)
    return t.reshape(n_layers, n_heads, len(reps), QBLK, NA_SLAB_ROWS * GRID_W)


SUBLANES = 8
CAST_K = 2048
CAST_CHUNK = 256


def _cast_cols_kernel(main_ref, next_ref, o_ref, *, shift):
    bn = o_ref.shape[1]
    for c in range(o_ref.shape[0] // CAST_CHUNK):
        cols = slice(c * CAST_CHUNK, (c + 1) * CAST_CHUNK)
        if shift == 0:
            x = main_ref[:, cols]
        else:
            x = jnp.concatenate([main_ref[shift:bn, cols], next_ref[:, cols]], axis=0)
        o_ref[cols, :] = x.T.astype(o_ref.dtype)


def _cast_cols(w_t, n_layers, col0, n, *, bn):
    k = w_t.shape[2]
    shift = col0 % bn
    base = col0 - shift
    nxt = shift if shift else SUBLANES
    assert n % bn == 0 and k % CAST_K == 0 and nxt % SUBLANES == 0 and bn % nxt == 0 and base % nxt == 0
    blocks = _nbytes((bn + nxt, CAST_K), F32) + _nbytes((CAST_K, bn), BF16)
    return pl.pallas_call(
        functools.partial(_cast_cols_kernel, shift=shift),
        grid=(n_layers, k // CAST_K, n // bn),
        in_specs=[
            pl.BlockSpec((None, bn, CAST_K), lambda l, i, j: (l, base // bn + j, i)),
            pl.BlockSpec((None, nxt, CAST_K), lambda l, i, j: (l, (base + (j + 1) * bn) // nxt, i)),
        ],
        out_specs=pl.BlockSpec((None, CAST_K, bn), lambda l, i, j: (l, i, j)),
        out_shape=jax.ShapeDtypeStruct((n_layers, k, n), BF16),
        compiler_params=pltpu.CompilerParams(
            dimension_semantics=("arbitrary", "arbitrary", "arbitrary"),
            vmem_limit_bytes=_vmem_limit(blocks, 0, 4 << 20)),
        name="cast_cols",
    )(w_t, w_t)


def _pad_gate_up(gate_up):
    r = gate_up.shape[1]
    out = jnp.zeros((2, 128, gate_up.shape[2]), F32)
    out = out.at[0, 0:r].set(gate_up[0])
    out = out.at[1, r:2 * r].set(gate_up[1])
    return out


BM = 1024
BM_SMALL = 256
BM_NORM = 512
BN = 512
BN_FF = 512
BN_A = 640
BN_QKVG = 1024
BN_C = 768
BN_MERGE = 512


def kernel(x, c, ctx, c_ctx, ada_down, ada_up, ada_bias, norm_gain, ffn_w_in, ffn_w_out, w_in,
           gla_gate_up, gla_gate_bias, gla_norm, swa_q_norm, swa_k_norm, swa_sink, na_q_norm,
           na_k_norm, na_rpb, w_branch, w_out):
    batch, seq, d = x.shape
    depth = ada_down.shape[0]
    assert (seq, d, ctx.shape[1]) == (SEQ, D_MODEL, CTX_LEN) and batch + 1 <= MOD_ROWS
    assert w_in.shape[2] == C_START + C_QKV_COLS + GATE_COLS
    lat_rows = batch * seq
    all_rows = lat_rows + batch * ctx.shape[1]

    v8 = jnp.concatenate([c, c_ctx[None], jnp.zeros((MOD_ROWS - batch - 1, d), F32)], axis=0)
    mods = _adaln(v8, ada_down, ada_up, ada_bias)
    mods = mods.reshape(depth, N_MOD, MOD_ROWS, 1, d)
    cos_tab, sin_tab = _rope_tables()
    na_bias = _na_bias_tables(na_rpb)

    ffn_in_f = ffn_w_in.reshape(depth * 2, d, -1)
    ffn_out_f = ffn_w_out.reshape(depth * 2, -1, d)
    w_branch_f = w_branch.reshape(depth, N_BRANCH * BRANCH_W, d)
    ffn_in_a = ffn_in_f[0:1].astype(BF16)
    ffn_out_a = ffn_out_f[0:1].astype(BF16)
    w_branch_bf = w_branch_f[0:1].astype(BF16)
    w_out_bf = w_out[0:1].astype(BF16)
    w_in_t = jnp.swapaxes(w_in, 1, 2)
    w_a_bf = _cast_cols(w_in_t, 1, 0, A_COLS, bn=BN_A)
    w_c_bf = _cast_cols(w_in_t, 1, C_START, C_QKV_COLS + GATE_COLS, bn=BN_C)

    h, h_ctx = x.reshape(lat_rows, d), ctx.reshape(-1, d)
    bm = BM if all_rows % BM == 0 and lat_rows % BM == 0 else BM_SMALL
    mm = dict(seq=seq, bm=bm)
    norm = dict(seq=seq, bm=BM_NORM if all_rows % BM_NORM == 0 and lat_rows % BM_NORM == 0 else BM_SMALL)

    for l in range(depth):
        last = l == depth - 1
        rows_out = lat_rows if last else all_rows
        gain = norm_gain[l].reshape(3, 1, d)
        m = mods[l]

        def next_layer(src, idx):
            return None if last else (src, idx)

        xn = _modulate(h, all_rows, gain[0], m[0], m[1], h_ctx=h_ctx, **norm)
        g1, ffn_in_b = _swiglu(xn, ffn_in_a, 0, all_rows, bm=bm, bn=BN_FF, side=(ffn_in_f, 2 * l + 1))
        h, ffn_out_b = _resid_matmul(g1, ffn_out_a, 0, h, all_rows, m[2], weight=MACARON_W, bn=BN,
                                     side=(ffn_out_f, 2 * l + 1), h_ctx=h_ctx, **mm)
        h_ctx = None

        xn = _modulate(h, all_rows, gain[1], m[3], m[4], **norm)
        pa, w_a_next = _matmul(xn, w_a_bf, 0, all_rows, 0, A_GD, bm=bm, bn=BN_QKVG, out_dtype=F32,
                               side_t=None if last else (w_in_t, l + 1, 0, A_COLS))
        pg, _ = _matmul(xn, w_a_bf, 0, all_rows, A_GD, A_COLS - A_GD, bm=bm, bn=A_COLS - A_GD, out_dtype=F32)
        pc, _ = _matmul(xn, w_c_bf, 0, all_rows, 0, C_QKV_COLS, bm=bm, bn=BN_C, out_dtype=F32)
        gates, w_c_next = _matmul(xn, w_c_bf, 0, rows_out, C_QKV_COLS, GATE_COLS, bm=bm, bn=BN_C,
                                  out_dtype=BF16, sigmoid=True,
                                  side_t=None if last else (w_in_t, l + 1, C_START, C_QKV_COLS + GATE_COLS))
        a_lat, a_ctx = _gla(pa, pg, _pad_gate_up(gla_gate_up[l]), gla_gate_bias[l].reshape(2, 1, -1),
                            gla_norm[l].reshape(1, -1), batch=batch)
        o_b = _swa(pc, swa_sink[l], cos_tab, sin_tab, swa_q_norm[l].reshape(1, -1), swa_k_norm[l].reshape(1, -1),
                   batch=batch, with_ctx_out=not last)
        o_c = _na(pc, na_bias[l], na_q_norm[l].reshape(1, -1), na_k_norm[l].reshape(1, -1),
                  batch=batch, with_ctx_out=not last)
        y, w_branch_next = _merge(a_lat, o_b, o_c, w_branch_bf.reshape(1, N_BRANCH, BRANCH_W, d), 0, gates, rows_out,
                                  bm=bm, bn=BN_MERGE, side=next_layer(w_branch_f, l + 1),
                                  oa_ctx=None if last else a_ctx)
        h, w_out_next = _resid_matmul(y, w_out_bf, 0, h, rows_out, m[5], weight=1.0, bn=BN,
                                      side=next_layer(w_out, l + 1), **mm)

        xn = _modulate(h, rows_out, gain[2], m[6], m[7], **norm)
        g2, ffn_in_a = _swiglu(xn, ffn_in_b, 0, rows_out, bm=bm, bn=BN_FF, side=next_layer(ffn_in_f, 2 * l + 2))
        h, ffn_out_a = _resid_matmul(g2, ffn_out_b, 0, h, rows_out, m[8], weight=MACARON_W, bn=BN,
                                     side=next_layer(ffn_out_f, 2 * l + 2), **mm)
        w_branch_bf, w_out_bf, w_a_bf, w_c_bf = w_branch_next, w_out_next, w_a_next, w_c_next

    return h.reshape(batch, seq, d)
```
